```python
import math
import jax
import jax.numpy as jnp
from jax import lax
import numpy as np

D_MODEL = 1024
BATCH = 16
SEQ = 4096
DEPTH = 2
DEC_BATCH = 4
DEC_SEQ = 4096
PAST_LEN = 128

HEAD_DIM = 64
ATTN_PATTERNS = ((128, 1), (512, 4), (2048, 16))
N_PATTERNS = len(ATTN_PATTERNS)
ATTN_HEADS_PER_GROUP = 4
N_ATTN_HEADS = N_PATTERNS * ATTN_HEADS_PER_GROUP
ATTN_WIDTH = N_ATTN_HEADS * HEAD_DIM
ATTN_OUT = ATTN_HEADS_PER_GROUP * HEAD_DIM
ALIBI_MAX_EXP = 8.0
HYENA_WIDTH = D_MODEL // 4
HYENA_BANDS = 16
HYENA_EMB = 2 * HYENA_BANDS + 1
HYENA_FFN = 64
HYENA_TARGET = 1e-2
HYENA_FAST_DECAY = 0.3
HYENA_SLOW_DECAY = 1.5
IN_PROJ_WIDTH = 3 * ATTN_WIDTH + 3 * HYENA_WIDTH
MIX_OUT_WIDTH = ATTN_OUT + HYENA_WIDTH
RWKV_HEAD = 64
RWKV_HEADS = D_MODEL // RWKV_HEAD
DECAY_LORA = max(32, int(round(1.8 * D_MODEL ** 0.5 / 32)) * 32)
AAA_LORA = max(32, int(round(1.8 * D_MODEL ** 0.5 / 32)) * 32)
GATE_LORA = max(32, int(round(0.6 * D_MODEL ** 0.8 / 32)) * 32)
D_FF = ((8 * D_MODEL) // 3 + 127) // 128 * 128
RMS_EPS = 1e-6
GN_EPS = 64e-5
NEG_INF = -1e30

kernel_name = "hybrid_dilated_hyena_rwkv7_encoder"


def _rmsnorm(x, gain):
    x32 = x.astype(jnp.float32)
    y = x32 * lax.rsqrt(jnp.mean(x32 * x32, axis=-1, keepdims=True) + RMS_EPS)
    return (y * gain.astype(jnp.float32)).astype(x.dtype)


def _adaln_params(c, w, b):
    m = jax.nn.silu(c) @ w + b
    return [t[:, None, :] for t in jnp.split(m, 6, axis=-1)]


def _centred_dwconv3(x, w, b):
    xp = jnp.pad(x, ((0, 0), (1, 1), (0, 0)))
    return xp[:, :-2] * w[0] + xp[:, 1:-1] * w[1] + xp[:, 2:] * w[2] + b


def _alibi_slopes():
    return jnp.exp2(-ALIBI_MAX_EXP * (jnp.arange(N_ATTN_HEADS, dtype=jnp.float32) + 1.0) / N_ATTN_HEADS)


def _dilated_window_attention(q, k, v, slopes, window, dilation):
    B, L, H, Dh = q.shape
    radius = window // (2 * dilation)
    blk = radius
    n = L // dilation
    nb = -(-n // blk)
    n_pad = nb * blk
    bd = B * dilation

    def to_classes(t):
        return t.reshape(B, n, dilation, H, Dh).transpose(0, 2, 1, 3, 4).reshape(bd, n, H, Dh)

    def key_windows(t):
        tp = jnp.pad(to_classes(t), ((0, 0), (radius, n_pad - n + radius), (0, 0), (0, 0)))
        tp = tp.reshape(bd, nb + 2, blk, H, Dh)
        return jnp.concatenate([tp[:, :-2], tp[:, 1:-1], tp[:, 2:]], axis=2)

    qc = jnp.pad(to_classes(q), ((0, 0), (0, n_pad - n), (0, 0), (0, 0))).reshape(bd, nb, blk, H, Dh)
    kw = key_windows(k)
    vw = key_windows(v)
    s = jnp.einsum('bnqhd,bnkhd->bnhqk', qc, kw, preferred_element_type=jnp.float32) * (Dh ** -0.5)
    qi = jnp.arange(blk)
    kj = jnp.arange(3 * blk)
    rel = kj[None, :] - radius - qi[:, None]
    kpos = jnp.arange(nb)[:, None] * blk - radius + kj[None, :]
    valid = (jnp.abs(rel)[None] <= radius) & ((kpos >= 0) & (kpos < n))[:, None, :]
    alibi = -slopes[:, None, None] * (jnp.abs(rel) * dilation).astype(jnp.float32)[None]
    s = jnp.where(valid[None, :, None], s + alibi[None, None], NEG_INF)
    m = jnp.max(s, axis=-1, keepdims=True)
    p = jnp.exp(s - m)
    den = jnp.sum(p, axis=-1, keepdims=True)
    o = jnp.einsum('bnhqk,bnkhd->bnqhd', p / den, vw)
    lse = jnp.transpose((m + jnp.log(den))[..., 0], (0, 1, 3, 2))

    def from_classes(t):
        t = t.reshape((bd, n_pad) + t.shape[3:])[:, :n]
        t = t.reshape((B, dilation, n) + t.shape[2:])
        t = jnp.swapaxes(t, 1, 2)
        return t.reshape((B, L) + t.shape[3:])

    return from_classes(o), from_classes(lse)


def _hyena_positional_features(L):
    t = jnp.linspace(0.0, 1.0, L, dtype=jnp.float32)[:, None]
    w = 2.0 * math.pi * jnp.arange(L, dtype=jnp.float32)[:, None] / L
    f = jnp.linspace(1e-4, HYENA_BANDS - 1, HYENA_BANDS, dtype=jnp.float32)[None, :]
    z = f * w
    return jnp.concatenate([t, jnp.cos(z), -jnp.sin(z)], axis=-1)


def _hyena_two_sided_filter(L, w1, b1, w2, b2, w3, b3, w4, freq):
    z = _hyena_positional_features(L)
    h = jnp.sin(freq * (z @ w1 + b1))
    h = jnp.sin(freq * (h @ w2 + b2))
    h = jnp.sin(freq * (h @ w3 + b3))
    h = (h @ w4).astype(jnp.float32).reshape(L, 2, HYENA_WIDTH)
    t = jnp.linspace(0.0, 1.0, L, dtype=jnp.float32)[:, None]
    max_decay = math.log(HYENA_TARGET) / HYENA_FAST_DECAY
    min_decay = math.log(HYENA_TARGET) / HYENA_SLOW_DECAY
    deltas = jnp.linspace(min_decay, max_decay, HYENA_WIDTH, dtype=jnp.float32)[None, :]
    window = jnp.exp(-t * jnp.abs(deltas))
    h_fwd = h[:, 0] * window
    h_bwd = h[:, 1] * window
    two_sided = jnp.concatenate([h_fwd, jnp.zeros((1, HYENA_WIDTH), jnp.float32), h_bwd[:0:-1]], axis=0)
    return two_sided / jnp.sum(jnp.abs(two_sided), axis=0, keepdims=True)


def _hyena(u, short_w, short_b, w1, b1, w2, b2, w3, b3, w4, freq, filt_bias):
    B, L, _ = u.shape
    u = _centred_dwconv3(u, short_w, short_b)
    x0, x1, v = jnp.split(u, 3, axis=-1)
    z = (v * x1).astype(jnp.float32)
    filt = _hyena_two_sided_filter(L, w1, b1, w2, b2, w3, b3, w4, freq)
    n_fft = 2 * L
    zf = jnp.fft.rfft(z, n=n_fft, axis=1)
    ff = jnp.fft.rfft(filt, n=n_fft, axis=0)
    conv = jnp.fft.irfft(zf * ff[None], n=n_fft, axis=1)[:, :L]
    z = conv + z * filt_bias.astype(jnp.float32)
    return z * x0.astype(jnp.float32)


def _attn_hyena_mixer(h, p):
    B, L, _ = h.shape
    proj = h @ p['w_in']
    qkv = proj[..., :3 * ATTN_WIDTH].reshape(B, L, 3, N_PATTERNS, ATTN_HEADS_PER_GROUP, HEAD_DIM)
    hy = proj[..., 3 * ATTN_WIDTH:]
    slopes = _alibi_slopes().reshape(N_PATTERNS, ATTN_HEADS_PER_GROUP)
    outs = []
    lses = []
    for g, (window, dilation) in enumerate(ATTN_PATTERNS):
        o, lse = _dilated_window_attention(qkv[:, :, 0, g], qkv[:, :, 1, g], qkv[:, :, 2, g],
                                           slopes[g], window, dilation)
        outs.append(o)
        lses.append(lse)
    alpha = jax.nn.softmax(jnp.stack(lses, axis=0), axis=0)
    attn = jnp.sum(alpha[..., None] * jnp.stack(outs, axis=0), axis=0).reshape(B, L, ATTN_OUT)
    hy_out = _hyena(hy, p['short_w'], p['short_b'], p['filt_w1'], p['filt_b1'], p['filt_w2'], p['filt_b2'],
                    p['filt_w3'], p['filt_b3'], p['filt_w4'], p['filt_freq'], p['filt_bias'])
    mixed = jnp.concatenate([attn.astype(h.dtype), hy_out.astype(h.dtype)], axis=-1)
    return mixed @ p['w_out']


def _wkv_scan(r, decay, k, v, kk, a, reverse):
    B, L, H, N = r.shape

    def step(S, inp):
        r_t, w_t, k_t, v_t, kk_t, a_t = inp
        sa = jnp.einsum('bhvk,bhk->bhv', S, -kk_t)
        S = (S * w_t[:, :, None, :] + sa[..., None] * (kk_t * a_t)[:, :, None, :]
             + v_t[..., None] * k_t[:, :, None, :])
        return S, jnp.einsum('bhvk,bhk->bhv', S, r_t)

    xs = tuple(jnp.swapaxes(t, 0, 1) for t in (r, decay, k, v, kk, a))
    S0 = jnp.zeros((B, H, N, N), jnp.float32)
    _, ys = lax.scan(step, S0, xs, reverse=reverse)
    return jnp.swapaxes(ys, 0, 1)


def _rwkv7_bidir(h, p):
    B, L, D = h.shape
    H, N = RWKV_HEADS, RWKV_HEAD

    def heads(t):
        return t.astype(jnp.float32).reshape(B, L, H, N)

    hp = jnp.pad(h, ((0, 0), (1, 1), (0, 0)))
    xx = 0.5 * (hp[:, :-2] + hp[:, 2:]) - h
    xr, xw, xk, xv, xa, xg = [h + xx * p['mu'][i] for i in range(6)]
    r = xr @ p['w_r']
    k = xk @ p['w_k']
    v = xv @ p['w_v']
    g = jax.nn.sigmoid(xg @ p['g1']) @ p['g2']
    kk = heads(k * p['k_k'])
    kk = kk / jnp.maximum(jnp.linalg.norm(kk, axis=-1, keepdims=True), 1e-12)
    ys = []
    ks = []
    for direction in range(2):
        w_log = -jax.nn.softplus(-(p['w0'][direction] + jnp.tanh(xw @ p['w1'][direction]) @ p['w2'][direction])) - 0.5
        decay = jnp.exp(-jnp.exp(w_log.astype(jnp.float32)))
        a = jax.nn.sigmoid(p['a0'][direction] + (xa @ p['a1'][direction]) @ p['a2'][direction])
        k_dir = k * (1.0 + (a - 1.0) * p['k_a'])
        ys.append(_wkv_scan(heads(r), heads(decay), heads(k_dir), heads(v), kk, heads(a),
                            reverse=(direction == 1)))
        ks.append(heads(k_dir))
    y = ys[0] + ys[1]
    mu = jnp.mean(y, axis=-1, keepdims=True)
    var = jnp.mean(jnp.square(y - mu), axis=-1, keepdims=True)
    yn = ((y - mu) * lax.rsqrt(var + GN_EPS)).reshape(B, L, D) * p['ln_w'] + p['ln_b']
    k_mean = 0.5 * (ks[0] + ks[1])
    bonus = jnp.sum(heads(r) * k_mean * p['r_k'], axis=-1, keepdims=True) * heads(v)
    out = (yn + bonus.reshape(B, L, D)) * g
    return out.astype(h.dtype) @ p['w_o']


def _conv_ffn(h, p):
    a, gate = jnp.split(h @ p['ffn_up'], 2, axis=-1)
    a = _centred_dwconv3(a, p['ffn_conv_w'], p['ffn_conv_b'])
    return (jax.nn.gelu(a) * gate) @ p['ffn_down']


def _trunk(x, c, layers, final_norm):
    for i in range(DEPTH):
        p = layers[i]
        sh1, sc1, g1, sh2, sc2, g2 = _adaln_params(c, p['ada_w'], p['ada_b'])
        h = _rmsnorm(x, p['norm1']) * (1.0 + sc1) + sh1
        mix = _attn_hyena_mixer(h, p) if i % 2 == 0 else _rwkv7_bidir(h, p)
        x = x + g1 * mix
        h = _rmsnorm(x, p['norm2']) * (1.0 + sc2) + sh2
        x = x + g2 * _conv_ffn(h, p)
    return _rmsnorm(x, final_norm)


def setup_inputs(seed: int = 0) -> dict:
    key = jax.random.key(seed)
    keys = iter(jax.random.split(key, 96))
    D = D_MODEL

    def nrm(shape, scale):
        return scale * jax.random.normal(next(keys), shape, jnp.float32)

    def gain(shape):
        return 1.0 + nrm(shape, 0.02)

    inp = {}
    inp['x_prompt'] = nrm((BATCH, SEQ, D), 1.0)
    inp['x_sample'] = nrm((DEC_BATCH, DEC_SEQ, D), 1.0)
    inp['c_prompt'] = nrm((BATCH, D), 1.0)
    inp['c_sample'] = nrm((DEC_BATCH, D), 1.0)
    inp['l0_ada_w'] = nrm((D, 6 * D), 0.5 * D ** -0.5)
    inp['l0_ada_b'] = nrm((6 * D,), 0.01)
    inp['l0_norm1'] = gain((D,))
    inp['l0_norm2'] = gain((D,))
    inp['l0_w_in'] = nrm((D, IN_PROJ_WIDTH), D ** -0.5)
    inp['l0_short_w'] = nrm((3, 3 * HYENA_WIDTH), 3 ** -0.5)
    inp['l0_short_b'] = nrm((3 * HYENA_WIDTH,), 0.01)
    inp['l0_filt_w1'] = nrm((HYENA_EMB, HYENA_FFN), HYENA_EMB ** -0.5)
    inp['l0_filt_b1'] = nrm((HYENA_FFN,), 0.1)
    inp['l0_filt_w2'] = nrm((HYENA_FFN, HYENA_FFN), HYENA_FFN ** -0.5)
    inp['l0_filt_b2'] = nrm((HYENA_FFN,), 0.1)
    inp['l0_filt_w3'] = nrm((HYENA_FFN, HYENA_FFN), HYENA_FFN ** -0.5)
    inp['l0_filt_b3'] = nrm((HYENA_FFN,), 0.1)
    inp['l0_filt_w4'] = nrm((HYENA_FFN, 2 * HYENA_WIDTH), HYENA_FFN ** -0.5)
    inp['l0_filt_freq'] = 1.0 + nrm((HYENA_FFN,), 0.1)
    inp['l0_filt_bias'] = nrm((HYENA_WIDTH,), 1.0)
    inp['l0_w_out'] = nrm((MIX_OUT_WIDTH, D), MIX_OUT_WIDTH ** -0.5)
    inp['l0_ffn_up'] = nrm((D, 2 * D_FF), D ** -0.5)
    inp['l0_ffn_conv_w'] = nrm((3, D_FF), 3 ** -0.5)
    inp['l0_ffn_conv_b'] = nrm((D_FF,), 0.01)
    inp['l0_ffn_down'] = nrm((D_FF, D), D_FF ** -0.5)
    inp['l1_ada_w'] = nrm((D, 6 * D), 0.5 * D ** -0.5)
    inp['l1_ada_b'] = nrm((6 * D,), 0.01)
    inp['l1_norm1'] = gain((D,))
    inp['l1_norm2'] = gain((D,))
    inp['l1_mu'] = jax.random.uniform(next(keys), (6, D), jnp.float32)
    inp['l1_w_r'] = nrm((D, D), D ** -0.5)
    inp['l1_w_k'] = nrm((D, D), D ** -0.5)
    inp['l1_w_v'] = nrm((D, D), D ** -0.5)
    inp['l1_w_o'] = nrm((D, D), D ** -0.5)
    inp['l1_w0'] = jax.random.uniform(next(keys), (2, D), jnp.float32, minval=-5.0, maxval=1.0)
    inp['l1_w1'] = nrm((2, D, DECAY_LORA), D ** -0.5)
    inp['l1_w2'] = nrm((2, DECAY_LORA, D), 0.1 * DECAY_LORA ** -0.5)
    inp['l1_a0'] = nrm((2, D), 0.01)
    inp['l1_a1'] = nrm((2, D, AAA_LORA), D ** -0.5)
    inp['l1_a2'] = nrm((2, AAA_LORA, D), 0.1 * AAA_LORA ** -0.5)
    inp['l1_g1'] = nrm((D, GATE_LORA), D ** -0.5)
    inp['l1_g2'] = nrm((GATE_LORA, D), GATE_LORA ** -0.5)
    inp['l1_k_k'] = 0.85 + nrm((D,), 0.02)
    inp['l1_k_a'] = 1.0 + nrm((D,), 0.02)
    inp['l1_r_k'] = nrm((RWKV_HEADS, RWKV_HEAD), 0.1)
    inp['l1_ln_w'] = gain((D,))
    inp['l1_ln_b'] = nrm((D,), 0.01)
    inp['l1_ffn_up'] = nrm((D, 2 * D_FF), D ** -0.5)
    inp['l1_ffn_conv_w'] = nrm((3, D_FF), 3 ** -0.5)
    inp['l1_ffn_conv_b'] = nrm((D_FF,), 0.01)
    inp['l1_ffn_down'] = nrm((D_FF, D), D_FF ** -0.5)
    inp['final_norm'] = gain((D,))
    return inp


def reference(x_prompt, x_sample, c_prompt, c_sample,
              l0_ada_w, l0_ada_b, l0_norm1, l0_norm2, l0_w_in, l0_short_w, l0_short_b,
              l0_filt_w1, l0_filt_b1, l0_filt_w2, l0_filt_b2, l0_filt_w3, l0_filt_b3, l0_filt_w4,
              l0_filt_freq, l0_filt_bias, l0_w_out, l0_ffn_up, l0_ffn_conv_w, l0_ffn_conv_b, l0_ffn_down,
              l1_ada_w, l1_ada_b, l1_norm1, l1_norm2, l1_mu, l1_w_r, l1_w_k, l1_w_v, l1_w_o,
              l1_w0, l1_w1, l1_w2, l1_a0, l1_a1, l1_a2, l1_g1, l1_g2, l1_k_k, l1_k_a, l1_r_k,
              l1_ln_w, l1_ln_b, l1_ffn_up, l1_ffn_conv_w, l1_ffn_conv_b, l1_ffn_down, final_norm):
    layer0 = dict(ada_w=l0_ada_w, ada_b=l0_ada_b, norm1=l0_norm1, norm2=l0_norm2, w_in=l0_w_in,
                  short_w=l0_short_w, short_b=l0_short_b, filt_w1=l0_filt_w1, filt_b1=l0_filt_b1,
                  filt_w2=l0_filt_w2, filt_b2=l0_filt_b2, filt_w3=l0_filt_w3, filt_b3=l0_filt_b3,
                  filt_w4=l0_filt_w4, filt_freq=l0_filt_freq, filt_bias=l0_filt_bias, w_out=l0_w_out,
                  ffn_up=l0_ffn_up, ffn_conv_w=l0_ffn_conv_w, ffn_conv_b=l0_ffn_conv_b, ffn_down=l0_ffn_down)
    layer1 = dict(ada_w=l1_ada_w, ada_b=l1_ada_b, norm1=l1_norm1, norm2=l1_norm2, mu=l1_mu,
                  w_r=l1_w_r, w_k=l1_w_k, w_v=l1_w_v, w_o=l1_w_o, w0=l1_w0, w1=l1_w1, w2=l1_w2,
                  a0=l1_a0, a1=l1_a1, a2=l1_a2, g1=l1_g1, g2=l1_g2, k_k=l1_k_k, k_a=l1_k_a, r_k=l1_r_k,
                  ln_w=l1_ln_w, ln_b=l1_ln_b, ffn_up=l1_ffn_up, ffn_conv_w=l1_ffn_conv_w,
                  ffn_conv_b=l1_ffn_conv_b, ffn_down=l1_ffn_down)
    layers = [layer0, layer1]
    y_prompt = _trunk(x_prompt, c_prompt, layers, final_norm)
    y_sample = _trunk(x_sample, c_sample, layers, final_norm)
    return (y_prompt, y_sample)
```

```python
import functools
import math

import jax
import jax.numpy as jnp
import numpy as np
from jax import lax
from jax.experimental import pallas as pl
from jax.experimental.pallas import tpu as pltpu

F32 = jnp.float32
BF16 = jnp.bfloat16
HIGHEST = lax.Precision.HIGHEST

HEAD_DIM = 64
ATTN_PATTERNS = ((128, 1), (512, 4), (2048, 16))
N_PATTERNS = 3
HEADS_PER_GROUP = 4
N_ATTN_HEADS = 12
ATTN_WIDTH = 768
GROUP_WIDTH = HEADS_PER_GROUP * HEAD_DIM
ALIBI_MAX_EXP = 8.0
HYENA_WIDTH = 256
HYENA_BANDS = 16
HYENA_EMB = 33
HYENA_FFN = 64
HYENA_TARGET = 1e-2
HYENA_FAST_DECAY = 0.3
HYENA_SLOW_DECAY = 1.5
RWKV_HEAD = 64
RMS_EPS = 1e-6
GN_EPS = 64e-5
NEG_INF = -1e30

LANES = 128
SUBLANES = 8
MXU_DIM = 256
VMEM_LIMIT = 56 * 1024 * 1024

ATTN_RADIUS = 64
ATTN_TQ = 128
ATTN_TK = ATTN_TQ + 2 * ATTN_RADIUS
CONV_P = 256
SCAN_C = 64
SCAN_HG = 256


def _cparams(sem):
    return pltpu.CompilerParams(dimension_semantics=sem, vmem_limit_bytes=VMEM_LIMIT)


def _const_spec(shape):
    nd = len(shape)
    return pl.BlockSpec(shape, lambda *_: (0,) * nd, pipeline_mode=pl.Buffered(1))


def _ada_kernel(c_ref, w_ref, b_ref, o_ref):
    c = c_ref[...]
    s = c * jax.nn.sigmoid(c)
    o_ref[...] = jnp.dot(s, w_ref[...], precision=HIGHEST, preferred_element_type=F32) + b_ref[...]


def _ada_params(c_pad, w, b):
    bp, d = c_pad.shape
    n = w.shape[1]
    tn = 1024
    return pl.pallas_call(
        _ada_kernel,
        grid=(n // tn,),
        in_specs=[_const_spec((bp, d)),
                  pl.BlockSpec((d, tn), lambda j: (0, j)),
                  pl.BlockSpec((1, tn), lambda j: (0, j))],
        out_specs=pl.BlockSpec((bp, tn), lambda j: (0, j)),
        out_shape=jax.ShapeDtypeStruct((bp, n), F32),
        compiler_params=_cparams(("parallel",)),
        name="ada_params",
    )(c_pad, w, b.reshape(1, n))


def _norm_mod(x, gain, shift, scale):
    ms = jnp.mean(x * x, axis=-1, keepdims=True)
    y = x * lax.rsqrt(ms + RMS_EPS) * gain
    return y * (1.0 + scale) + shift


def _inproj_kernel(x_ref, mod_ref, gain_ref, w_ref, qkv_ref, hy_ref):
    h = _norm_mod(x_ref[...], gain_ref[...], mod_ref[0, 0:1, :], mod_ref[0, 1:2, :])
    p = jnp.dot(h.astype(BF16), w_ref[...], preferred_element_type=F32)
    nq = qkv_ref.shape[-1]
    qkv_ref[...] = p[:, :nq].astype(BF16)
    hy_ref[...] = p[:, nq:]


def _in_proj(x2, mods, gain, w_bf, seq, tm):
    rows, d = x2.shape
    n = w_bf.shape[1]
    nq = 3 * ATTN_WIDTH
    tpb = seq // tm
    return pl.pallas_call(
        _inproj_kernel,
        grid=(rows // tm,),
        in_specs=[pl.BlockSpec((tm, d), lambda i: (i, 0)),
                  pl.BlockSpec((1, 6, d), lambda i: (i // tpb, 0, 0)),
                  _const_spec((1, d)),
                  _const_spec((d, n))],
        out_specs=[pl.BlockSpec((tm, nq), lambda i: (i, 0)),
                   pl.BlockSpec((tm, n - nq), lambda i: (i, 0))],
        out_shape=[jax.ShapeDtypeStruct((rows, nq), BF16),
                   jax.ShapeDtypeStruct((rows, n - nq), F32)],
        compiler_params=_cparams(("parallel",)),
        name="in_proj",
    )(x2, mods, gain.reshape(1, d), w_bf)


def _attn_kernel(q_ref, k_ref, v_ref, bias_ref, o_ref, lse_ref, kpad, vpad, *, n):
    nq = n // ATTN_TQ
    zeros = jnp.zeros((ATTN_RADIUS, GROUP_WIDTH), BF16)
    kpad[0:ATTN_RADIUS, :] = zeros
    vpad[0:ATTN_RADIUS, :] = zeros
    kpad[n + ATTN_RADIUS:n + 2 * ATTN_RADIUS, :] = zeros
    vpad[n + ATTN_RADIUS:n + 2 * ATTN_RADIUS, :] = zeros
    kpad[ATTN_RADIUS:n + ATTN_RADIUS, :] = k_ref[0]
    vpad[ATTN_RADIUS:n + ATTN_RADIUS, :] = v_ref[0]
    lane = lax.broadcasted_iota(jnp.int32, (1, LANES), 1)
    low = lane < HEAD_DIM

    def body(i, carry):
        r0 = pl.multiple_of(i * ATTN_TQ, ATTN_TQ)
        q = q_ref[0, pl.ds(r0, ATTN_TQ), :]
        kw = kpad[pl.ds(r0, ATTN_TK), :]
        vw = vpad[pl.ds(r0, ATTN_TK), :]
        sel = jnp.where(i == 0, 0, jnp.where(i == nq - 1, 2, 1))
        for pair in range(GROUP_WIDTH // LANES):
            cols = slice(pair * LANES, (pair + 1) * LANES)
            q2, k2, v2 = q[:, cols], kw[:, cols], vw[:, cols]
            outs, lses = [], []
            for hh in range(2):
                hm = low if hh == 0 else jnp.logical_not(low)
                qm = jnp.where(hm, q2, jnp.zeros_like(q2))
                s = lax.dot_general(qm, k2, (((1,), (1,)), ((), ())), preferred_element_type=F32)
                s = s * (HEAD_DIM ** -0.5) + bias_ref[2 * pair + hh, sel]
                m = jnp.max(s, axis=-1, keepdims=True)
                p = jnp.exp(s - m)
                den = jnp.sum(p, axis=-1, keepdims=True)
                o = jnp.dot(p.astype(BF16), v2, preferred_element_type=F32)
                outs.append(o / den)
                lses.append(jnp.broadcast_to(m + jnp.log(den), (ATTN_TQ, LANES)))
            o_ref[0, pl.ds(r0, ATTN_TQ), cols] = jnp.where(low, outs[0], outs[1])
            lse_ref[0, pl.ds(r0, ATTN_TQ), cols] = jnp.where(low, lses[0], lses[1])
        return carry

    lax.fori_loop(0, nq, body, 0)


def _attn_bias(group, dilation):
    slopes = np.exp2(-ALIBI_MAX_EXP * (np.arange(N_ATTN_HEADS, dtype=np.float32) + 1.0) / N_ATTN_HEADS)
    slopes = slopes.reshape(N_PATTERNS, HEADS_PER_GROUP)[group].astype(np.float32)
    qi = np.arange(ATTN_TQ)[:, None]
    kj = np.arange(ATTN_TK)[None, :]
    rel = kj - ATTN_RADIUS - qi
    band = np.abs(rel) <= ATTN_RADIUS
    alibi = -slopes[:, None, None] * (np.abs(rel) * dilation).astype(np.float32)[None]
    kinds = []
    for lo, hi in ((ATTN_RADIUS, ATTN_TK), (0, ATTN_TK), (0, ATTN_TQ + ATTN_RADIUS)):
        valid = band & (kj >= lo) & (kj < hi)
        kinds.append(np.where(valid[None], alibi, np.float32(NEG_INF)))
    return jnp.asarray(np.stack(kinds, axis=1), dtype=F32)


def _attention_group(qkv, group, dilation, batch, seq):
    n = seq // dilation
    assert n % ATTN_TQ == 0 and n >= 2 * ATTN_TQ
    width = 3 * ATTN_WIDTH
    view = qkv.reshape(batch, n, dilation * width)
    cpr = width // GROUP_WIDTH
    cpp = ATTN_WIDTH // GROUP_WIDTH

    def col(part):
        return lambda b, r: (b, 0, r * cpr + part * cpp + group)

    blk = (1, n, GROUP_WIDTH)
    o, lse = pl.pallas_call(
        functools.partial(_attn_kernel, n=n),
        grid=(batch, dilation),
        in_specs=[pl.BlockSpec(blk, col(0)), pl.BlockSpec(blk, col(1)), pl.BlockSpec(blk, col(2)),
                  _const_spec((HEADS_PER_GROUP, 3, ATTN_TQ, ATTN_TK))],
        out_specs=[pl.BlockSpec(blk, lambda b, r: (b, 0, r)),
                   pl.BlockSpec(blk, lambda b, r: (b, 0, r))],
        out_shape=[jax.ShapeDtypeStruct((batch, n, dilation * GROUP_WIDTH), F32)] * 2,
        scratch_shapes=[pltpu.VMEM((n + 2 * ATTN_RADIUS, GROUP_WIDTH), BF16)] * 2,
        compiler_params=_cparams(("parallel", "parallel")),
        name=f"attn_g{group}",
    )(view, view, view, _attn_bias(group, dilation))
    return o.reshape(batch * seq, GROUP_WIDTH), lse.reshape(batch * seq, GROUP_WIDTH)


def _filter_kernel(bands_ref, deltas_ref, w1_ref, b1_ref, w2_ref, b2_ref, w3_ref, b3_ref, w4_ref, freq_ref,
                   ft_ref, *, seq, tl):
    i = pl.program_id(0)
    freq = freq_ref[...]
    lane = lax.broadcasted_iota(jnp.int32, (1, LANES), 1)
    row = lax.broadcasted_iota(jnp.int32, (tl, 1), 0) + i * tl

    def half_filter(pos, half):
        posf = pos.astype(F32)
        t = posf / float(seq - 1)
        z = bands_ref[...] * (2.0 * math.pi * posf / float(seq))
        feat = jnp.where(lane == 0, t,
                         jnp.where(lane <= HYENA_BANDS, jnp.cos(z),
                                   jnp.where(lane <= 2 * HYENA_BANDS, -jnp.sin(z), 0.0)))
        h = jnp.sin(freq * (_mm(feat, w1_ref[...]) + b1_ref[...]))
        h = jnp.sin(freq * (_mm(h, w2_ref[...]) + b2_ref[...]))
        h = jnp.sin(freq * (_mm(h, w3_ref[...]) + b3_ref[...]))
        h = _mm(h, w4_ref[:, half * HYENA_WIDTH:(half + 1) * HYENA_WIDTH])
        return h * jnp.exp(-t * jnp.abs(deltas_ref[...]))

    hf = half_filter(row, 0)
    hb = half_filter(jnp.where(row == 0, 0, seq - row), 1)
    hb = jnp.where(row == 0, 0.0, hb)

    @pl.when(i == 0)
    def _():
        ft_ref[:, 0:CONV_P] = jnp.zeros((HYENA_WIDTH, CONV_P), F32)

    c0 = pl.multiple_of(CONV_P + i * tl, LANES)
    ft_ref[:, pl.ds(c0, tl)] = hb.T
    c1 = pl.multiple_of(CONV_P + seq + i * tl, LANES)
    ft_ref[:, pl.ds(c1, tl)] = hf.T

    @pl.when(i == pl.num_programs(0) - 1)
    def _():
        full = ft_ref[...]
        norm = jnp.sum(jnp.abs(full), axis=1, keepdims=True)
        ft_ref[...] = full / norm


def _hyena_filter(seq, w1, b1, w2, b2, w3, b3, w4, freq):
    tl = 512
    f = jnp.linspace(1e-4, HYENA_BANDS - 1, HYENA_BANDS, dtype=F32)
    bands = jnp.concatenate([jnp.zeros((1,), F32), f, f, jnp.zeros((LANES - HYENA_EMB,), F32)]).reshape(1, LANES)
    max_decay = math.log(HYENA_TARGET) / HYENA_FAST_DECAY
    min_decay = math.log(HYENA_TARGET) / HYENA_SLOW_DECAY
    deltas = jnp.linspace(min_decay, max_decay, HYENA_WIDTH, dtype=F32).reshape(1, HYENA_WIDTH)
    w1p = jnp.pad(w1, ((0, LANES - HYENA_EMB), (0, 0)))
    row = lambda a: a.reshape(1, -1)
    args = (bands, deltas, w1p, row(b1), w2, row(b2), w3, row(b3), w4, row(freq))
    return pl.pallas_call(
        functools.partial(_filter_kernel, seq=seq, tl=tl),
        grid=(seq // tl,),
        in_specs=[_const_spec(a.shape) for a in args],
        out_specs=pl.BlockSpec((HYENA_WIDTH, CONV_P + 2 * seq), lambda i: (0, 0)),
        out_shape=jax.ShapeDtypeStruct((HYENA_WIDTH, CONV_P + 2 * seq), F32),
        compiler_params=_cparams(("arbitrary",)),
        name="hyena_filter",
    )(*args)


def _shift_rows(x, prev_row, next_row):
    n = x.shape[0]
    row = lax.broadcasted_iota(jnp.int32, (n, 1), 0)
    xm = jnp.where(row == 0, prev_row, pltpu.roll(x, 1, axis=0))
    xp = jnp.where(row == n - 1, next_row, pltpu.roll(x, n - 1, axis=0))
    return xm, xp


def _hyena_pre_kernel(x_ref, prev_ref, next_ref, w_ref, b_ref, zt_ref, x0t_ref):
    i = pl.program_id(1)
    x = x_ref[0]
    prev_row = jnp.where(i == 0, 0.0, prev_ref[0, SUBLANES - 1:SUBLANES, :])
    next_row = jnp.where(i == pl.num_programs(1) - 1, 0.0, next_ref[0, 0:1, :])
    xm, xp = _shift_rows(x, prev_row, next_row)
    u = xm * w_ref[0:1, :] + x * w_ref[1:2, :] + xp * w_ref[2:3, :] + b_ref[...]
    c = HYENA_WIDTH
    x0, x1, v = u[:, :c], u[:, c:2 * c], u[:, 2 * c:]
    zt_ref[0] = (v * x1).T
    x0t_ref[0] = x0.T


def _hyena_pre(hy3, short_w, short_b, tl):
    batch, seq, width = hy3.shape
    nt = seq // tl
    hb = tl // SUBLANES
    last = seq // SUBLANES - 1
    out_blk = pl.BlockSpec((1, HYENA_WIDTH, tl), lambda b, i: (b, 0, i))
    return pl.pallas_call(
        _hyena_pre_kernel,
        grid=(batch, nt),
        in_specs=[pl.BlockSpec((1, tl, width), lambda b, i: (b, i, 0)),
                  pl.BlockSpec((1, SUBLANES, width), lambda b, i: (b, jnp.maximum(i * hb - 1, 0), 0)),
                  pl.BlockSpec((1, SUBLANES, width), lambda b, i: (b, jnp.minimum((i + 1) * hb, last), 0)),
                  _const_spec((3, width)), _const_spec((1, width))],
        out_specs=[out_blk, out_blk],
        out_shape=[jax.ShapeDtypeStruct((batch, HYENA_WIDTH, seq), F32)] * 2,
        compiler_params=_cparams(("parallel", "parallel")),
        name="hyena_pre",
    )(hy3, hy3, hy3, short_w, short_b.reshape(1, width))


def _hyena_conv_kernel(bias_ref, zt_ref, x0t_ref, ft_ref, o_ref, troll, zpad, *, seq, cb):
    batch = zt_ref.shape[0]
    nb = seq // CONV_P
    p = CONV_P
    g = pl.program_id(0)
    zero_margin = jnp.zeros((batch, nb, p), F32)
    zpad[:, 0:nb, :] = zero_margin
    zpad[:, 2 * nb:3 * nb, :] = zero_margin
    chunk = 1024

    def channel(ci, carry):
        for w in range(2 * seq // chunk):
            a = w * chunk
            src = ft_ref[pl.ds(ci, 1), a:a + chunk + p]
            rolled = pltpu.roll(jnp.broadcast_to(src, (p, chunk + p)), 0, axis=1, stride=1, stride_axis=0)
            troll[:, a:a + chunk] = rolled[:, p:].astype(BF16)
        z = zt_ref[:, ci, :, :]
        zpad[:, nb:2 * nb, :] = z
        acc = jnp.zeros((batch * nb, p), F32)
        for d in range(-(nb - 1), nb):
            zs = zpad[:, nb - d:2 * nb - d, :].reshape(batch * nb, p).astype(BF16)
            t = troll[:, seq + d * p:seq + (d + 1) * p]
            acc = acc + jnp.dot(zs, t, preferred_element_type=F32)
        bias = bias_ref[g * cb + ci]
        y = (acc.reshape(batch, nb, p) + z * bias) * x0t_ref[:, ci, :, :]
        o_ref[:, ci, :, :] = y
        return carry

    lax.fori_loop(0, cb, channel, 0)


def _hyena_conv(zt, x0t, ft, filt_bias, cb):
    batch, c, seq = zt.shape
    nb = seq // CONV_P
    z4 = zt.reshape(batch, c, nb, CONV_P)
    x4 = x0t.reshape(batch, c, nb, CONV_P)
    blk = pl.BlockSpec((batch, cb, nb, CONV_P), lambda g: (0, g, 0, 0))
    out = pl.pallas_call(
        functools.partial(_hyena_conv_kernel, seq=seq, cb=cb),
        grid=(c // cb,),
        in_specs=[pl.BlockSpec(memory_space=pltpu.SMEM), blk, blk,
                  pl.BlockSpec((cb, CONV_P + 2 * seq), lambda g: (g, 0))],
        out_specs=blk,
        out_shape=jax.ShapeDtypeStruct((batch, c, nb, CONV_P), F32),
        scratch_shapes=[pltpu.VMEM((CONV_P, 2 * seq), BF16),
                        pltpu.VMEM((batch, 3 * nb, CONV_P), F32)],
        compiler_params=_cparams(("parallel",)),
        name="hyena_conv",
    )(filt_bias, z4, x4, ft)
    return out.reshape(batch, c, seq)


def _outproj_kernel(o0, o1, o2, l0, l1, l2, hyt_ref, x_ref, mod_ref, wa_ref, wh_ref, out_ref):
    ls = [l0[...], l1[...], l2[...]]
    os_ = [o0[...], o1[...], o2[...]]
    m = jnp.maximum(jnp.maximum(ls[0], ls[1]), ls[2])
    es = [jnp.exp(l - m) for l in ls]
    den = es[0] + es[1] + es[2]
    attn = (es[0] * os_[0] + es[1] * os_[1] + es[2] * os_[2]) / den
    hy = hyt_ref[0].T
    mix = jnp.dot(attn.astype(BF16), wa_ref[...], preferred_element_type=F32)
    mix = mix + jnp.dot(hy.astype(BF16), wh_ref[...], preferred_element_type=F32)
    out_ref[...] = x_ref[...] + mod_ref[0, 2:3, :] * mix


def _out_proj(os_, ls, hyt, x2, mods, w_out_bf, seq, tm):
    rows, d = x2.shape
    tpb = seq // tm
    rb = pl.BlockSpec((tm, GROUP_WIDTH), lambda i: (i, 0))
    wa, wh = w_out_bf[:GROUP_WIDTH], w_out_bf[GROUP_WIDTH:]
    return pl.pallas_call(
        _outproj_kernel,
        grid=(rows // tm,),
        in_specs=[rb] * 6 + [
            pl.BlockSpec((1, HYENA_WIDTH, tm), lambda i: (i // tpb, 0, i % tpb)),
            pl.BlockSpec((tm, d), lambda i: (i, 0)),
            pl.BlockSpec((1, 6, d), lambda i: (i // tpb, 0, 0)),
            _const_spec(wa.shape), _const_spec(wh.shape)],
        out_specs=pl.BlockSpec((tm, d), lambda i: (i, 0)),
        out_shape=jax.ShapeDtypeStruct((rows, d), F32),
        compiler_params=_cparams(("parallel",)),
        name="out_proj",
    )(*os_, *ls, hyt, x2, mods, wa, wh)


def _halo_rows(x_ref, prev_ref, next_ref, tiles_per_seq):
    i = pl.program_id(0)
    first = (i % tiles_per_seq) == 0
    last = (i % tiles_per_seq) == tiles_per_seq - 1
    xe = jnp.concatenate([prev_ref[...], x_ref[...], next_ref[...]], axis=0)
    return xe, first, last


def _ffn_kernel(x_ref, prev_ref, next_ref, mod_ref, gain_ref, wa_ref, wg_ref, cw_ref, cb_ref, wd_ref, fin_ref,
                out_ref, acc_ref, *, tiles_per_seq, fc, final_norm):
    tm = x_ref.shape[0]
    x = x_ref[...]
    xe, first, last = _halo_rows(x_ref, prev_ref, next_ref, tiles_per_seq)
    he32 = _norm_mod(xe, gain_ref[...], mod_ref[0, 3:4, :], mod_ref[0, 4:5, :])
    he = he32.astype(BF16)
    h = he32[SUBLANES:SUBLANES + tm].astype(BF16)
    row = lax.broadcasted_iota(jnp.int32, (tm + 2 * SUBLANES, 1), 0)
    lo = jnp.where(first, SUBLANES, 0)
    hi = jnp.where(last, tm + SUBLANES, tm + 2 * SUBLANES)
    keep = jnp.logical_and(row >= lo, row < hi)
    nchunk = wa_ref.shape[1] // fc
    n_ext = tm + 2 * SUBLANES

    def body(j, carry):
        c0 = pl.multiple_of(j * fc, fc)
        a = jnp.dot(he, wa_ref[:, pl.ds(c0, fc)], preferred_element_type=F32)
        a = jnp.where(keep, a, 0.0)
        gate = jnp.dot(h, wg_ref[:, pl.ds(c0, fc)], preferred_element_type=F32)
        am = pltpu.roll(a, 1, axis=0)[SUBLANES:SUBLANES + tm]
        ap = pltpu.roll(a, n_ext - 1, axis=0)[SUBLANES:SUBLANES + tm]
        ac = a[SUBLANES:SUBLANES + tm]
        cw = cw_ref[:, pl.ds(c0, fc)]
        conv = am * cw[0:1] + ac * cw[1:2] + ap * cw[2:3] + cb_ref[:, pl.ds(c0, fc)]
        act = (jax.nn.gelu(conv) * gate).astype(BF16)
        part = jnp.dot(act, wd_ref[pl.ds(c0, fc), :], preferred_element_type=F32)

        @pl.when(j == 0)
        def _():
            acc_ref[...] = part

        @pl.when(j > 0)
        def _():
            acc_ref[...] += part
        return carry

    lax.fori_loop(0, nchunk, body, 0)
    y = x + mod_ref[0, 5:6, :] * acc_ref[...]
    if final_norm:
        ms = jnp.mean(y * y, axis=-1, keepdims=True)
        y = y * lax.rsqrt(ms + RMS_EPS) * fin_ref[...]
    out_ref[...] = y


def _halo_specs(tm, d, nrows):
    hb = tm // SUBLANES
    last = nrows // SUBLANES - 1
    return [pl.BlockSpec((tm, d), lambda i: (i, 0)),
            pl.BlockSpec((SUBLANES, d), lambda i: (jnp.maximum(i * hb - 1, 0), 0)),
            pl.BlockSpec((SUBLANES, d), lambda i: (jnp.minimum((i + 1) * hb, last), 0))]


def _conv_ffn(x2, mods, gain, up_bf, conv_w, conv_b, down_bf, fin_gain, seq, tm, final_norm):
    rows, d = x2.shape
    dff = down_bf.shape[0]
    fc = 256
    tpb = seq // tm
    wa, wg = up_bf[:, :dff], up_bf[:, dff:]
    return pl.pallas_call(
        functools.partial(_ffn_kernel, tiles_per_seq=tpb, fc=fc, final_norm=final_norm),
        grid=(rows // tm,),
        in_specs=_halo_specs(tm, d, rows) + [
            pl.BlockSpec((1, 6, d), lambda i: (i // tpb, 0, 0)),
            _const_spec((1, d)), _const_spec(wa.shape), _const_spec(wg.shape),
            _const_spec((3, dff)), _const_spec((1, dff)), _const_spec(down_bf.shape), _const_spec((1, d))],
        out_specs=pl.BlockSpec((tm, d), lambda i: (i, 0)),
        out_shape=jax.ShapeDtypeStruct((rows, d), F32),
        scratch_shapes=[pltpu.VMEM((tm, d), F32)],
        compiler_params=_cparams(("parallel",)),
        name="conv_ffn",
    )(x2, x2, x2, mods, gain.reshape(1, d), wa, wg, conv_w, conv_b.reshape(1, dff), down_bf,
      fin_gain.reshape(1, d))


def _head_sum(x, ones_bd):
    hi = x.astype(BF16)
    lo = (x - hi.astype(F32)).astype(BF16)
    parts = []
    for c in range(x.shape[1] // MXU_DIM):
        cols = slice(c * MXU_DIM, (c + 1) * MXU_DIM)
        parts.append(jnp.dot(hi[:, cols], ones_bd, preferred_element_type=F32)
                     + jnp.dot(lo[:, cols], ones_bd, preferred_element_type=F32))
    return jnp.concatenate(parts, axis=1)


def _ones_blockdiag():
    r = lax.broadcasted_iota(jnp.int32, (MXU_DIM, MXU_DIM), 0) // RWKV_HEAD
    c = lax.broadcasted_iota(jnp.int32, (MXU_DIM, MXU_DIM), 1) // RWKV_HEAD
    return jnp.where(r == c, 1.0, 0.0).astype(BF16)


def _rwkv_pre_kernel(x_ref, prev_ref, next_ref, mod_ref, gain_ref, mu_ref, wr_ref, wk_ref, wv_ref,
                     w1_ref, w2_ref, w0_ref, a1_ref, a2_ref, a0_ref, g1_ref, g2_ref, kk_w_ref, ka_ref, rk_ref,
                     r_out, v_out, kk_out, g_out, bonus_out, lw0_out, k0_out, a0_out, lw1_out, k1_out, a1_out,
                     *, tiles_per_seq):
    tm, d = x_ref.shape
    xe, first, last = _halo_rows(x_ref, prev_ref, next_ref, tiles_per_seq)
    he = _norm_mod(xe, gain_ref[...], mod_ref[0, 0:1, :], mod_ref[0, 1:2, :])
    h = he[SUBLANES:SUBLANES + tm]
    prev_row = jnp.where(first, 0.0, he[SUBLANES - 1:SUBLANES])
    next_row = jnp.where(last, 0.0, he[tm + SUBLANES:tm + SUBLANES + 1])
    hm, hp = _shift_rows(h, prev_row, next_row)
    xx = 0.5 * (hm + hp) - h

    def mixed(i):
        return (h + xx * mu_ref[i:i + 1, :]).astype(BF16)

    r = jnp.dot(mixed(0), wr_ref[...], preferred_element_type=F32)
    k = jnp.dot(mixed(2), wk_ref[...], preferred_element_type=F32)
    v = jnp.dot(mixed(3), wv_ref[...], preferred_element_type=F32)
    gl = jax.nn.sigmoid(jnp.dot(mixed(5), g1_ref[...], preferred_element_type=F32))
    g = jnp.dot(gl.astype(BF16), g2_ref[...], preferred_element_type=F32)
    wl = jnp.tanh(jnp.dot(mixed(1), w1_ref[...], preferred_element_type=F32))
    wl = jnp.dot(wl.astype(BF16), w2_ref[...], preferred_element_type=F32)
    al = jnp.dot(mixed(4), a1_ref[...], preferred_element_type=F32)
    al = jnp.dot(al.astype(BF16), a2_ref[...], preferred_element_type=F32)

    ones_bd = _ones_blockdiag()
    kk = k * kk_w_ref[...]
    nrm = jnp.sqrt(_head_sum(kk * kk, ones_bd))
    kk = kk / jnp.maximum(nrm, 1e-12)
    r_out[...] = r
    v_out[...] = v
    kk_out[...] = kk
    g_out[...] = g
    ksum = jnp.zeros_like(k)
    for direction, (lw_o, k_o, a_o) in enumerate(((lw0_out, k0_out, a0_out), (lw1_out, k1_out, a1_out))):
        cols = slice(direction * d, (direction + 1) * d)
        u = w0_ref[direction:direction + 1, :] + wl[:, cols]
        lw_o[...] = -jax.nn.sigmoid(u) * math.exp(-0.5)
        a = jax.nn.sigmoid(a0_ref[direction:direction + 1, :] + al[:, cols])
        kd = k * (1.0 + (a - 1.0) * ka_ref[...])
        a_o[...] = a
        k_o[...] = kd
        ksum = ksum + kd
    bonus_out[...] = _head_sum(r * (0.5 * ksum) * rk_ref[...], ones_bd) * v


def _blockdiag2(m):
    z = jnp.zeros_like(m[0])
    return jnp.concatenate([jnp.concatenate([m[0], z], axis=1), jnp.concatenate([z, m[1]], axis=1)], axis=0)


def _rwkv_pre(x2, mods, p, seq, tm):
    rows, d = x2.shape
    tpb = seq // tm
    bf = lambda a: a.astype(BF16)
    w1 = bf(jnp.concatenate([p['w1'][0], p['w1'][1]], axis=1))
    w2 = bf(_blockdiag2(p['w2']))
    a1 = bf(jnp.concatenate([p['a1'][0], p['a1'][1]], axis=1))
    a2 = bf(_blockdiag2(p['a2']))
    glora = p['g1'].shape[1]
    gpad = -glora % LANES
    g1 = bf(jnp.pad(p['g1'], ((0, 0), (0, gpad))))
    g2 = bf(jnp.pad(p['g2'], ((0, gpad), (0, 0))))
    row = lambda a: a.reshape(1, d)
    args = (x2, x2, x2, mods, row(p['norm1']), p['mu'], bf(p['w_r']), bf(p['w_k']), bf(p['w_v']),
            w1, w2, p['w0'], a1, a2, p['a0'], g1, g2, row(p['k_k']), row(p['k_a']), row(p['r_k']))
    in_specs = _halo_specs(tm, d, rows) + [pl.BlockSpec((1, 6, d), lambda i: (i // tpb, 0, 0))]
    in_specs += [_const_spec(a.shape) for a in args[4:]]
    ob = pl.BlockSpec((tm, d), lambda i: (i, 0))
    return pl.pallas_call(
        functools.partial(_rwkv_pre_kernel, tiles_per_seq=tpb),
        grid=(rows // tm,),
        in_specs=in_specs,
        out_specs=[ob] * 11,
        out_shape=[jax.ShapeDtypeStruct((rows, d), F32)] * 11,
        compiler_params=_cparams(("parallel",)),
        name="rwkv_pre",
    )(*args)


def _scan_masks(reverse):
    c = SCAN_C
    t = lax.broadcasted_iota(jnp.int32, (c, 4 * c), 0)
    s = lax.broadcasted_iota(jnp.int32, (c, 4 * c), 1) % c
    before = (s > t) if reverse else (s < t)
    upto = jnp.logical_or(before, s == t)
    eye = jnp.where(s == t, 1.0, 0.0)
    rr = lax.broadcasted_iota(jnp.int32, (4 * c, SCAN_HG), 0) // c
    cc = lax.broadcasted_iota(jnp.int32, (4 * c, SCAN_HG), 1) // RWKV_HEAD
    return before, upto, eye, rr == cc


def _bd(x, blockmask):
    return jnp.where(blockmask, jnp.concatenate([x] * 4, axis=0), 0.0)


def _mm(a, b):
    return jnp.dot(a, b, precision=HIGHEST, preferred_element_type=F32)


def _mm_nt(a, b):
    return lax.dot_general(a, b, (((1,), (1,)), ((), ())), precision=HIGHEST, preferred_element_type=F32)


def _scan_chunk(r, v, kk, lw, k, a, s_state, masks, reverse):
    c = SCAN_C
    before, upto, eye, blockmask = masks
    tt = lax.broadcasted_iota(jnp.int32, (c, c), 0)
    ss = lax.broadcasted_iota(jnp.int32, (c, c), 1)
    tri = jnp.where((ss >= tt) if reverse else (ss <= tt), 1.0, 0.0)
    g = _mm(tri, lw)
    gtot = g[0:1] if reverse else g[c - 1:c]
    big_g = jnp.exp(g)
    kkd = kk * jnp.exp(g - lw)
    rd = r * big_g
    b = kk * a
    ginv = jnp.exp(-g)
    gend = jnp.exp(gtot - g)
    bi, ki = b * ginv, k * ginv
    bg, kg = b * gend, k * gend

    lhs = jnp.concatenate([kkd, rd], axis=0)
    a1 = _mm_nt(lhs, _bd(bi, blockmask))
    a2 = _mm_nt(lhs, _bd(ki, blockmask))
    n_ab = jnp.where(before, a1[:c], 0.0)
    a_rb = jnp.where(upto, a1[c:], 0.0)
    a_ak = jnp.where(before, a2[:c], 0.0)
    a_rk = jnp.where(upto, a2[c:], 0.0)

    npow = _mm(n_ab, _bd(n_ab, blockmask))
    tinv = eye - n_ab
    steps = int(math.log2(c)) - 1
    for step in range(steps):
        prod = _mm(jnp.concatenate([npow, tinv], axis=0), _bd(npow, blockmask))
        tinv = tinv + prod[c:]
        npow = prod[:c]

    bdv = _bd(v, blockmask)
    w1 = _mm(a_ak, bdv)
    tw = _mm(tinv, jnp.concatenate([_bd(kkd, blockmask), _bd(w1, blockmask)], axis=1))
    kkt, wt = tw[:, :SCAN_HG], tw[:, SCAN_HG:]
    aw = _mm(a_rb, jnp.concatenate([_bd(kkt, blockmask), _bd(wt, blockmask)], axis=1))
    rt = rd - aw[:, :SCAN_HG]
    ypre = _mm(a_rk, bdv) - aw[:, SCAN_HG:]
    x1 = _mm(bg.T, tw)
    x2 = _mm(kg.T, v)
    sq = lax.broadcasted_iota(jnp.int32, (SCAN_HG, SCAN_HG), 0)
    sv = lax.broadcasted_iota(jnp.int32, (SCAN_HG, SCAN_HG), 1)
    same_head = (sq // RWKV_HEAD) == (sv // RWKV_HEAD)
    gdiag = jnp.where(sq == sv, jnp.broadcast_to(jnp.exp(gtot), (SCAN_HG, SCAN_HG)), 0.0)
    m = gdiag - jnp.where(same_head, x1[:, :SCAN_HG], 0.0)
    dl = jnp.where(same_head, x2 - x1[:, SCAN_HG:], 0.0)
    out = _mm(jnp.concatenate([rt, m], axis=0), s_state)
    y = out[:c] + ypre
    s_new = out[c:] + dl
    return y, s_new


def _scan_kernel(r_ref, v_ref, kk_ref, lw_ref, k_ref, a_ref, y_ref, s_ref, *, reverse):
    tb = r_ref.shape[0]
    nchunk = tb // SCAN_C

    @pl.when(pl.program_id(2) == 0)
    def _():
        s_ref[...] = jnp.zeros_like(s_ref)

    masks = _scan_masks(reverse)

    def body(j, carry):
        jj = (nchunk - 1 - j) if reverse else j
        r0 = pl.multiple_of(jj * SCAN_C, SCAN_C)
        rows = pl.ds(r0, SCAN_C)
        y, s_new = _scan_chunk(r_ref[rows, :], v_ref[rows, :], kk_ref[rows, :], lw_ref[rows, :], k_ref[rows, :],
                               a_ref[rows, :], s_ref[...], masks, reverse)
        y_ref[rows, :] = y
        s_ref[...] = s_new
        return carry

    lax.fori_loop(0, nchunk, body, 0)


def _wkv_scan(r, v, kk, lw, k, a, batch, seq, reverse, tb):
    rows, d = r.shape
    nt = seq // tb
    ng = d // SCAN_HG

    def idx(b, g, t):
        tt = (nt - 1 - t) if reverse else t
        return (b * nt + tt, g)

    blk = pl.BlockSpec((tb, SCAN_HG), idx)
    return pl.pallas_call(
        functools.partial(_scan_kernel, reverse=reverse),
        grid=(batch, ng, nt),
        in_specs=[blk] * 6,
        out_specs=blk,
        out_shape=jax.ShapeDtypeStruct((rows, d), F32),
        scratch_shapes=[pltpu.VMEM((SCAN_HG, SCAN_HG), F32)],
        compiler_params=_cparams(("parallel", "parallel", "arbitrary")),
        name="wkv_bwd" if reverse else "wkv_fwd",
    )(r, v, kk, lw, k, a)


def _rwkv_post_kernel(yf_ref, yb_ref, bonus_ref, g_ref, x_ref, mod_ref, lnw_ref, lnb_ref, wo_ref, out_ref):
    y = yf_ref[...] + yb_ref[...]
    ones_bd = _ones_blockdiag()
    mu = _head_sum(y, ones_bd) * (1.0 / RWKV_HEAD)
    yc = y - mu
    var = _head_sum(yc * yc, ones_bd) * (1.0 / RWKV_HEAD)
    yn = yc * lax.rsqrt(var + GN_EPS) * lnw_ref[...] + lnb_ref[...]
    out = ((yn + bonus_ref[...]) * g_ref[...]).astype(BF16)
    mix = jnp.dot(out, wo_ref[...], preferred_element_type=F32)
    out_ref[...] = x_ref[...] + mod_ref[0, 2:3, :] * mix


def _rwkv_post(yf, yb, bonus, g, x2, mods, ln_w, ln_b, wo_bf, seq, tm):
    rows, d = x2.shape
    tpb = seq // tm
    rb = pl.BlockSpec((tm, d), lambda i: (i, 0))
    return pl.pallas_call(
        _rwkv_post_kernel,
        grid=(rows // tm,),
        in_specs=[rb] * 5 + [pl.BlockSpec((1, 6, d), lambda i: (i // tpb, 0, 0)),
                             _const_spec((1, d)), _const_spec((1, d)), _const_spec((d, d))],
        out_specs=rb,
        out_shape=jax.ShapeDtypeStruct((rows, d), F32),
        compiler_params=_cparams(("parallel",)),
        name="rwkv_post",
    )(yf, yb, bonus, g, x2, mods, ln_w.reshape(1, d), ln_b.reshape(1, d), wo_bf)


def _mods(c, w, b):
    batch, d = c.shape
    pad = -batch % SUBLANES
    m = _ada_params(jnp.pad(c, ((0, pad), (0, 0))), w, b)[:batch]
    return m.reshape(batch, 6, d)


def _trunk(x, c, l0, l1, final_norm):
    batch, seq, d = x.shape
    rows = batch * seq
    x2 = x.reshape(rows, d)
    bf = lambda a: a.astype(BF16)
    tm = 512

    mods0 = _mods(c, l0['ada_w'], l0['ada_b'])
    qkv, hy = _in_proj(x2, mods0, l0['norm1'], bf(l0['w_in']), seq, tm)
    os_, ls = [], []
    for group, (_, dilation) in enumerate(ATTN_PATTERNS):
        o, lse = _attention_group(qkv, group, dilation, batch, seq)
        os_.append(o)
        ls.append(lse)
    ft = _hyena_filter(seq, l0['filt_w1'], l0['filt_b1'], l0['filt_w2'], l0['filt_b2'], l0['filt_w3'],
                       l0['filt_b3'], l0['filt_w4'], l0['filt_freq'])
    zt, x0t = _hyena_pre(hy.reshape(batch, seq, 3 * HYENA_WIDTH), l0['short_w'], l0['short_b'], tl=512)
    hyt = _hyena_conv(zt, x0t, ft, l0['filt_bias'], cb=8)
    x2 = _out_proj(os_, ls, hyt, x2, mods0, bf(l0['w_out']), seq, tm)
    x2 = _conv_ffn(x2, mods0, l0['norm2'], bf(l0['ffn_up']), l0['ffn_conv_w'], l0['ffn_conv_b'],
                   bf(l0['ffn_down']), final_norm, seq, tm, final_norm=False)

    mods1 = _mods(c, l1['ada_w'], l1['ada_b'])
    r, v, kk, g, bonus, lw0, k0, a0, lw1, k1, a1 = _rwkv_pre(x2, mods1, l1, seq, tm=256)
    yf = _wkv_scan(r, v, kk, lw0, k0, a0, batch, seq, reverse=False, tb=512)
    yb = _wkv_scan(r, v, kk, lw1, k1, a1, batch, seq, reverse=True, tb=512)
    x2 = _rwkv_post(yf, yb, bonus, g, x2, mods1, l1['ln_w'], l1['ln_b'], bf(l1['w_o']), seq, tm)
    x2 = _conv_ffn(x2, mods1, l1['norm2'], bf(l1['ffn_up']), l1['ffn_conv_w'], l1['ffn_conv_b'],
                   bf(l1['ffn_down']), final_norm, seq, tm, final_norm=True)
    return x2.reshape(batch, seq, d)


def kernel(x_prompt, x_sample, c_prompt, c_sample, l0_ada_w, l0_ada_b, l0_norm1, l0_norm2, l0_w_in, l0_short_w, l0_short_b, l0_filt_w1, l0_filt_b1, l0_filt_w2, l0_filt_b2, l0_filt_w3, l0_filt_b3, l0_filt_w4, l0_filt_freq, l0_filt_bias, l0_w_out, l0_ffn_up, l0_ffn_conv_w, l0_ffn_conv_b, l0_ffn_down, l1_ada_w, l1_ada_b, l1_norm1, l1_norm2, l1_mu, l1_w_r, l1_w_k, l1_w_v, l1_w_o, l1_w0, l1_w1, l1_w2, l1_a0, l1_a1, l1_a2, l1_g1, l1_g2, l1_k_k, l1_k_a, l1_r_k, l1_ln_w, l1_ln_b, l1_ffn_up, l1_ffn_conv_w, l1_ffn_conv_b, l1_ffn_down, final_norm):
    layer0 = dict(ada_w=l0_ada_w, ada_b=l0_ada_b, norm1=l0_norm1, norm2=l0_norm2, w_in=l0_w_in,
                  short_w=l0_short_w, short_b=l0_short_b, filt_w1=l0_filt_w1, filt_b1=l0_filt_b1,
                  filt_w2=l0_filt_w2, filt_b2=l0_filt_b2, filt_w3=l0_filt_w3, filt_b3=l0_filt_b3,
                  filt_w4=l0_filt_w4, filt_freq=l0_filt_freq, filt_bias=l0_filt_bias, w_out=l0_w_out,
                  ffn_up=l0_ffn_up, ffn_conv_w=l0_ffn_conv_w, ffn_conv_b=l0_ffn_conv_b, ffn_down=l0_ffn_down)
    layer1 = dict(ada_w=l1_ada_w, ada_b=l1_ada_b, norm1=l1_norm1, norm2=l1_norm2, mu=l1_mu,
                  w_r=l1_w_r, w_k=l1_w_k, w_v=l1_w_v, w_o=l1_w_o, w0=l1_w0, w1=l1_w1, w2=l1_w2,
                  a0=l1_a0, a1=l1_a1, a2=l1_a2, g1=l1_g1, g2=l1_g2, k_k=l1_k_k, k_a=l1_k_a,
                  r_k=l1_r_k.reshape(-1), ln_w=l1_ln_w, ln_b=l1_ln_b, ffn_up=l1_ffn_up,
                  ffn_conv_w=l1_ffn_conv_w, ffn_conv_b=l1_ffn_conv_b, ffn_down=l1_ffn_down)
    nb = x_prompt.shape[0]
    x = jnp.concatenate([x_prompt, x_sample], axis=0)
    c = jnp.concatenate([c_prompt, c_sample], axis=0)
    y = _trunk(x, c, layer0, layer1, final_norm)
    return (y[:nb], y[nb:])
```

```python
import functools
import math

import jax
import jax.numpy as jnp
import numpy as np
from jax import lax
from jax.experimental import pallas as pl
from jax.experimental.pallas import tpu as pltpu

F32 = jnp.float32
BF16 = jnp.bfloat16
HIGHEST = lax.Precision.HIGHEST

HEAD_DIM = 64
ATTN_PATTERNS = ((128, 1), (512, 4), (2048, 16))
N_PATTERNS = 3
HEADS_PER_GROUP = 4
N_ATTN_HEADS = 12
ATTN_WIDTH = 768
GROUP_WIDTH = HEADS_PER_GROUP * HEAD_DIM
ALIBI_MAX_EXP = 8.0
HYENA_WIDTH = 256
HYENA_BANDS = 16
HYENA_EMB = 33
HYENA_FFN = 64
HYENA_TARGET = 1e-2
HYENA_FAST_DECAY = 0.3
HYENA_SLOW_DECAY = 1.5
RWKV_HEAD = 64
RMS_EPS = 1e-6
GN_EPS = 64e-5
NEG_INF = -1e30

LANES = 128
SUBLANES = 8
MXU_DIM = 256
VMEM_LIMIT = 56 * 1024 * 1024

ATTN_RADIUS = 64
ATTN_TQ = 128
ATTN_TK = ATTN_TQ + 2 * ATTN_RADIUS
CONV_P = 256
SCAN_C = 64
SCAN_HG = 256
SCAN_UNROLL = 2


def _cparams(sem):
    return pltpu.CompilerParams(dimension_semantics=sem, vmem_limit_bytes=VMEM_LIMIT)


def _const_spec(shape):
    nd = len(shape)
    return pl.BlockSpec(shape, lambda *_: (0,) * nd, pipeline_mode=pl.Buffered(1))


def _ada_kernel(c_ref, w_ref, b_ref, o_ref):
    c = c_ref[...]
    s = c * jax.nn.sigmoid(c)
    o_ref[...] = jnp.dot(s, w_ref[...], precision=HIGHEST, preferred_element_type=F32) + b_ref[...]


def _ada_params(c_pad, w, b):
    bp, d = c_pad.shape
    n = w.shape[1]
    tn = 1024
    return pl.pallas_call(
        _ada_kernel,
        grid=(n // tn,),
        in_specs=[_const_spec((bp, d)),
                  pl.BlockSpec((d, tn), lambda j: (0, j)),
                  pl.BlockSpec((1, tn), lambda j: (0, j))],
        out_specs=pl.BlockSpec((bp, tn), lambda j: (0, j)),
        out_shape=jax.ShapeDtypeStruct((bp, n), F32),
        compiler_params=_cparams(("parallel",)),
        name="ada_params",
    )(c_pad, w, b.reshape(1, n))


def _norm_mod(x, gain, shift, scale):
    ms = jnp.mean(x * x, axis=-1, keepdims=True)
    y = x * lax.rsqrt(ms + RMS_EPS) * gain
    return y * (1.0 + scale) + shift


def _inproj_kernel(x_ref, mod_ref, gain_ref, w_ref, qkv_ref, hy_ref):
    h = _norm_mod(x_ref[...], gain_ref[...], mod_ref[0, 0:1, :], mod_ref[0, 1:2, :])
    p = jnp.dot(h.astype(BF16), w_ref[...], preferred_element_type=F32)
    nq = qkv_ref.shape[-1]
    qkv_ref[...] = p[:, :nq].astype(BF16)
    hy_ref[...] = p[:, nq:]


def _in_proj(x2, mods, gain, w_bf, seq, tm):
    rows, d = x2.shape
    n = w_bf.shape[1]
    nq = 3 * ATTN_WIDTH
    tpb = seq // tm
    return pl.pallas_call(
        _inproj_kernel,
        grid=(rows // tm,),
        in_specs=[pl.BlockSpec((tm, d), lambda i: (i, 0)),
                  pl.BlockSpec((1, 6, d), lambda i: (i // tpb, 0, 0)),
                  _const_spec((1, d)),
                  _const_spec((d, n))],
        out_specs=[pl.BlockSpec((tm, nq), lambda i: (i, 0)),
                   pl.BlockSpec((tm, n - nq), lambda i: (i, 0))],
        out_shape=[jax.ShapeDtypeStruct((rows, nq), BF16),
                   jax.ShapeDtypeStruct((rows, n - nq), F32)],
        compiler_params=_cparams(("parallel",)),
        name="in_proj",
    )(x2, mods, gain.reshape(1, d), w_bf)


def _attn_kernel(q_ref, k_ref, v_ref, bias_ref, o_ref, lse_ref, kpad, vpad, *, n):
    nq = n // ATTN_TQ
    zeros = jnp.zeros((ATTN_RADIUS, GROUP_WIDTH), BF16)
    kpad[0:ATTN_RADIUS, :] = zeros
    vpad[0:ATTN_RADIUS, :] = zeros
    kpad[n + ATTN_RADIUS:n + 2 * ATTN_RADIUS, :] = zeros
    vpad[n + ATTN_RADIUS:n + 2 * ATTN_RADIUS, :] = zeros
    kpad[ATTN_RADIUS:n + ATTN_RADIUS, :] = k_ref[0]
    vpad[ATTN_RADIUS:n + ATTN_RADIUS, :] = v_ref[0]
    lane = lax.broadcasted_iota(jnp.int32, (1, LANES), 1)
    low = lane < HEAD_DIM

    def body(i, carry):
        r0 = pl.multiple_of(i * ATTN_TQ, ATTN_TQ)
        q = q_ref[0, pl.ds(r0, ATTN_TQ), :]
        kw = kpad[pl.ds(r0, ATTN_TK), :]
        vw = vpad[pl.ds(r0, ATTN_TK), :]
        sel = jnp.where(i == 0, 0, jnp.where(i == nq - 1, 2, 1))
        for pair in range(GROUP_WIDTH // LANES):
            cols = slice(pair * LANES, (pair + 1) * LANES)
            q2, k2, v2 = q[:, cols], kw[:, cols], vw[:, cols]
            outs, lses = [], []
            for hh in range(2):
                hm = low if hh == 0 else jnp.logical_not(low)
                qm = jnp.where(hm, q2, jnp.zeros_like(q2))
                s = lax.dot_general(qm, k2, (((1,), (1,)), ((), ())), preferred_element_type=F32)
                s = s * (HEAD_DIM ** -0.5) + bias_ref[2 * pair + hh, sel]
                m = jnp.max(s, axis=-1, keepdims=True)
                p = jnp.exp(s - m)
                den = jnp.sum(p, axis=-1, keepdims=True)
                o = jnp.dot(p.astype(BF16), v2, preferred_element_type=F32)
                outs.append(o / den)
                lses.append(jnp.broadcast_to(m + jnp.log(den), (ATTN_TQ, LANES)))
            o_ref[0, pl.ds(r0, ATTN_TQ), cols] = jnp.where(low, outs[0], outs[1])
            lse_ref[0, pl.ds(r0, ATTN_TQ), cols] = jnp.where(low, lses[0], lses[1])
        return carry

    lax.fori_loop(0, nq, body, 0)


def _attn_bias(group, dilation):
    slopes = np.exp2(-ALIBI_MAX_EXP * (np.arange(N_ATTN_HEADS, dtype=np.float32) + 1.0) / N_ATTN_HEADS)
    slopes = slopes.reshape(N_PATTERNS, HEADS_PER_GROUP)[group].astype(np.float32)
    qi = np.arange(ATTN_TQ)[:, None]
    kj = np.arange(ATTN_TK)[None, :]
    rel = kj - ATTN_RADIUS - qi
    band = np.abs(rel) <= ATTN_RADIUS
    alibi = -slopes[:, None, None] * (np.abs(rel) * dilation).astype(np.float32)[None]
    kinds = []
    for lo, hi in ((ATTN_RADIUS, ATTN_TK), (0, ATTN_TK), (0, ATTN_TQ + ATTN_RADIUS)):
        valid = band & (kj >= lo) & (kj < hi)
        kinds.append(np.where(valid[None], alibi, np.float32(NEG_INF)))
    return jnp.asarray(np.stack(kinds, axis=1), dtype=F32)


def _attention_group(qkv, group, dilation, batch, seq):
    n = seq // dilation
    assert n % ATTN_TQ == 0 and n >= 2 * ATTN_TQ
    width = 3 * ATTN_WIDTH
    view = qkv.reshape(batch, n, dilation * width)
    cpr = width // GROUP_WIDTH
    cpp = ATTN_WIDTH // GROUP_WIDTH

    def col(part):
        return lambda b, r: (b, 0, r * cpr + part * cpp + group)

    blk = (1, n, GROUP_WIDTH)
    o, lse = pl.pallas_call(
        functools.partial(_attn_kernel, n=n),
        grid=(batch, dilation),
        in_specs=[pl.BlockSpec(blk, col(0)), pl.BlockSpec(blk, col(1)), pl.BlockSpec(blk, col(2)),
                  _const_spec((HEADS_PER_GROUP, 3, ATTN_TQ, ATTN_TK))],
        out_specs=[pl.BlockSpec(blk, lambda b, r: (b, 0, r)),
                   pl.BlockSpec(blk, lambda b, r: (b, 0, r))],
        out_shape=[jax.ShapeDtypeStruct((batch, n, dilation * GROUP_WIDTH), F32)] * 2,
        scratch_shapes=[pltpu.VMEM((n + 2 * ATTN_RADIUS, GROUP_WIDTH), BF16)] * 2,
        compiler_params=_cparams(("parallel", "parallel")),
        name=f"attn_g{group}",
    )(view, view, view, _attn_bias(group, dilation))
    return o.reshape(batch * seq, GROUP_WIDTH), lse.reshape(batch * seq, GROUP_WIDTH)


def _filter_kernel(bands_ref, deltas_ref, w1_ref, b1_ref, w2_ref, b2_ref, w3_ref, b3_ref, w4_ref, freq_ref,
                   ft_ref, *, seq, tl):
    i = pl.program_id(0)
    freq = freq_ref[...]
    lane = lax.broadcasted_iota(jnp.int32, (1, LANES), 1)
    row = lax.broadcasted_iota(jnp.int32, (tl, 1), 0) + i * tl

    def half_filter(pos, half):
        posf = pos.astype(F32)
        t = posf / float(seq - 1)
        z = bands_ref[...] * (2.0 * math.pi * posf / float(seq))
        feat = jnp.where(lane == 0, t,
                         jnp.where(lane <= HYENA_BANDS, jnp.cos(z),
                                   jnp.where(lane <= 2 * HYENA_BANDS, -jnp.sin(z), 0.0)))
        h = jnp.sin(freq * (_mm(feat, w1_ref[...]) + b1_ref[...]))
        h = jnp.sin(freq * (_mm(h, w2_ref[...]) + b2_ref[...]))
        h = jnp.sin(freq * (_mm(h, w3_ref[...]) + b3_ref[...]))
        h = _mm(h, w4_ref[:, half * HYENA_WIDTH:(half + 1) * HYENA_WIDTH])
        return h * jnp.exp(-t * jnp.abs(deltas_ref[...]))

    hf = half_filter(row, 0)
    hb = half_filter(jnp.where(row == 0, 0, seq - row), 1)
    hb = jnp.where(row == 0, 0.0, hb)

    @pl.when(i == 0)
    def _():
        ft_ref[:, 0:CONV_P] = jnp.zeros((HYENA_WIDTH, CONV_P), F32)

    c0 = pl.multiple_of(CONV_P + i * tl, LANES)
    ft_ref[:, pl.ds(c0, tl)] = hb.T
    c1 = pl.multiple_of(CONV_P + seq + i * tl, LANES)
    ft_ref[:, pl.ds(c1, tl)] = hf.T

    @pl.when(i == pl.num_programs(0) - 1)
    def _():
        full = ft_ref[...]
        norm = jnp.sum(jnp.abs(full), axis=1, keepdims=True)
        ft_ref[...] = full / norm


def _hyena_filter(seq, w1, b1, w2, b2, w3, b3, w4, freq):
    tl = 512
    f = jnp.linspace(1e-4, HYENA_BANDS - 1, HYENA_BANDS, dtype=F32)
    bands = jnp.concatenate([jnp.zeros((1,), F32), f, f, jnp.zeros((LANES - HYENA_EMB,), F32)]).reshape(1, LANES)
    max_decay = math.log(HYENA_TARGET) / HYENA_FAST_DECAY
    min_decay = math.log(HYENA_TARGET) / HYENA_SLOW_DECAY
    deltas = jnp.linspace(min_decay, max_decay, HYENA_WIDTH, dtype=F32).reshape(1, HYENA_WIDTH)
    w1p = jnp.pad(w1, ((0, LANES - HYENA_EMB), (0, 0)))
    row = lambda a: a.reshape(1, -1)
    args = (bands, deltas, w1p, row(b1), w2, row(b2), w3, row(b3), w4, row(freq))
    return pl.pallas_call(
        functools.partial(_filter_kernel, seq=seq, tl=tl),
        grid=(seq // tl,),
        in_specs=[_const_spec(a.shape) for a in args],
        out_specs=pl.BlockSpec((HYENA_WIDTH, CONV_P + 2 * seq), lambda i: (0, 0)),
        out_shape=jax.ShapeDtypeStruct((HYENA_WIDTH, CONV_P + 2 * seq), F32),
        compiler_params=_cparams(("arbitrary",)),
        name="hyena_filter",
    )(*args)


def _shift_rows(x, prev_row, next_row):
    n = x.shape[0]
    row = lax.broadcasted_iota(jnp.int32, (n, 1), 0)
    xm = jnp.where(row == 0, prev_row, pltpu.roll(x, 1, axis=0))
    xp = jnp.where(row == n - 1, next_row, pltpu.roll(x, n - 1, axis=0))
    return xm, xp


def _hyena_pre_kernel(x_ref, prev_ref, next_ref, w_ref, b_ref, zt_ref, x0t_ref):
    i = pl.program_id(1)
    x = x_ref[0]
    prev_row = jnp.where(i == 0, 0.0, prev_ref[0, SUBLANES - 1:SUBLANES, :])
    next_row = jnp.where(i == pl.num_programs(1) - 1, 0.0, next_ref[0, 0:1, :])
    xm, xp = _shift_rows(x, prev_row, next_row)
    u = xm * w_ref[0:1, :] + x * w_ref[1:2, :] + xp * w_ref[2:3, :] + b_ref[...]
    c = HYENA_WIDTH
    x0, x1, v = u[:, :c], u[:, c:2 * c], u[:, 2 * c:]
    zt_ref[0] = (v * x1).T
    x0t_ref[0] = x0.T


def _hyena_pre(hy3, short_w, short_b, tl):
    batch, seq, width = hy3.shape
    nt = seq // tl
    hb = tl // SUBLANES
    last = seq // SUBLANES - 1
    out_blk = pl.BlockSpec((1, HYENA_WIDTH, tl), lambda b, i: (b, 0, i))
    return pl.pallas_call(
        _hyena_pre_kernel,
        grid=(batch, nt),
        in_specs=[pl.BlockSpec((1, tl, width), lambda b, i: (b, i, 0)),
                  pl.BlockSpec((1, SUBLANES, width), lambda b, i: (b, jnp.maximum(i * hb - 1, 0), 0)),
                  pl.BlockSpec((1, SUBLANES, width), lambda b, i: (b, jnp.minimum((i + 1) * hb, last), 0)),
                  _const_spec((3, width)), _const_spec((1, width))],
        out_specs=[out_blk, out_blk],
        out_shape=[jax.ShapeDtypeStruct((batch, HYENA_WIDTH, seq), F32)] * 2,
        compiler_params=_cparams(("parallel", "parallel")),
        name="hyena_pre",
    )(hy3, hy3, hy3, short_w, short_b.reshape(1, width))


def _hyena_conv_kernel(bias_ref, zt_ref, x0t_ref, ft_ref, o_ref, troll, zpad, *, seq, cb):
    batch = zt_ref.shape[0]
    nb = seq // CONV_P
    p = CONV_P
    g = pl.program_id(0)
    zero_margin = jnp.zeros((batch, nb, p), F32)
    zpad[:, 0:nb, :] = zero_margin
    zpad[:, 2 * nb:3 * nb, :] = zero_margin
    chunk = 1024

    def channel(ci, carry):
        for w in range(2 * seq // chunk):
            a = w * chunk
            src = ft_ref[pl.ds(ci, 1), a:a + chunk + p]
            rolled = pltpu.roll(jnp.broadcast_to(src, (p, chunk + p)), 0, axis=1, stride=1, stride_axis=0)
            troll[:, a:a + chunk] = rolled[:, p:].astype(BF16)
        z = zt_ref[:, ci, :, :]
        zpad[:, nb:2 * nb, :] = z
        acc = jnp.zeros((batch * nb, p), F32)
        for d in range(-(nb - 1), nb):
            zs = zpad[:, nb - d:2 * nb - d, :].reshape(batch * nb, p).astype(BF16)
            t = troll[:, seq + d * p:seq + (d + 1) * p]
            acc = acc + jnp.dot(zs, t, preferred_element_type=F32)
        bias = bias_ref[g * cb + ci]
        y = (acc.reshape(batch, nb, p) + z * bias) * x0t_ref[:, ci, :, :]
        o_ref[:, ci, :, :] = y
        return carry

    lax.fori_loop(0, cb, channel, 0)


def _hyena_conv(zt, x0t, ft, filt_bias, cb):
    batch, c, seq = zt.shape
    nb = seq // CONV_P
    z4 = zt.reshape(batch, c, nb, CONV_P)
    x4 = x0t.reshape(batch, c, nb, CONV_P)
    blk = pl.BlockSpec((batch, cb, nb, CONV_P), lambda g: (0, g, 0, 0))
    out = pl.pallas_call(
        functools.partial(_hyena_conv_kernel, seq=seq, cb=cb),
        grid=(c // cb,),
        in_specs=[pl.BlockSpec(memory_space=pltpu.SMEM), blk, blk,
                  pl.BlockSpec((cb, CONV_P + 2 * seq), lambda g: (g, 0))],
        out_specs=blk,
        out_shape=jax.ShapeDtypeStruct((batch, c, nb, CONV_P), F32),
        scratch_shapes=[pltpu.VMEM((CONV_P, 2 * seq), BF16),
                        pltpu.VMEM((batch, 3 * nb, CONV_P), F32)],
        compiler_params=_cparams(("parallel",)),
        name="hyena_conv",
    )(filt_bias, z4, x4, ft)
    return out.reshape(batch, c, seq)


def _outproj_kernel(o0, o1, o2, l0, l1, l2, hyt_ref, x_ref, mod_ref, wa_ref, wh_ref, out_ref):
    ls = [l0[...], l1[...], l2[...]]
    os_ = [o0[...], o1[...], o2[...]]
    m = jnp.maximum(jnp.maximum(ls[0], ls[1]), ls[2])
    es = [jnp.exp(l - m) for l in ls]
    den = es[0] + es[1] + es[2]
    attn = (es[0] * os_[0] + es[1] * os_[1] + es[2] * os_[2]) / den
    hy = hyt_ref[0].T
    mix = jnp.dot(attn.astype(BF16), wa_ref[...], preferred_element_type=F32)
    mix = mix + jnp.dot(hy.astype(BF16), wh_ref[...], preferred_element_type=F32)
    out_ref[...] = x_ref[...] + mod_ref[0, 2:3, :] * mix


def _out_proj(os_, ls, hyt, x2, mods, w_out_bf, seq, tm):
    rows, d = x2.shape
    tpb = seq // tm
    rb = pl.BlockSpec((tm, GROUP_WIDTH), lambda i: (i, 0))
    wa, wh = w_out_bf[:GROUP_WIDTH], w_out_bf[GROUP_WIDTH:]
    return pl.pallas_call(
        _outproj_kernel,
        grid=(rows // tm,),
        in_specs=[rb] * 6 + [
            pl.BlockSpec((1, HYENA_WIDTH, tm), lambda i: (i // tpb, 0, i % tpb)),
            pl.BlockSpec((tm, d), lambda i: (i, 0)),
            pl.BlockSpec((1, 6, d), lambda i: (i // tpb, 0, 0)),
            _const_spec(wa.shape), _const_spec(wh.shape)],
        out_specs=pl.BlockSpec((tm, d), lambda i: (i, 0)),
        out_shape=jax.ShapeDtypeStruct((rows, d), F32),
        compiler_params=_cparams(("parallel",)),
        name="out_proj",
    )(*os_, *ls, hyt, x2, mods, wa, wh)


def _halo_rows(x_ref, prev_ref, next_ref, tiles_per_seq):
    i = pl.program_id(0)
    first = (i % tiles_per_seq) == 0
    last = (i % tiles_per_seq) == tiles_per_seq - 1
    xe = jnp.concatenate([prev_ref[...], x_ref[...], next_ref[...]], axis=0)
    return xe, first, last


def _ffn_kernel(x_ref, prev_ref, next_ref, mod_ref, gain_ref, wa_ref, wg_ref, cw_ref, cb_ref, wd_ref, fin_ref,
                out_ref, acc_ref, *, tiles_per_seq, fc, final_norm):
    tm = x_ref.shape[0]
    x = x_ref[...]
    xe, first, last = _halo_rows(x_ref, prev_ref, next_ref, tiles_per_seq)
    he32 = _norm_mod(xe, gain_ref[...], mod_ref[0, 3:4, :], mod_ref[0, 4:5, :])
    he = he32.astype(BF16)
    h = he32[SUBLANES:SUBLANES + tm].astype(BF16)
    row = lax.broadcasted_iota(jnp.int32, (tm + 2 * SUBLANES, 1), 0)
    lo = jnp.where(first, SUBLANES, 0)
    hi = jnp.where(last, tm + SUBLANES, tm + 2 * SUBLANES)
    keep = jnp.logical_and(row >= lo, row < hi)
    nchunk = wa_ref.shape[1] // fc
    n_ext = tm + 2 * SUBLANES

    def body(j, carry):
        c0 = pl.multiple_of(j * fc, fc)
        a = jnp.dot(he, wa_ref[:, pl.ds(c0, fc)], preferred_element_type=F32)
        a = jnp.where(keep, a, 0.0)
        gate = jnp.dot(h, wg_ref[:, pl.ds(c0, fc)], preferred_element_type=F32)
        am = pltpu.roll(a, 1, axis=0)[SUBLANES:SUBLANES + tm]
        ap = pltpu.roll(a, n_ext - 1, axis=0)[SUBLANES:SUBLANES + tm]
        ac = a[SUBLANES:SUBLANES + tm]
        cw = cw_ref[:, pl.ds(c0, fc)]
        conv = am * cw[0:1] + ac * cw[1:2] + ap * cw[2:3] + cb_ref[:, pl.ds(c0, fc)]
        act = (jax.nn.gelu(conv) * gate).astype(BF16)
        part = jnp.dot(act, wd_ref[pl.ds(c0, fc), :], preferred_element_type=F32)

        @pl.when(j == 0)
        def _():
            acc_ref[...] = part

        @pl.when(j > 0)
        def _():
            acc_ref[...] += part
        return carry

    lax.fori_loop(0, nchunk, body, 0)
    y = x + mod_ref[0, 5:6, :] * acc_ref[...]
    if final_norm:
        ms = jnp.mean(y * y, axis=-1, keepdims=True)
        y = y * lax.rsqrt(ms + RMS_EPS) * fin_ref[...]
    out_ref[...] = y


def _halo_specs(tm, d, nrows):
    hb = tm // SUBLANES
    last = nrows // SUBLANES - 1
    return [pl.BlockSpec((tm, d), lambda i: (i, 0)),
            pl.BlockSpec((SUBLANES, d), lambda i: (jnp.maximum(i * hb - 1, 0), 0)),
            pl.BlockSpec((SUBLANES, d), lambda i: (jnp.minimum((i + 1) * hb, last), 0))]


def _conv_ffn(x2, mods, gain, up_bf, conv_w, conv_b, down_bf, fin_gain, seq, tm, final_norm):
    rows, d = x2.shape
    dff = down_bf.shape[0]
    fc = 256
    tpb = seq // tm
    wa, wg = up_bf[:, :dff], up_bf[:, dff:]
    return pl.pallas_call(
        functools.partial(_ffn_kernel, tiles_per_seq=tpb, fc=fc, final_norm=final_norm),
        grid=(rows // tm,),
        in_specs=_halo_specs(tm, d, rows) + [
            pl.BlockSpec((1, 6, d), lambda i: (i // tpb, 0, 0)),
            _const_spec((1, d)), _const_spec(wa.shape), _const_spec(wg.shape),
            _const_spec((3, dff)), _const_spec((1, dff)), _const_spec(down_bf.shape), _const_spec((1, d))],
        out_specs=pl.BlockSpec((tm, d), lambda i: (i, 0)),
        out_shape=jax.ShapeDtypeStruct((rows, d), F32),
        scratch_shapes=[pltpu.VMEM((tm, d), F32)],
        compiler_params=_cparams(("parallel",)),
        name="conv_ffn",
    )(x2, x2, x2, mods, gain.reshape(1, d), wa, wg, conv_w, conv_b.reshape(1, dff), down_bf,
      fin_gain.reshape(1, d))


def _head_sum(x, ones_bd):
    hi = x.astype(BF16)
    lo = (x - hi.astype(F32)).astype(BF16)
    parts = []
    for c in range(x.shape[1] // MXU_DIM):
        cols = slice(c * MXU_DIM, (c + 1) * MXU_DIM)
        parts.append(jnp.dot(hi[:, cols], ones_bd, preferred_element_type=F32)
                     + jnp.dot(lo[:, cols], ones_bd, preferred_element_type=F32))
    return jnp.concatenate(parts, axis=1)


def _ones_blockdiag():
    r = lax.broadcasted_iota(jnp.int32, (MXU_DIM, MXU_DIM), 0) // RWKV_HEAD
    c = lax.broadcasted_iota(jnp.int32, (MXU_DIM, MXU_DIM), 1) // RWKV_HEAD
    return jnp.where(r == c, 1.0, 0.0).astype(BF16)


def _rwkv_pre_kernel(x_ref, prev_ref, next_ref, mod_ref, gain_ref, mu_ref, wr_ref, wk_ref, wv_ref,
                     w1_ref, w2_ref, w0_ref, a1_ref, a2_ref, a0_ref, g1_ref, g2_ref, kk_w_ref, ka_ref, rk_ref,
                     r_out, v_out, kk_out, g_out, bonus_out, lw0_out, k0_out, a0_out, lw1_out, k1_out, a1_out,
                     *, tiles_per_seq):
    tm, d = x_ref.shape
    xe, first, last = _halo_rows(x_ref, prev_ref, next_ref, tiles_per_seq)
    he = _norm_mod(xe, gain_ref[...], mod_ref[0, 0:1, :], mod_ref[0, 1:2, :])
    h = he[SUBLANES:SUBLANES + tm]
    prev_row = jnp.where(first, 0.0, he[SUBLANES - 1:SUBLANES])
    next_row = jnp.where(last, 0.0, he[tm + SUBLANES:tm + SUBLANES + 1])
    hm, hp = _shift_rows(h, prev_row, next_row)
    xx = 0.5 * (hm + hp) - h

    def mixed(i):
        return (h + xx * mu_ref[i:i + 1, :]).astype(BF16)

    r = jnp.dot(mixed(0), wr_ref[...], preferred_element_type=F32)
    k = jnp.dot(mixed(2), wk_ref[...], preferred_element_type=F32)
    v = jnp.dot(mixed(3), wv_ref[...], preferred_element_type=F32)
    gl = jax.nn.sigmoid(jnp.dot(mixed(5), g1_ref[...], preferred_element_type=F32))
    g = jnp.dot(gl.astype(BF16), g2_ref[...], preferred_element_type=F32)
    wl = jnp.tanh(jnp.dot(mixed(1), w1_ref[...], preferred_element_type=F32))
    wl = jnp.dot(wl.astype(BF16), w2_ref[...], preferred_element_type=F32)
    al = jnp.dot(mixed(4), a1_ref[...], preferred_element_type=F32)
    al = jnp.dot(al.astype(BF16), a2_ref[...], preferred_element_type=F32)

    ones_bd = _ones_blockdiag()
    kk = k * kk_w_ref[...]
    nrm = jnp.sqrt(_head_sum(kk * kk, ones_bd))
    kk = kk / jnp.maximum(nrm, 1e-12)
    r_out[...] = r
    v_out[...] = v
    kk_out[...] = kk
    g_out[...] = g
    ksum = jnp.zeros_like(k)
    for direction, (lw_o, k_o, a_o) in enumerate(((lw0_out, k0_out, a0_out), (lw1_out, k1_out, a1_out))):
        cols = slice(direction * d, (direction + 1) * d)
        u = w0_ref[direction:direction + 1, :] + wl[:, cols]
        lw_o[...] = -jax.nn.sigmoid(u) * math.exp(-0.5)
        a = jax.nn.sigmoid(a0_ref[direction:direction + 1, :] + al[:, cols])
        kd = k * (1.0 + (a - 1.0) * ka_ref[...])
        a_o[...] = a
        k_o[...] = kd
        ksum = ksum + kd
    bonus_out[...] = _head_sum(r * (0.5 * ksum) * rk_ref[...], ones_bd) * v


def _blockdiag2(m):
    z = jnp.zeros_like(m[0])
    return jnp.concatenate([jnp.concatenate([m[0], z], axis=1), jnp.concatenate([z, m[1]], axis=1)], axis=0)


def _rwkv_pre(x2, mods, p, seq, tm):
    rows, d = x2.shape
    tpb = seq // tm
    bf = lambda a: a.astype(BF16)
    w1 = bf(jnp.concatenate([p['w1'][0], p['w1'][1]], axis=1))
    w2 = bf(_blockdiag2(p['w2']))
    a1 = bf(jnp.concatenate([p['a1'][0], p['a1'][1]], axis=1))
    a2 = bf(_blockdiag2(p['a2']))
    glora = p['g1'].shape[1]
    gpad = -glora % LANES
    g1 = bf(jnp.pad(p['g1'], ((0, 0), (0, gpad))))
    g2 = bf(jnp.pad(p['g2'], ((0, gpad), (0, 0))))
    row = lambda a: a.reshape(1, d)
    args = (x2, x2, x2, mods, row(p['norm1']), p['mu'], bf(p['w_r']), bf(p['w_k']), bf(p['w_v']),
            w1, w2, p['w0'], a1, a2, p['a0'], g1, g2, row(p['k_k']), row(p['k_a']), row(p['r_k']))
    in_specs = _halo_specs(tm, d, rows) + [pl.BlockSpec((1, 6, d), lambda i: (i // tpb, 0, 0))]
    in_specs += [_const_spec(a.shape) for a in args[4:]]
    ob = pl.BlockSpec((tm, d), lambda i: (i, 0))
    return pl.pallas_call(
        functools.partial(_rwkv_pre_kernel, tiles_per_seq=tpb),
        grid=(rows // tm,),
        in_specs=in_specs,
        out_specs=[ob] * 11,
        out_shape=[jax.ShapeDtypeStruct((rows, d), F32)] * 11,
        compiler_params=_cparams(("parallel",)),
        name="rwkv_pre",
    )(*args)


def _scan_masks(reverse):
    c = SCAN_C
    t = lax.broadcasted_iota(jnp.int32, (c, 4 * c), 0)
    s = lax.broadcasted_iota(jnp.int32, (c, 4 * c), 1) % c
    before = (s > t) if reverse else (s < t)
    upto = jnp.logical_or(before, s == t)
    eye = jnp.where(s == t, 1.0, 0.0)
    rr = lax.broadcasted_iota(jnp.int32, (4 * c, SCAN_HG), 0)
    cc = lax.broadcasted_iota(jnp.int32, (4 * c, SCAN_HG), 1)
    same_head = jnp.where((rr // c) == (cc // RWKV_HEAD), 1.0, 0.0)
    eye_big = jnp.where(rr == cc, 1.0, 0.0)
    return before, upto, eye, same_head, same_head.astype(BF16), eye_big


def _bd(xb, head_mask_bf):
    return jnp.concatenate([xb] * 4, axis=0) * head_mask_bf


def _mm(a, b):
    return jnp.dot(a, b, precision=HIGHEST, preferred_element_type=F32)


def _bdot(a, b):
    return jnp.dot(a, b, preferred_element_type=F32)


def _bdot_nt(a, b):
    return lax.dot_general(a, b, (((1,), (1,)), ((), ())), preferred_element_type=F32)


def _scan_prepare(chains):
    c = SCAN_C
    bf = lambda x: x.astype(BF16)
    idx = range(len(chains))
    r, v, kk, lw, k, a = [[ch[0][j] for ch in chains] for j in range(6)]
    tri, before, upto, eye = [[ch[1][j] for ch in chains] for j in range(4)]
    head_mask_bf = chains[0][1][5]
    rev = [ch[2] for ch in chains]
    bd = lambda x: _bd(bf(x), head_mask_bf)

    g = [_mm(tri[i], lw[i]) for i in idx]
    gtot = [g[i][0:1] if rev[i] else g[i][c - 1:c] for i in idx]
    kkd = [kk[i] * jnp.exp(g[i] - lw[i]) for i in idx]
    rd = [r[i] * jnp.exp(g[i]) for i in idx]
    b = [kk[i] * a[i] for i in idx]
    ginv = [jnp.exp(-g[i]) for i in idx]
    lhs = [bf(jnp.concatenate([kkd[i], rd[i]], axis=0)) for i in idx]
    a1 = [_bdot_nt(lhs[i], bd(b[i] * ginv[i])) for i in idx]
    a2 = [_bdot_nt(lhs[i], bd(k[i] * ginv[i])) for i in idx]
    n_ab = [jnp.where(before[i], a1[i][:c], 0.0) for i in idx]
    a_rb = [jnp.where(upto[i], a1[i][c:], 0.0) for i in idx]
    a_k = [jnp.concatenate([jnp.where(before[i], a2[i][:c], 0.0), jnp.where(upto[i], a2[i][c:], 0.0)], axis=0)
           for i in idx]

    nb = [bf(n_ab[i]) for i in idx]
    npow = [_bdot(nb[i], _bd(nb[i], head_mask_bf)) for i in idx]
    tinv = [eye[i] - n_ab[i] for i in idx]
    steps = int(math.log2(c)) - 1
    for step in range(steps):
        pb = [bf(npow[i]) for i in idx]
        rhs = [_bd(pb[i], head_mask_bf) for i in idx]
        if step == steps - 1:
            tinv = [tinv[i] + _bdot(bf(tinv[i]), rhs[i]) for i in idx]
        else:
            prod = [_bdot(jnp.concatenate([pb[i], bf(tinv[i])], axis=0), rhs[i]) for i in idx]
            tinv = [tinv[i] + prod[i][c:] for i in idx]
            npow = [prod[i][:c] for i in idx]

    vb = [bf(v[i]) for i in idx]
    av = [_bdot(bf(a_k[i]), _bd(vb[i], head_mask_bf)) for i in idx]
    tw = [_bdot(bf(tinv[i]), jnp.concatenate([bd(kkd[i]), bd(av[i][:c])], axis=1)) for i in idx]
    kkt = [tw[i][:, :SCAN_HG] for i in idx]
    wt = [tw[i][:, SCAN_HG:] for i in idx]
    aw = [_bdot(bf(a_rb[i]), jnp.concatenate([bd(kkt[i]), bd(wt[i])], axis=1)) for i in idx]
    lhs_state = [bf(jnp.concatenate([rd[i] - aw[i][:, :SCAN_HG], kkt[i]], axis=0)) for i in idx]
    ypre = [av[i][c:] - aw[i][:, SCAN_HG:] for i in idx]
    gend = [jnp.exp(gtot[i] - g[i]) for i in idx]
    upd_t = [bf(jnp.concatenate([b[i] * gend[i], k[i] * gend[i]], axis=0).T) for i in idx]
    gcol_src = [jnp.concatenate([g[i], g[i]], axis=0).T for i in idx]
    gcol = [jnp.exp(gcol_src[i][:, 0:1] if rev[i] else gcol_src[i][:, c - 1:c]) for i in idx]
    return [(lhs_state[i], wt[i], ypre[i], upd_t[i], gcol[i], vb[i]) for i in idx]


def _scan_apply(preps, states, head_mask):
    c = SCAN_C
    idx = range(len(preps))
    o = [_bdot(preps[i][0], states[i].astype(BF16)) for i in idx]
    y = [o[i][:c] + preps[i][2] for i in idx]
    u = [-(preps[i][1] + o[i][c:]) for i in idx]
    upd = [_bdot(preps[i][3], jnp.concatenate([u[i].astype(BF16), preps[i][5]], axis=0)) for i in idx]
    return y, [states[i] * preps[i][4] + head_mask * upd[i] for i in idx]


def _scan_consts(reverse):
    c = SCAN_C
    before, upto, eye, head_mask, head_mask_bf, _ = _scan_masks(reverse)
    tt = lax.broadcasted_iota(jnp.int32, (c, c), 0)
    ss = lax.broadcasted_iota(jnp.int32, (c, c), 1)
    tri = jnp.where((ss >= tt) if reverse else (ss <= tt), 1.0, 0.0)
    return tri, before, upto, eye, head_mask, head_mask_bf


def _scan_kernel(rf_ref, vf_ref, kkf_ref, lwf_ref, kf_ref, af_ref, rb_ref, vb_ref, kkb_ref, lwb_ref, kb_ref, ab_ref,
                 yf_ref, yb_ref, sf_ref, sb_ref):
    tb = rf_ref.shape[0]
    nchunk = tb // SCAN_C

    @pl.when(pl.program_id(2) == 0)
    def _():
        sf_ref[...] = jnp.zeros_like(sf_ref)
        sb_ref[...] = jnp.zeros_like(sb_ref)

    consts_f = _scan_consts(False)
    consts_b = _scan_consts(True)
    head_mask = consts_f[4]

    fwd_refs = (rf_ref, vf_ref, kkf_ref, lwf_ref, kf_ref, af_ref)
    bwd_refs = (rb_ref, vb_ref, kkb_ref, lwb_ref, kb_ref, ab_ref)

    def body(j, carry):
        rows, chains = [], []
        for u in range(SCAN_UNROLL):
            rows_f = pl.ds(pl.multiple_of((j * SCAN_UNROLL + u) * SCAN_C, SCAN_C), SCAN_C)
            rows_b = pl.ds(pl.multiple_of((nchunk - 1 - j * SCAN_UNROLL - u) * SCAN_C, SCAN_C), SCAN_C)
            rows += [rows_f, rows_b]
            chains.append((tuple(ref[rows_f, :] for ref in fwd_refs), consts_f, False))
            chains.append((tuple(ref[rows_b, :] for ref in bwd_refs), consts_b, True))
        preps = _scan_prepare(chains)
        states = [sf_ref[...], sb_ref[...]]
        for u in range(SCAN_UNROLL):
            ys, states = _scan_apply(preps[2 * u:2 * u + 2], states, head_mask)
            yf_ref[rows[2 * u], :] = ys[0]
            yb_ref[rows[2 * u + 1], :] = ys[1]
        sf_ref[...] = states[0]
        sb_ref[...] = states[1]
        return carry

    lax.fori_loop(0, nchunk // SCAN_UNROLL, body, 0)


def _wkv_scan(r, v, kk, lw0, k0, a0, lw1, k1, a1, batch, seq, tb):
    rows, d = r.shape
    nt = seq // tb
    ng = d // SCAN_HG
    fwd = pl.BlockSpec((tb, SCAN_HG), lambda b, g, t: (b * nt + t, g))
    bwd = pl.BlockSpec((tb, SCAN_HG), lambda b, g, t: (b * nt + nt - 1 - t, g))
    return pl.pallas_call(
        _scan_kernel,
        grid=(batch, ng, nt),
        in_specs=[fwd] * 6 + [bwd] * 6,
        out_specs=[fwd, bwd],
        out_shape=[jax.ShapeDtypeStruct((rows, d), F32)] * 2,
        scratch_shapes=[pltpu.VMEM((SCAN_HG, SCAN_HG), F32)] * 2,
        compiler_params=_cparams(("parallel", "parallel", "arbitrary")),
        name="wkv_scan",
    )(r, v, kk, lw0, k0, a0, r, v, kk, lw1, k1, a1)


def _rwkv_post_kernel(yf_ref, yb_ref, bonus_ref, g_ref, x_ref, mod_ref, lnw_ref, lnb_ref, wo_ref, out_ref):
    y = yf_ref[...] + yb_ref[...]
    ones_bd = _ones_blockdiag()
    mu = _head_sum(y, ones_bd) * (1.0 / RWKV_HEAD)
    yc = y - mu
    var = _head_sum(yc * yc, ones_bd) * (1.0 / RWKV_HEAD)
    yn = yc * lax.rsqrt(var + GN_EPS) * lnw_ref[...] + lnb_ref[...]
    out = ((yn + bonus_ref[...]) * g_ref[...]).astype(BF16)
    mix = jnp.dot(out, wo_ref[...], preferred_element_type=F32)
    out_ref[...] = x_ref[...] + mod_ref[0, 2:3, :] * mix


def _rwkv_post(yf, yb, bonus, g, x2, mods, ln_w, ln_b, wo_bf, seq, tm):
    rows, d = x2.shape
    tpb = seq // tm
    rb = pl.BlockSpec((tm, d), lambda i: (i, 0))
    return pl.pallas_call(
        _rwkv_post_kernel,
        grid=(rows // tm,),
        in_specs=[rb] * 5 + [pl.BlockSpec((1, 6, d), lambda i: (i // tpb, 0, 0)),
                             _const_spec((1, d)), _const_spec((1, d)), _const_spec((d, d))],
        out_specs=rb,
        out_shape=jax.ShapeDtypeStruct((rows, d), F32),
        compiler_params=_cparams(("parallel",)),
        name="rwkv_post",
    )(yf, yb, bonus, g, x2, mods, ln_w.reshape(1, d), ln_b.reshape(1, d), wo_bf)


def _mods(c, w, b):
    batch, d = c.shape
    pad = -batch % SUBLANES
    m = _ada_params(jnp.pad(c, ((0, pad), (0, 0))), w, b)[:batch]
    return m.reshape(batch, 6, d)


def _trunk(x, c, l0, l1, final_norm):
    batch, seq, d = x.shape
    rows = batch * seq
    x2 = x.reshape(rows, d)
    bf = lambda a: a.astype(BF16)
    tm = 512

    mods0 = _mods(c, l0['ada_w'], l0['ada_b'])
    qkv, hy = _in_proj(x2, mods0, l0['norm1'], bf(l0['w_in']), seq, tm)
    os_, ls = [], []
    for group, (_, dilation) in enumerate(ATTN_PATTERNS):
        o, lse = _attention_group(qkv, group, dilation, batch, seq)
        os_.append(o)
        ls.append(lse)
    ft = _hyena_filter(seq, l0['filt_w1'], l0['filt_b1'], l0['filt_w2'], l0['filt_b2'], l0['filt_w3'],
                       l0['filt_b3'], l0['filt_w4'], l0['filt_freq'])
    zt, x0t = _hyena_pre(hy.reshape(batch, seq, 3 * HYENA_WIDTH), l0['short_w'], l0['short_b'], tl=512)
    hyt = _hyena_conv(zt, x0t, ft, l0['filt_bias'], cb=8)
    x2 = _out_proj(os_, ls, hyt, x2, mods0, bf(l0['w_out']), seq, tm)
    x2 = _conv_ffn(x2, mods0, l0['norm2'], bf(l0['ffn_up']), l0['ffn_conv_w'], l0['ffn_conv_b'],
                   bf(l0['ffn_down']), final_norm, seq, tm, final_norm=False)

    mods1 = _mods(c, l1['ada_w'], l1['ada_b'])
    r, v, kk, g, bonus, lw0, k0, a0, lw1, k1, a1 = _rwkv_pre(x2, mods1, l1, seq, tm=256)
    yf, yb = _wkv_scan(r, v, kk, lw0, k0, a0, lw1, k1, a1, batch, seq, tb=512)
    x2 = _rwkv_post(yf, yb, bonus, g, x2, mods1, l1['ln_w'], l1['ln_b'], bf(l1['w_o']), seq, tm)
    x2 = _conv_ffn(x2, mods1, l1['norm2'], bf(l1['ffn_up']), l1['ffn_conv_w'], l1['ffn_conv_b'],
                   bf(l1['ffn_down']), final_norm, seq, tm, final_norm=True)
    return x2.reshape(batch, seq, d)


def kernel(x_prompt, x_sample, c_prompt, c_sample, l0_ada_w, l0_ada_b, l0_norm1, l0_norm2, l0_w_in, l0_short_w, l0_short_b, l0_filt_w1, l0_filt_b1, l0_filt_w2, l0_filt_b2, l0_filt_w3, l0_filt_b3, l0_filt_w4, l0_filt_freq, l0_filt_bias, l0_w_out, l0_ffn_up, l0_ffn_conv_w, l0_ffn_conv_b, l0_ffn_down, l1_ada_w, l1_ada_b, l1_norm1, l1_norm2, l1_mu, l1_w_r, l1_w_k, l1_w_v, l1_w_o, l1_w0, l1_w1, l1_w2, l1_a0, l1_a1, l1_a2, l1_g1, l1_g2, l1_k_k, l1_k_a, l1_r_k, l1_ln_w, l1_ln_b, l1_ffn_up, l1_ffn_conv_w, l1_ffn_conv_b, l1_ffn_down, final_norm):
    layer0 = dict(ada_w=l0_ada_w, ada_b=l0_ada_b, norm1=l0_norm1, norm2=l0_norm2, w_in=l0_w_in,
                  short_w=l0_short_w, short_b=l0_short_b, filt_w1=l0_filt_w1, filt_b1=l0_filt_b1,
                  filt_w2=l0_filt_w2, filt_b2=l0_filt_b2, filt_w3=l0_filt_w3, filt_b3=l0_filt_b3,
                  filt_w4=l0_filt_w4, filt_freq=l0_filt_freq, filt_bias=l0_filt_bias, w_out=l0_w_out,
                  ffn_up=l0_ffn_up, ffn_conv_w=l0_ffn_conv_w, ffn_conv_b=l0_ffn_conv_b, ffn_down=l0_ffn_down)
    layer1 = dict(ada_w=l1_ada_w, ada_b=l1_ada_b, norm1=l1_norm1, norm2=l1_norm2, mu=l1_mu,
                  w_r=l1_w_r, w_k=l1_w_k, w_v=l1_w_v, w_o=l1_w_o, w0=l1_w0, w1=l1_w1, w2=l1_w2,
                  a0=l1_a0, a1=l1_a1, a2=l1_a2, g1=l1_g1, g2=l1_g2, k_k=l1_k_k, k_a=l1_k_a,
                  r_k=l1_r_k.reshape(-1), ln_w=l1_ln_w, ln_b=l1_ln_b, ffn_up=l1_ffn_up,
                  ffn_conv_w=l1_ffn_conv_w, ffn_conv_b=l1_ffn_conv_b, ffn_down=l1_ffn_down)
    nb = x_prompt.shape[0]
    x = jnp.concatenate([x_prompt, x_sample], axis=0)
    c = jnp.concatenate([c_prompt, c_sample], axis=0)
    y = _trunk(x, c, layer0, layer1, final_norm)
    return (y[:nb], y[nb:])
```

```python
import functools
import math

import jax
import jax.numpy as jnp
import numpy as np
from jax import lax
from jax.experimental import pallas as pl
from jax.experimental.pallas import tpu as pltpu

F32 = jnp.float32
BF16 = jnp.bfloat16
HIGHEST = lax.Precision.HIGHEST

HEAD_DIM = 64
ATTN_PATTERNS = ((128, 1), (512, 4), (2048, 16))
N_PATTERNS = 3
HEADS_PER_GROUP = 4
N_ATTN_HEADS = 12
ATTN_WIDTH = 768
GROUP_WIDTH = HEADS_PER_GROUP * HEAD_DIM
ALIBI_MAX_EXP = 8.0
HYENA_WIDTH = 256
HYENA_BANDS = 16
HYENA_EMB = 33
HYENA_FFN = 64
HYENA_TARGET = 1e-2
HYENA_FAST_DECAY = 0.3
HYENA_SLOW_DECAY = 1.5
RWKV_HEAD = 64
RMS_EPS = 1e-6
GN_EPS = 64e-5
NEG_INF = -1e30

LANES = 128
SUBLANES = 8
MXU_DIM = 256
VMEM_LIMIT = 56 * 1024 * 1024

ATTN_RADIUS = 64
ATTN_TQ = 128
ATTN_TK = ATTN_TQ + 2 * ATTN_RADIUS
CONV_P = 256
SCAN_C = 64
SCAN_HG = 256
SCAN_UNROLL = 4


def _cparams(sem):
    return pltpu.CompilerParams(dimension_semantics=sem, vmem_limit_bytes=VMEM_LIMIT)


def _const_spec(shape):
    nd = len(shape)
    return pl.BlockSpec(shape, lambda *_: (0,) * nd, pipeline_mode=pl.Buffered(1))


def _ada_kernel(c_ref, w_ref, b_ref, o_ref):
    c = c_ref[...]
    s = c * jax.nn.sigmoid(c)
    o_ref[...] = jnp.dot(s, w_ref[...], precision=HIGHEST, preferred_element_type=F32) + b_ref[...]


def _ada_params(c_pad, w, b):
    bp, d = c_pad.shape
    n = w.shape[1]
    tn = 1024
    return pl.pallas_call(
        _ada_kernel,
        grid=(n // tn,),
        in_specs=[_const_spec((bp, d)),
                  pl.BlockSpec((d, tn), lambda j: (0, j)),
                  pl.BlockSpec((1, tn), lambda j: (0, j))],
        out_specs=pl.BlockSpec((bp, tn), lambda j: (0, j)),
        out_shape=jax.ShapeDtypeStruct((bp, n), F32),
        compiler_params=_cparams(("parallel",)),
        name="ada_params",
    )(c_pad, w, b.reshape(1, n))


def _norm_mod(x, gain, shift, scale):
    ms = jnp.mean(x * x, axis=-1, keepdims=True)
    y = x * lax.rsqrt(ms + RMS_EPS) * gain
    return y * (1.0 + scale) + shift


def _two_source_specs(tm, d, n_first):
    return [pl.BlockSpec((tm, d), lambda i: (jnp.minimum(i, n_first - 1), 0)),
            pl.BlockSpec((tm, d), lambda i: (jnp.maximum(i - n_first, 0), 0))]


def _inproj_kernel(xa_ref, xb_ref, mod_ref, gain_ref, w_ref, q0_ref, q1_ref, q2_ref, hy_ref, scr_ref, *, n_first):
    tm = xa_ref.shape[0]
    x = jnp.where(pl.program_id(0) < n_first, xa_ref[...], xb_ref[...])
    h = _norm_mod(x, gain_ref[...], mod_ref[0, 0:1, :], mod_ref[0, 1:2, :])
    p = jnp.dot(h.astype(BF16), w_ref[...], preferred_element_type=F32)
    gw = 3 * GROUP_WIDTH
    q0_ref[...] = p[:, :gw].astype(BF16)
    for g, out_ref in ((1, q1_ref), (2, q2_ref)):
        dil = out_ref.shape[1]
        for cb in range(gw // LANES):
            cols = slice(cb * LANES, (cb + 1) * LANES)
            scr_ref[cb] = p[:, g * gw + cb * LANES:g * gw + (cb + 1) * LANES]
            for r in range(dil):
                out_ref[0, r, :, cols] = scr_ref[cb, pl.ds(r, tm // dil, stride=dil), :].astype(BF16)
    hy_ref[...] = p[:, N_PATTERNS * gw:]


def _in_proj(xa, xb, mods, gain, w_bf, seq, tm):
    d = xa.shape[1]
    rows = xa.shape[0] + xb.shape[0]
    batch = rows // seq
    n = w_bf.shape[1]
    gw = 3 * GROUP_WIDTH
    tpb = seq // tm
    dils = [dil for _, dil in ATTN_PATTERNS]
    assert dils[0] == 1

    def class_spec(dil):
        return pl.BlockSpec((1, dil, tm // dil, gw), lambda i: (i // tpb, 0, i % tpb, 0))

    return pl.pallas_call(
        functools.partial(_inproj_kernel, n_first=xa.shape[0] // tm),
        grid=(rows // tm,),
        in_specs=_two_source_specs(tm, d, xa.shape[0] // tm) + [
            pl.BlockSpec((1, 6, d), lambda i: (i // tpb, 0, 0)),
            _const_spec((1, d)),
            _const_spec((d, n))],
        out_specs=[pl.BlockSpec((tm, gw), lambda i: (i, 0)), class_spec(dils[1]), class_spec(dils[2]),
                   pl.BlockSpec((tm, n - N_PATTERNS * gw), lambda i: (i, 0))],
        out_shape=[jax.ShapeDtypeStruct((rows, gw), BF16)]
        + [jax.ShapeDtypeStruct((batch, dil, seq // dil, gw), BF16) for dil in dils[1:]]
        + [jax.ShapeDtypeStruct((rows, n - N_PATTERNS * gw), F32)],
        scratch_shapes=[pltpu.VMEM((gw // LANES, tm, LANES), F32)],
        compiler_params=_cparams(("parallel",)),
        name="in_proj",
    )(xa, xb, mods, gain.reshape(1, d), w_bf)


def _attn_kernel(q_ref, k_ref, v_ref, bias_ref, o_ref, lse_ref, kpad, vpad, *, n):
    nq = n // ATTN_TQ
    zeros = jnp.zeros((ATTN_RADIUS, GROUP_WIDTH), BF16)
    kpad[0:ATTN_RADIUS, :] = zeros
    vpad[0:ATTN_RADIUS, :] = zeros
    kpad[n + ATTN_RADIUS:n + 2 * ATTN_RADIUS, :] = zeros
    vpad[n + ATTN_RADIUS:n + 2 * ATTN_RADIUS, :] = zeros
    kpad[ATTN_RADIUS:n + ATTN_RADIUS, :] = k_ref[...]
    vpad[ATTN_RADIUS:n + ATTN_RADIUS, :] = v_ref[...]
    lane = lax.broadcasted_iota(jnp.int32, (1, LANES), 1)
    low = lane < HEAD_DIM

    def body(i, carry):
        r0 = pl.multiple_of(i * ATTN_TQ, ATTN_TQ)
        q = q_ref[pl.ds(r0, ATTN_TQ), :]
        kw = kpad[pl.ds(r0, ATTN_TK), :]
        vw = vpad[pl.ds(r0, ATTN_TK), :]
        sel = jnp.where(i == 0, 0, jnp.where(i == nq - 1, 2, 1))
        for pair in range(GROUP_WIDTH // LANES):
            cols = slice(pair * LANES, (pair + 1) * LANES)
            q2, k2, v2 = q[:, cols], kw[:, cols], vw[:, cols]
            outs, lses = [], []
            for hh in range(2):
                hm = low if hh == 0 else jnp.logical_not(low)
                qm = jnp.where(hm, q2, jnp.zeros_like(q2))
                s = lax.dot_general(qm, k2, (((1,), (1,)), ((), ())), preferred_element_type=F32)
                s = s * (HEAD_DIM ** -0.5) + bias_ref[2 * pair + hh, sel]
                m = jnp.max(s, axis=-1, keepdims=True)
                p = jnp.exp(s - m)
                den = jnp.sum(p, axis=-1, keepdims=True)
                o = jnp.dot(p.astype(BF16), v2, preferred_element_type=F32)
                outs.append(o / den)
                lses.append(jnp.broadcast_to(m + jnp.log(den), (ATTN_TQ, LANES)))
            o_ref[pl.ds(r0, ATTN_TQ), cols] = jnp.where(low, outs[0], outs[1]).astype(o_ref.dtype)
            lse_ref[pl.ds(r0, ATTN_TQ), cols] = jnp.where(low, lses[0], lses[1])
        return carry

    lax.fori_loop(0, nq, body, 0)


def _attn_bias(group, dilation):
    slopes = np.exp2(-ALIBI_MAX_EXP * (np.arange(N_ATTN_HEADS, dtype=np.float32) + 1.0) / N_ATTN_HEADS)
    slopes = slopes.reshape(N_PATTERNS, HEADS_PER_GROUP)[group].astype(np.float32)
    qi = np.arange(ATTN_TQ)[:, None]
    kj = np.arange(ATTN_TK)[None, :]
    rel = kj - ATTN_RADIUS - qi
    band = np.abs(rel) <= ATTN_RADIUS
    alibi = -slopes[:, None, None] * (np.abs(rel) * dilation).astype(np.float32)[None]
    kinds = []
    for lo, hi in ((ATTN_RADIUS, ATTN_TK), (0, ATTN_TK), (0, ATTN_TQ + ATTN_RADIUS)):
        valid = band & (kj >= lo) & (kj < hi)
        kinds.append(np.where(valid[None], alibi, np.float32(NEG_INF)))
    return jnp.asarray(np.stack(kinds, axis=1), dtype=F32)


def _attention_group(qkv, group):
    batch, dilation, n, _ = qkv.shape
    assert n % ATTN_TQ == 0 and n >= 2 * ATTN_TQ

    def part(p):
        return pl.BlockSpec((None, None, n, GROUP_WIDTH), lambda b, r: (b, r, 0, p))

    out_blk = pl.BlockSpec((None, None, n, GROUP_WIDTH), lambda b, r: (b, r, 0, 0))
    return pl.pallas_call(
        functools.partial(_attn_kernel, n=n),
        grid=(batch, dilation),
        in_specs=[part(0), part(1), part(2), _const_spec((HEADS_PER_GROUP, 3, ATTN_TQ, ATTN_TK))],
        out_specs=[out_blk, out_blk],
        out_shape=[jax.ShapeDtypeStruct((batch, dilation, n, GROUP_WIDTH), BF16),
                   jax.ShapeDtypeStruct((batch, dilation, n, GROUP_WIDTH), F32)],
        scratch_shapes=[pltpu.VMEM((n + 2 * ATTN_RADIUS, GROUP_WIDTH), BF16)] * 2,
        compiler_params=_cparams(("parallel", "parallel")),
        name=f"attn_g{group}",
    )(qkv, qkv, qkv, _attn_bias(group, dilation))


def _filter_kernel(bands_ref, deltas_ref, w1_ref, b1_ref, w2_ref, b2_ref, w3_ref, b3_ref, w4_ref, freq_ref,
                   ft_ref, *, seq, tl):
    i = pl.program_id(0)
    freq = freq_ref[...]
    lane = lax.broadcasted_iota(jnp.int32, (1, LANES), 1)
    row = lax.broadcasted_iota(jnp.int32, (tl, 1), 0) + i * tl

    def half_filter(pos, half):
        posf = pos.astype(F32)
        t = posf / float(seq - 1)
        z = bands_ref[...] * (2.0 * math.pi * posf / float(seq))
        feat = jnp.where(lane == 0, t,
                         jnp.where(lane <= HYENA_BANDS, jnp.cos(z),
                                   jnp.where(lane <= 2 * HYENA_BANDS, -jnp.sin(z), 0.0)))
        h = jnp.sin(freq * (_mm(feat, w1_ref[...]) + b1_ref[...]))
        h = jnp.sin(freq * (_mm(h, w2_ref[...]) + b2_ref[...]))
        h = jnp.sin(freq * (_mm(h, w3_ref[...]) + b3_ref[...]))
        h = _mm(h, w4_ref[:, half * HYENA_WIDTH:(half + 1) * HYENA_WIDTH])
        return h * jnp.exp(-t * jnp.abs(deltas_ref[...]))

    hf = half_filter(row, 0)
    hb = half_filter(jnp.where(row == 0, 0, seq - row), 1)
    hb = jnp.where(row == 0, 0.0, hb)

    @pl.when(i == 0)
    def _():
        ft_ref[:, 0:CONV_P] = jnp.zeros((HYENA_WIDTH, CONV_P), F32)

    c0 = pl.multiple_of(CONV_P + i * tl, LANES)
    ft_ref[:, pl.ds(c0, tl)] = hb.T
    c1 = pl.multiple_of(CONV_P + seq + i * tl, LANES)
    ft_ref[:, pl.ds(c1, tl)] = hf.T

    @pl.when(i == pl.num_programs(0) - 1)
    def _():
        full = ft_ref[...]
        norm = jnp.sum(jnp.abs(full), axis=1, keepdims=True)
        ft_ref[...] = full / norm


def _hyena_filter(seq, w1, b1, w2, b2, w3, b3, w4, freq):
    tl = 512
    f = jnp.linspace(1e-4, HYENA_BANDS - 1, HYENA_BANDS, dtype=F32)
    bands = jnp.concatenate([jnp.zeros((1,), F32), f, f, jnp.zeros((LANES - HYENA_EMB,), F32)]).reshape(1, LANES)
    max_decay = math.log(HYENA_TARGET) / HYENA_FAST_DECAY
    min_decay = math.log(HYENA_TARGET) / HYENA_SLOW_DECAY
    deltas = jnp.linspace(min_decay, max_decay, HYENA_WIDTH, dtype=F32).reshape(1, HYENA_WIDTH)
    w1p = jnp.pad(w1, ((0, LANES - HYENA_EMB), (0, 0)))
    row = lambda a: a.reshape(1, -1)
    args = (bands, deltas, w1p, row(b1), w2, row(b2), w3, row(b3), w4, row(freq))
    return pl.pallas_call(
        functools.partial(_filter_kernel, seq=seq, tl=tl),
        grid=(seq // tl,),
        in_specs=[_const_spec(a.shape) for a in args],
        out_specs=pl.BlockSpec((HYENA_WIDTH, CONV_P + 2 * seq), lambda i: (0, 0)),
        out_shape=jax.ShapeDtypeStruct((HYENA_WIDTH, CONV_P + 2 * seq), F32),
        compiler_params=_cparams(("arbitrary",)),
        name="hyena_filter",
    )(*args)


def _shift_rows(x, prev_row, next_row):
    n = x.shape[0]
    row = lax.broadcasted_iota(jnp.int32, (n, 1), 0)
    xm = jnp.where(row == 0, prev_row, pltpu.roll(x, 1, axis=0))
    xp = jnp.where(row == n - 1, next_row, pltpu.roll(x, n - 1, axis=0))
    return xm, xp


def _hyena_pre_kernel(x_ref, prev_ref, next_ref, w_ref, b_ref, zt_ref, x0t_ref):
    i = pl.program_id(1)
    x = x_ref[0]
    prev_row = jnp.where(i == 0, 0.0, prev_ref[0, SUBLANES - 1:SUBLANES, :])
    next_row = jnp.where(i == pl.num_programs(1) - 1, 0.0, next_ref[0, 0:1, :])
    xm, xp = _shift_rows(x, prev_row, next_row)
    u = xm * w_ref[0:1, :] + x * w_ref[1:2, :] + xp * w_ref[2:3, :] + b_ref[...]
    c = HYENA_WIDTH
    x0, x1, v = u[:, :c], u[:, c:2 * c], u[:, 2 * c:]
    zt_ref[0] = (v * x1).T
    x0t_ref[0] = x0.T


def _hyena_pre(hy3, short_w, short_b, tl):
    batch, seq, width = hy3.shape
    nt = seq // tl
    hb = tl // SUBLANES
    last = seq // SUBLANES - 1
    out_blk = pl.BlockSpec((1, HYENA_WIDTH, tl), lambda b, i: (b, 0, i))
    return pl.pallas_call(
        _hyena_pre_kernel,
        grid=(batch, nt),
        in_specs=[pl.BlockSpec((1, tl, width), lambda b, i: (b, i, 0)),
                  pl.BlockSpec((1, SUBLANES, width), lambda b, i: (b, jnp.maximum(i * hb - 1, 0), 0)),
                  pl.BlockSpec((1, SUBLANES, width), lambda b, i: (b, jnp.minimum((i + 1) * hb, last), 0)),
                  _const_spec((3, width)), _const_spec((1, width))],
        out_specs=[out_blk, out_blk],
        out_shape=[jax.ShapeDtypeStruct((batch, HYENA_WIDTH, seq), F32)] * 2,
        compiler_params=_cparams(("parallel", "parallel")),
        name="hyena_pre",
    )(hy3, hy3, hy3, short_w, short_b.reshape(1, width))


def _hyena_conv_kernel(bias_ref, zt_ref, x0t_ref, ft_ref, o_ref, troll, zpad, *, seq, cb):
    batch = zt_ref.shape[0]
    nb = seq // CONV_P
    p = CONV_P
    g = pl.program_id(0)
    zero_margin = jnp.zeros((batch, nb, p), F32)
    zpad[:, 0:nb, :] = zero_margin
    zpad[:, 2 * nb:3 * nb, :] = zero_margin
    chunk = 1024

    def channel(ci, carry):
        for w in range(2 * seq // chunk):
            a = w * chunk
            src = ft_ref[pl.ds(ci, 1), a:a + chunk + p]
            rolled = pltpu.roll(jnp.broadcast_to(src, (p, chunk + p)), 0, axis=1, stride=1, stride_axis=0)
            troll[:, a:a + chunk] = rolled[:, p:].astype(BF16)
        z = zt_ref[:, ci, :, :]
        zpad[:, nb:2 * nb, :] = z
        acc = jnp.zeros((batch * nb, p), F32)
        for d in range(-(nb - 1), nb):
            zs = zpad[:, nb - d:2 * nb - d, :].reshape(batch * nb, p).astype(BF16)
            t = troll[:, seq + d * p:seq + (d + 1) * p]
            acc = acc + jnp.dot(zs, t, preferred_element_type=F32)
        bias = bias_ref[g * cb + ci]
        y = (acc.reshape(batch, nb, p) + z * bias) * x0t_ref[:, ci, :, :]
        o_ref[:, ci, :, :] = y
        return carry

    lax.fori_loop(0, cb, channel, 0)


def _hyena_conv(zt, x0t, ft, filt_bias, cb):
    batch, c, seq = zt.shape
    nb = seq // CONV_P
    z4 = zt.reshape(batch, c, nb, CONV_P)
    x4 = x0t.reshape(batch, c, nb, CONV_P)
    blk = pl.BlockSpec((batch, cb, nb, CONV_P), lambda g: (0, g, 0, 0))
    out = pl.pallas_call(
        functools.partial(_hyena_conv_kernel, seq=seq, cb=cb),
        grid=(c // cb,),
        in_specs=[pl.BlockSpec(memory_space=pltpu.SMEM), blk, blk,
                  pl.BlockSpec((cb, CONV_P + 2 * seq), lambda g: (g, 0))],
        out_specs=blk,
        out_shape=jax.ShapeDtypeStruct((batch, c, nb, CONV_P), F32),
        scratch_shapes=[pltpu.VMEM((CONV_P, 2 * seq), BF16),
                        pltpu.VMEM((batch, 3 * nb, CONV_P), F32)],
        compiler_params=_cparams(("parallel",)),
        name="hyena_conv",
    )(filt_bias, z4, x4, ft)
    return out.reshape(batch, c, seq)


def _interleave_classes(blk_ref, scr_ref):
    dil, per, width = blk_ref.shape
    slabs = width // LANES
    for r in range(dil):
        rows = blk_ref[r].astype(F32)
        for cb in range(slabs):
            scr_ref[cb, pl.ds(r, per, stride=dil), :] = rows[:, cb * LANES:(cb + 1) * LANES]
    return jnp.concatenate([scr_ref[cb] for cb in range(slabs)], axis=1)


def _outproj_kernel(o0, o1, o2, l0, l1, l2, hyt_ref, xa_ref, xb_ref, mod_ref, wa_ref, wh_ref, out_ref,
                    so1, so2, sl1, sl2, *, n_first):
    ls = [l0[0], _interleave_classes(l1, sl1), _interleave_classes(l2, sl2)]
    os_ = [o0[0].astype(F32), _interleave_classes(o1, so1), _interleave_classes(o2, so2)]
    x = jnp.where(pl.program_id(0) < n_first, xa_ref[...], xb_ref[...])
    m = jnp.maximum(jnp.maximum(ls[0], ls[1]), ls[2])
    es = [jnp.exp(l - m) for l in ls]
    den = es[0] + es[1] + es[2]
    attn = (es[0] * os_[0] + es[1] * os_[1] + es[2] * os_[2]) / den
    hy = hyt_ref[0].T
    mix = jnp.dot(attn.astype(BF16), wa_ref[...], preferred_element_type=F32)
    mix = mix + jnp.dot(hy.astype(BF16), wh_ref[...], preferred_element_type=F32)
    out_ref[...] = x + mod_ref[0, 2:3, :] * mix


def _out_proj(os_, ls, hyt, xa, xb, mods, w_out_bf, seq, tm):
    d = xa.shape[1]
    rows = xa.shape[0] + xb.shape[0]
    tpb = seq // tm
    wa, wh = w_out_bf[:GROUP_WIDTH], w_out_bf[GROUP_WIDTH:]

    def class_spec(a):
        dil = a.shape[1]
        return pl.BlockSpec((None, dil, tm // dil, GROUP_WIDTH), lambda i: (i // tpb, 0, i % tpb, 0))

    return pl.pallas_call(
        functools.partial(_outproj_kernel, n_first=xa.shape[0] // tm),
        grid=(rows // tm,),
        in_specs=[class_spec(a) for a in (*os_, *ls)] + [
            pl.BlockSpec((1, HYENA_WIDTH, tm), lambda i: (i // tpb, 0, i % tpb))]
        + _two_source_specs(tm, d, xa.shape[0] // tm) + [
            pl.BlockSpec((1, 6, d), lambda i: (i // tpb, 0, 0)),
            _const_spec(wa.shape), _const_spec(wh.shape)],
        out_specs=pl.BlockSpec((tm, d), lambda i: (i, 0)),
        out_shape=jax.ShapeDtypeStruct((rows, d), F32),
        scratch_shapes=[pltpu.VMEM((GROUP_WIDTH // LANES, tm, LANES), F32)] * 4,
        compiler_params=_cparams(("parallel",)),
        name="out_proj",
    )(*os_, *ls, hyt, xa, xb, mods, wa, wh)


def _halo_rows(x_ref, prev_ref, next_ref, tiles_per_seq):
    i = pl.program_id(0)
    first = (i % tiles_per_seq) == 0
    last = (i % tiles_per_seq) == tiles_per_seq - 1
    xe = jnp.concatenate([prev_ref[...], x_ref[...], next_ref[...]], axis=0)
    return xe, first, last


def _ffn_kernel(x_ref, prev_ref, next_ref, mod_ref, gain_ref, wa_ref, wg_ref, cw_ref, cb_ref, wd_ref, fin_ref,
                *outs_and_scratch, tiles_per_seq, fc, final_norm, n_first):
    *out_refs, act_ref, a0_ref, g0_ref, a1_ref, g1_ref = outs_and_scratch
    tm = x_ref.shape[0]
    x = x_ref[...]
    xe, first, last = _halo_rows(x_ref, prev_ref, next_ref, tiles_per_seq)
    he32 = _norm_mod(xe, gain_ref[...], mod_ref[0, 3:4, :], mod_ref[0, 4:5, :])
    he = he32.astype(BF16)
    h = he32[SUBLANES:SUBLANES + tm].astype(BF16)
    row = lax.broadcasted_iota(jnp.int32, (tm + 2 * SUBLANES, 1), 0)
    lo = jnp.where(first, SUBLANES, 0)
    hi = jnp.where(last, tm + SUBLANES, tm + 2 * SUBLANES)
    keep = jnp.logical_and(row >= lo, row < hi)
    nchunk = wa_ref.shape[1] // fc
    n_ext = tm + 2 * SUBLANES

    def up(j, bufs):
        c0 = pl.multiple_of(j * fc, fc)
        bufs[0][...] = jnp.dot(he, wa_ref[:, pl.ds(c0, fc)], preferred_element_type=F32)
        bufs[1][...] = jnp.dot(h, wg_ref[:, pl.ds(c0, fc)], preferred_element_type=F32)

    def activate(j, bufs):
        c0 = pl.multiple_of(j * fc, fc)
        a = jnp.where(keep, bufs[0][...], 0.0)
        am = pltpu.roll(a, 1, axis=0)[SUBLANES:SUBLANES + tm]
        ap = pltpu.roll(a, n_ext - 1, axis=0)[SUBLANES:SUBLANES + tm]
        ac = a[SUBLANES:SUBLANES + tm]
        cw = cw_ref[:, pl.ds(c0, fc)]
        conv = am * cw[0:1] + ac * cw[1:2] + ap * cw[2:3] + cb_ref[:, pl.ds(c0, fc)]
        act_ref[:, pl.ds(c0, fc)] = (jax.nn.gelu(conv) * bufs[1][...]).astype(BF16)

    even, odd = (a0_ref, g0_ref), (a1_ref, g1_ref)
    up(0, even)

    def body(j, carry):
        up(2 * j + 1, odd)
        activate(2 * j, even)
        up(2 * j + 2, even)
        activate(2 * j + 1, odd)
        return carry

    assert nchunk % 2 == 1
    lax.fori_loop(0, nchunk // 2, body, 0)
    activate(nchunk - 1, even)
    down = jnp.dot(act_ref[...], wd_ref[...], preferred_element_type=F32)
    y = x + mod_ref[0, 5:6, :] * down
    if final_norm:
        ms = jnp.mean(y * y, axis=-1, keepdims=True)
        y = y * lax.rsqrt(ms + RMS_EPS) * fin_ref[...]
    if n_first is None:
        out_refs[0][...] = y
    else:
        i = pl.program_id(0)

        @pl.when(i < n_first)
        def _():
            out_refs[0][...] = y

        @pl.when(i >= n_first)
        def _():
            out_refs[1][...] = y


def _halo_specs(tm, d, nrows):
    hb = tm // SUBLANES
    last = nrows // SUBLANES - 1
    return [pl.BlockSpec((tm, d), lambda i: (i, 0)),
            pl.BlockSpec((SUBLANES, d), lambda i: (jnp.maximum(i * hb - 1, 0), 0)),
            pl.BlockSpec((SUBLANES, d), lambda i: (jnp.minimum((i + 1) * hb, last), 0))]


def _conv_ffn(x2, mods, gain, up_bf, conv_w, conv_b, down_bf, fin_gain, seq, tm, final_norm, split_rows=None):
    rows, d = x2.shape
    dff = down_bf.shape[0]
    fc = 256
    tpb = seq // tm
    wa, wg = up_bf[:, :dff], up_bf[:, dff:]
    if split_rows is None:
        n_first = None
        out_specs = pl.BlockSpec((tm, d), lambda i: (i, 0))
        out_shape = jax.ShapeDtypeStruct((rows, d), F32)
    else:
        n_first = split_rows // tm
        out_specs = _two_source_specs(tm, d, n_first)
        out_shape = [jax.ShapeDtypeStruct((split_rows, d), F32), jax.ShapeDtypeStruct((rows - split_rows, d), F32)]
    return pl.pallas_call(
        functools.partial(_ffn_kernel, tiles_per_seq=tpb, fc=fc, final_norm=final_norm, n_first=n_first),
        grid=(rows // tm,),
        in_specs=_halo_specs(tm, d, rows) + [
            pl.BlockSpec((1, 6, d), lambda i: (i // tpb, 0, 0)),
            _const_spec((1, d)), _const_spec(wa.shape), _const_spec(wg.shape),
            _const_spec((3, dff)), _const_spec((1, dff)), _const_spec(down_bf.shape), _const_spec((1, d))],
        out_specs=out_specs,
        out_shape=out_shape,
        scratch_shapes=[pltpu.VMEM((tm, dff), BF16)]
        + [pltpu.VMEM((tm + 2 * SUBLANES, fc), F32), pltpu.VMEM((tm, fc), F32)] * 2,
        compiler_params=_cparams(("arbitrary",)),
        name="conv_ffn",
    )(x2, x2, x2, mods, gain.reshape(1, d), wa, wg, conv_w, conv_b.reshape(1, dff), down_bf,
      fin_gain.reshape(1, d))


def _head_sum(x, ones_bd):
    hi = x.astype(BF16)
    lo = (x - hi.astype(F32)).astype(BF16)
    parts = []
    for c in range(x.shape[1] // MXU_DIM):
        cols = slice(c * MXU_DIM, (c + 1) * MXU_DIM)
        parts.append(jnp.dot(hi[:, cols], ones_bd, preferred_element_type=F32)
                     + jnp.dot(lo[:, cols], ones_bd, preferred_element_type=F32))
    return jnp.concatenate(parts, axis=1)


def _ones_blockdiag():
    r = lax.broadcasted_iota(jnp.int32, (MXU_DIM, MXU_DIM), 0) // RWKV_HEAD
    c = lax.broadcasted_iota(jnp.int32, (MXU_DIM, MXU_DIM), 1) // RWKV_HEAD
    return jnp.where(r == c, 1.0, 0.0).astype(BF16)


def _rwkv_pre_kernel(x_ref, prev_ref, next_ref, mod_ref, gain_ref, mu_ref, wr_ref, wk_ref, wv_ref,
                     w1_ref, w2_ref, w0_ref, a1_ref, a2_ref, a0_ref, g1_ref, g2_ref, kk_w_ref, ka_ref, rk_ref,
                     r_out, v_out, kk_out, k_out, a0_out, a1_out, lw0_out, lw1_out, g_out, bonus_out,
                     *, tiles_per_seq):
    tm, d = x_ref.shape
    xe, first, last = _halo_rows(x_ref, prev_ref, next_ref, tiles_per_seq)
    he = _norm_mod(xe, gain_ref[...], mod_ref[0, 0:1, :], mod_ref[0, 1:2, :])
    h = he[SUBLANES:SUBLANES + tm]
    prev_row = jnp.where(first, 0.0, he[SUBLANES - 1:SUBLANES])
    next_row = jnp.where(last, 0.0, he[tm + SUBLANES:tm + SUBLANES + 1])
    hm, hp = _shift_rows(h, prev_row, next_row)
    xx = 0.5 * (hm + hp) - h

    def mixed(i):
        return (h + xx * mu_ref[i:i + 1, :]).astype(BF16)

    r = jnp.dot(mixed(0), wr_ref[...], preferred_element_type=F32)
    k = jnp.dot(mixed(2), wk_ref[...], preferred_element_type=F32)
    v = jnp.dot(mixed(3), wv_ref[...], preferred_element_type=F32)
    gl = jax.nn.sigmoid(jnp.dot(mixed(5), g1_ref[...], preferred_element_type=F32))
    g = jnp.dot(gl.astype(BF16), g2_ref[...], preferred_element_type=F32)
    wl = jnp.tanh(jnp.dot(mixed(1), w1_ref[...], preferred_element_type=F32))
    wl = jnp.dot(wl.astype(BF16), w2_ref[...], preferred_element_type=F32)
    al = jnp.dot(mixed(4), a1_ref[...], preferred_element_type=F32)
    al = jnp.dot(al.astype(BF16), a2_ref[...], preferred_element_type=F32)

    ones_bd = _ones_blockdiag()
    kk = k * kk_w_ref[...]
    nrm = jnp.sqrt(_head_sum(kk * kk, ones_bd))
    kk = kk / jnp.maximum(nrm, 1e-12)
    r_out[...] = r.astype(BF16)
    v_out[...] = v.astype(BF16)
    kk_out[...] = kk.astype(BF16)
    k_out[...] = k.astype(BF16)
    g_out[...] = g
    ksum = jnp.zeros_like(k)
    for direction, (lw_o, a_o) in enumerate(((lw0_out, a0_out), (lw1_out, a1_out))):
        cols = slice(direction * d, (direction + 1) * d)
        u = w0_ref[direction:direction + 1, :] + wl[:, cols]
        lw_o[...] = -jax.nn.sigmoid(u) * math.exp(-0.5)
        a = jax.nn.sigmoid(a0_ref[direction:direction + 1, :] + al[:, cols])
        a_o[...] = a.astype(BF16)
        ksum = ksum + _k_dir(k, a, ka_ref[...])
    bonus_out[...] = _head_sum(r * (0.5 * ksum) * rk_ref[...], ones_bd) * v


def _k_dir(k, a, k_a):
    return k * (1.0 + (a - 1.0) * k_a)


def _blockdiag2(m):
    z = jnp.zeros_like(m[0])
    return jnp.concatenate([jnp.concatenate([m[0], z], axis=1), jnp.concatenate([z, m[1]], axis=1)], axis=0)


def _rwkv_pre(x2, mods, p, seq, tm):
    rows, d = x2.shape
    tpb = seq // tm
    bf = lambda a: a.astype(BF16)
    w1 = bf(jnp.concatenate([p['w1'][0], p['w1'][1]], axis=1))
    w2 = bf(_blockdiag2(p['w2']))
    a1 = bf(jnp.concatenate([p['a1'][0], p['a1'][1]], axis=1))
    a2 = bf(_blockdiag2(p['a2']))
    glora = p['g1'].shape[1]
    gpad = -glora % LANES
    g1 = bf(jnp.pad(p['g1'], ((0, 0), (0, gpad))))
    g2 = bf(jnp.pad(p['g2'], ((0, gpad), (0, 0))))
    row = lambda a: a.reshape(1, d)
    args = (x2, x2, x2, mods, row(p['norm1']), p['mu'], bf(p['w_r']), bf(p['w_k']), bf(p['w_v']),
            w1, w2, p['w0'], a1, a2, p['a0'], g1, g2, row(p['k_k']), row(p['k_a']), row(p['r_k']))
    in_specs = _halo_specs(tm, d, rows) + [pl.BlockSpec((1, 6, d), lambda i: (i // tpb, 0, 0))]
    in_specs += [_const_spec(a.shape) for a in args[4:]]
    ob = pl.BlockSpec((tm, d), lambda i: (i, 0))
    return pl.pallas_call(
        functools.partial(_rwkv_pre_kernel, tiles_per_seq=tpb),
        grid=(rows // tm,),
        in_specs=in_specs,
        out_specs=[ob] * 10,
        out_shape=[jax.ShapeDtypeStruct((rows, d), BF16)] * 6 + [jax.ShapeDtypeStruct((rows, d), F32)] * 4,
        compiler_params=_cparams(("parallel",)),
        name="rwkv_pre",
    )(*args)


def _scan_masks(reverse):
    c = SCAN_C
    t = lax.broadcasted_iota(jnp.int32, (c, 4 * c), 0)
    s = lax.broadcasted_iota(jnp.int32, (c, 4 * c), 1) % c
    before = (s > t) if reverse else (s < t)
    upto = jnp.logical_or(before, s == t)
    eye = jnp.where(s == t, 1.0, 0.0)
    rr = lax.broadcasted_iota(jnp.int32, (4 * c, SCAN_HG), 0)
    cc = lax.broadcasted_iota(jnp.int32, (4 * c, SCAN_HG), 1)
    same_head = jnp.where((rr // c) == (cc // RWKV_HEAD), 1.0, 0.0)
    eye_big = jnp.where(rr == cc, 1.0, 0.0)
    return before, upto, eye, same_head, same_head.astype(BF16), eye_big


def _bd(xb, head_mask_bf):
    return jnp.concatenate([xb] * 4, axis=0) * head_mask_bf


def _mm(a, b):
    return jnp.dot(a, b, precision=HIGHEST, preferred_element_type=F32)


def _bdot(a, b):
    return jnp.dot(a, b, preferred_element_type=F32)


def _bdot_nt(a, b):
    return lax.dot_general(a, b, (((1,), (1,)), ((), ())), preferred_element_type=F32)


def _cumsum_rows(tri_bf, x):
    hi = x.astype(BF16)
    rest = x - hi.astype(F32)
    mid = rest.astype(BF16)
    lo = (rest - mid.astype(F32)).astype(BF16)
    return _bdot(tri_bf, hi) + _bdot(tri_bf, mid) + _bdot(tri_bf, lo)


def _scan_prepare(chains):
    c = SCAN_C
    bf = lambda x: x.astype(BF16)
    idx = range(len(chains))
    r, v, kk, lw, k, a = [[ch[0][j] for ch in chains] for j in range(6)]
    tri, before, upto, eye = [[ch[1][j] for ch in chains] for j in range(4)]
    head_mask_bf = chains[0][1][5]
    rev = [ch[2] for ch in chains]
    bd = lambda x: _bd(bf(x), head_mask_bf)

    g = [_cumsum_rows(tri[i], lw[i]) for i in idx]
    gtot = [g[i][0:1] if rev[i] else g[i][c - 1:c] for i in idx]
    kkd = [kk[i] * jnp.exp(g[i] - lw[i]) for i in idx]
    rd = [r[i] * jnp.exp(g[i]) for i in idx]
    b = [kk[i] * a[i] for i in idx]
    ginv = [jnp.exp(-g[i]) for i in idx]
    lhs = [bf(jnp.concatenate([kkd[i], rd[i]], axis=0)) for i in idx]
    a1 = [_bdot_nt(lhs[i], bd(b[i] * ginv[i])) for i in idx]
    a2 = [_bdot_nt(lhs[i], bd(k[i] * ginv[i])) for i in idx]
    n_ab = [jnp.where(before[i], a1[i][:c], 0.0) for i in idx]
    a_rb = [jnp.where(upto[i], a1[i][c:], 0.0) for i in idx]
    a_k = [jnp.concatenate([jnp.where(before[i], a2[i][:c], 0.0), jnp.where(upto[i], a2[i][c:], 0.0)], axis=0)
           for i in idx]

    nb = [bf(n_ab[i]) for i in idx]
    npow = [_bdot(nb[i], _bd(nb[i], head_mask_bf)) for i in idx]
    tinv = [eye[i] - n_ab[i] for i in idx]
    steps = int(math.log2(c)) - 1
    for step in range(steps):
        pb = [bf(npow[i]) for i in idx]
        rhs = [_bd(pb[i], head_mask_bf) for i in idx]
        if step == steps - 1:
            tinv = [tinv[i] + _bdot(bf(tinv[i]), rhs[i]) for i in idx]
        else:
            prod = [_bdot(jnp.concatenate([pb[i], bf(tinv[i])], axis=0), rhs[i]) for i in idx]
            tinv = [tinv[i] + prod[i][c:] for i in idx]
            npow = [prod[i][:c] for i in idx]

    vb = [bf(v[i]) for i in idx]
    av = [_bdot(bf(a_k[i]), _bd(vb[i], head_mask_bf)) for i in idx]
    tw = [_bdot(bf(tinv[i]), jnp.concatenate([bd(kkd[i]), bd(av[i][:c])], axis=1)) for i in idx]
    kkt = [tw[i][:, :SCAN_HG] for i in idx]
    wt = [tw[i][:, SCAN_HG:] for i in idx]
    aw = [_bdot(bf(a_rb[i]), jnp.concatenate([bd(kkt[i]), bd(wt[i])], axis=1)) for i in idx]
    lhs_state = [bf(jnp.concatenate([rd[i] - aw[i][:, :SCAN_HG], kkt[i]], axis=0)) for i in idx]
    ypre = [av[i][c:] - aw[i][:, SCAN_HG:] for i in idx]
    gend = [jnp.exp(gtot[i] - g[i]) for i in idx]
    upd_t = [bf(jnp.concatenate([b[i] * gend[i], k[i] * gend[i]], axis=0).T) for i in idx]
    gcol_src = [jnp.concatenate([g[i], g[i]], axis=0).T for i in idx]
    gcol = [jnp.exp(gcol_src[i][:, 0:1] if rev[i] else gcol_src[i][:, c - 1:c]) for i in idx]
    return [(lhs_state[i], wt[i], ypre[i], upd_t[i], gcol[i], vb[i]) for i in idx]


def _scan_apply(preps, states, head_mask):
    c = SCAN_C
    idx = range(len(preps))
    o = [_bdot(preps[i][0], states[i].astype(BF16)) for i in idx]
    y = [o[i][:c] + preps[i][2] for i in idx]
    u = [-(preps[i][1] + o[i][c:]) for i in idx]
    upd = [_bdot(preps[i][3], jnp.concatenate([u[i].astype(BF16), preps[i][5]], axis=0)) for i in idx]
    return y, [states[i] * preps[i][4] + head_mask * upd[i] for i in idx]


def _scan_consts(reverse):
    c = SCAN_C
    before, upto, eye, head_mask, head_mask_bf, _ = _scan_masks(reverse)
    tt = lax.broadcasted_iota(jnp.int32, (c, c), 0)
    ss = lax.broadcasted_iota(jnp.int32, (c, c), 1)
    tri = jnp.where((ss >= tt) if reverse else (ss <= tt), 1.0, 0.0).astype(BF16)
    return tri, before, upto, eye, head_mask, head_mask_bf


def _scan_kernel(ka_ref, rf_ref, vf_ref, kkf_ref, kf_ref, af_ref, lwf_ref, rb_ref, vb_ref, kkb_ref, kb_ref, ab_ref,
                 lwb_ref, yf_ref, yb_ref, sf_ref, sb_ref):
    tb = rf_ref.shape[0]
    nchunk = tb // SCAN_C

    @pl.when(pl.program_id(2) == 0)
    def _():
        sf_ref[...] = jnp.zeros_like(sf_ref)
        sb_ref[...] = jnp.zeros_like(sb_ref)

    consts_f = _scan_consts(False)
    consts_b = _scan_consts(True)
    head_mask = consts_f[4]
    k_a = ka_ref[...]

    def operands(refs, rows):
        r, v, kk, k, a = (ref[rows, :].astype(F32) for ref in refs[:5])
        return r, v, kk, refs[5][rows, :], _k_dir(k, a, k_a), a

    fwd_refs = (rf_ref, vf_ref, kkf_ref, kf_ref, af_ref, lwf_ref)
    bwd_refs = (rb_ref, vb_ref, kkb_ref, kb_ref, ab_ref, lwb_ref)

    def body(j, carry):
        rows, chains = [], []
        for u in range(SCAN_UNROLL):
            rows_f = pl.ds(pl.multiple_of((j * SCAN_UNROLL + u) * SCAN_C, SCAN_C), SCAN_C)
            rows_b = pl.ds(pl.multiple_of((nchunk - 1 - j * SCAN_UNROLL - u) * SCAN_C, SCAN_C), SCAN_C)
            rows += [rows_f, rows_b]
            chains.append((operands(fwd_refs, rows_f), consts_f, False))
            chains.append((operands(bwd_refs, rows_b), consts_b, True))
        preps = _scan_prepare(chains)
        states = [sf_ref[...], sb_ref[...]]
        for u in range(SCAN_UNROLL):
            ys, states = _scan_apply(preps[2 * u:2 * u + 2], states, head_mask)
            yf_ref[rows[2 * u], :] = ys[0].astype(yf_ref.dtype)
            yb_ref[rows[2 * u + 1], :] = ys[1].astype(yb_ref.dtype)
        sf_ref[...] = states[0]
        sb_ref[...] = states[1]
        return carry

    lax.fori_loop(0, nchunk // SCAN_UNROLL, body, 0)


def _wkv_scan(r, v, kk, k, a0, a1, lw0, lw1, k_a, batch, seq, tb):
    rows, d = r.shape
    nt = seq // tb
    ng = d // SCAN_HG
    fwd = pl.BlockSpec((tb, SCAN_HG), lambda b, g, t: (b * nt + t, g))
    bwd = pl.BlockSpec((tb, SCAN_HG), lambda b, g, t: (b * nt + nt - 1 - t, g))
    return pl.pallas_call(
        _scan_kernel,
        grid=(batch, ng, nt),
        in_specs=[pl.BlockSpec((1, SCAN_HG), lambda b, g, t: (0, g))] + [fwd] * 6 + [bwd] * 6,
        out_specs=[fwd, bwd],
        out_shape=[jax.ShapeDtypeStruct((rows, d), BF16)] * 2,
        scratch_shapes=[pltpu.VMEM((SCAN_HG, SCAN_HG), F32)] * 2,
        compiler_params=_cparams(("parallel", "parallel", "arbitrary")),
        name="wkv_scan",
    )(k_a.reshape(1, d), r, v, kk, k, a0, lw0, r, v, kk, k, a1, lw1)


def _rwkv_post_kernel(yf_ref, yb_ref, bonus_ref, g_ref, x_ref, mod_ref, lnw_ref, lnb_ref, wo_ref, out_ref):
    y = yf_ref[...].astype(F32) + yb_ref[...].astype(F32)
    ones_bd = _ones_blockdiag()
    mu = _head_sum(y, ones_bd) * (1.0 / RWKV_HEAD)
    yc = y - mu
    var = _head_sum(yc * yc, ones_bd) * (1.0 / RWKV_HEAD)
    yn = yc * lax.rsqrt(var + GN_EPS) * lnw_ref[...] + lnb_ref[...]
    out = ((yn + bonus_ref[...]) * g_ref[...]).astype(BF16)
    mix = jnp.dot(out, wo_ref[...], preferred_element_type=F32)
    out_ref[...] = x_ref[...] + mod_ref[0, 2:3, :] * mix


def _rwkv_post(yf, yb, bonus, g, x2, mods, ln_w, ln_b, wo_bf, seq, tm):
    rows, d = x2.shape
    tpb = seq // tm
    rb = pl.BlockSpec((tm, d), lambda i: (i, 0))
    return pl.pallas_call(
        _rwkv_post_kernel,
        grid=(rows // tm,),
        in_specs=[rb] * 5 + [pl.BlockSpec((1, 6, d), lambda i: (i // tpb, 0, 0)),
                             _const_spec((1, d)), _const_spec((1, d)), _const_spec((d, d))],
        out_specs=rb,
        out_shape=jax.ShapeDtypeStruct((rows, d), F32),
        compiler_params=_cparams(("parallel",)),
        name="rwkv_post",
    )(yf, yb, bonus, g, x2, mods, ln_w.reshape(1, d), ln_b.reshape(1, d), wo_bf)


def _mods(c, w, b):
    batch, d = c.shape
    pad = -batch % SUBLANES
    m = _ada_params(jnp.pad(c, ((0, pad), (0, 0))), w, b)[:batch]
    return m.reshape(batch, 6, d)


def _group_major_w_in(w_in):
    d = w_in.shape[0]
    qkv = w_in[:, :3 * ATTN_WIDTH].reshape(d, 3, N_PATTERNS, GROUP_WIDTH)
    qkv = jnp.transpose(qkv, (0, 2, 1, 3)).reshape(d, 3 * ATTN_WIDTH)
    return jnp.concatenate([qkv, w_in[:, 3 * ATTN_WIDTH:]], axis=1)


def _trunk(xa, xb, c, l0, l1, final_norm):
    seq, d = xa.shape[1:]
    assert xb.shape[1:] == (seq, d)
    batch = xa.shape[0] + xb.shape[0]
    xa2, xb2 = xa.reshape(-1, d), xb.reshape(-1, d)
    bf = lambda a: a.astype(BF16)
    tm = 512

    mods0 = _mods(c, l0['ada_w'], l0['ada_b'])
    q0, q1, q2, hy = _in_proj(xa2, xb2, mods0, l0['norm1'], bf(_group_major_w_in(l0['w_in'])), seq, tm)
    os_, ls = [], []
    for group, qkv in enumerate((q0.reshape(batch, 1, seq, 3 * GROUP_WIDTH), q1, q2)):
        o, lse = _attention_group(qkv, group)
        os_.append(o)
        ls.append(lse)
    ft = _hyena_filter(seq, l0['filt_w1'], l0['filt_b1'], l0['filt_w2'], l0['filt_b2'], l0['filt_w3'],
                       l0['filt_b3'], l0['filt_w4'], l0['filt_freq'])
    zt, x0t = _hyena_pre(hy.reshape(batch, seq, 3 * HYENA_WIDTH), l0['short_w'], l0['short_b'], tl=512)
    hyt = _hyena_conv(zt, x0t, ft, l0['filt_bias'], cb=8)
    x2 = _out_proj(os_, ls, hyt, xa2, xb2, mods0, bf(l0['w_out']), seq, tm)
    x2 = _conv_ffn(x2, mods0, l0['norm2'], bf(l0['ffn_up']), l0['ffn_conv_w'], l0['ffn_conv_b'],
                   bf(l0['ffn_down']), final_norm, seq, tm, final_norm=False)

    mods1 = _mods(c, l1['ada_w'], l1['ada_b'])
    r, v, kk, k, a0, a1, lw0, lw1, g, bonus = _rwkv_pre(x2, mods1, l1, seq, tm=256)
    yf, yb = _wkv_scan(r, v, kk, k, a0, a1, lw0, lw1, l1['k_a'], batch, seq, tb=512)
    x2 = _rwkv_post(yf, yb, bonus, g, x2, mods1, l1['ln_w'], l1['ln_b'], bf(l1['w_o']), seq, tm)
    ya, yb_ = _conv_ffn(x2, mods1, l1['norm2'], bf(l1['ffn_up']), l1['ffn_conv_w'], l1['ffn_conv_b'],
                        bf(l1['ffn_down']), final_norm, seq, tm, final_norm=True, split_rows=xa2.shape[0])
    return ya.reshape(xa.shape), yb_.reshape(xb.shape)


def kernel(x_prompt, x_sample, c_prompt, c_sample, l0_ada_w, l0_ada_b, l0_norm1, l0_norm2, l0_w_in, l0_short_w, l0_short_b, l0_filt_w1, l0_filt_b1, l0_filt_w2, l0_filt_b2, l0_filt_w3, l0_filt_b3, l0_filt_w4, l0_filt_freq, l0_filt_bias, l0_w_out, l0_ffn_up, l0_ffn_conv_w, l0_ffn_conv_b, l0_ffn_down, l1_ada_w, l1_ada_b, l1_norm1, l1_norm2, l1_mu, l1_w_r, l1_w_k, l1_w_v, l1_w_o, l1_w0, l1_w1, l1_w2, l1_a0, l1_a1, l1_a2, l1_g1, l1_g2, l1_k_k, l1_k_a, l1_r_k, l1_ln_w, l1_ln_b, l1_ffn_up, l1_ffn_conv_w, l1_ffn_conv_b, l1_ffn_down, final_norm):
    layer0 = dict(ada_w=l0_ada_w, ada_b=l0_ada_b, norm1=l0_norm1, norm2=l0_norm2, w_in=l0_w_in,
                  short_w=l0_short_w, short_b=l0_short_b, filt_w1=l0_filt_w1, filt_b1=l0_filt_b1,
                  filt_w2=l0_filt_w2, filt_b2=l0_filt_b2, filt_w3=l0_filt_w3, filt_b3=l0_filt_b3,
                  filt_w4=l0_filt_w4, filt_freq=l0_filt_freq, filt_bias=l0_filt_bias, w_out=l0_w_out,
                  ffn_up=l0_ffn_up, ffn_conv_w=l0_ffn_conv_w, ffn_conv_b=l0_ffn_conv_b, ffn_down=l0_ffn_down)
    layer1 = dict(ada_w=l1_ada_w, ada_b=l1_ada_b, norm1=l1_norm1, norm2=l1_norm2, mu=l1_mu,
                  w_r=l1_w_r, w_k=l1_w_k, w_v=l1_w_v, w_o=l1_w_o, w0=l1_w0, w1=l1_w1, w2=l1_w2,
                  a0=l1_a0, a1=l1_a1, a2=l1_a2, g1=l1_g1, g2=l1_g2, k_k=l1_k_k, k_a=l1_k_a,
                  r_k=l1_r_k.reshape(-1), ln_w=l1_ln_w, ln_b=l1_ln_b, ffn_up=l1_ffn_up,
                  ffn_conv_w=l1_ffn_conv_w, ffn_conv_b=l1_ffn_conv_b, ffn_down=l1_ffn_down)
    c = jnp.concatenate([c_prompt, c_sample], axis=0)
    return _trunk(x_prompt, x_sample, c, layer0, layer1, final_norm)
```

```python
import functools
import math

import jax
import jax.numpy as jnp
import numpy as np
from jax import lax
from jax.experimental import pallas as pl
from jax.experimental.pallas import tpu as pltpu

F32 = jnp.float32
BF16 = jnp.bfloat16
HIGHEST = lax.Precision.HIGHEST

HEAD_DIM = 64
ATTN_PATTERNS = ((128, 1), (512, 4), (2048, 16))
N_PATTERNS = 3
HEADS_PER_GROUP = 4
N_ATTN_HEADS = 12
ATTN_WIDTH = 768
GROUP_WIDTH = HEADS_PER_GROUP * HEAD_DIM
ALIBI_MAX_EXP = 8.0
HYENA_WIDTH = 256
HYENA_BANDS = 16
HYENA_EMB = 33
HYENA_FFN = 64
HYENA_TARGET = 1e-2
HYENA_FAST_DECAY = 0.3
HYENA_SLOW_DECAY = 1.5
RWKV_HEAD = 64
RMS_EPS = 1e-6
GN_EPS = 64e-5
NEG_INF = -1e30

LANES = 128
SUBLANES = 8
MXU_DIM = 256
VMEM_LIMIT = 56 * 1024 * 1024

ATTN_RADIUS = 64
ATTN_TQ = 128
ATTN_TK = ATTN_TQ + 2 * ATTN_RADIUS
CONV_P = 256
SCAN_C = 64
SCAN_HG = 256
SCAN_UNROLL = 4


def _cparams(sem):
    return pltpu.CompilerParams(dimension_semantics=sem, vmem_limit_bytes=VMEM_LIMIT)


def _const_spec(shape):
    nd = len(shape)
    return pl.BlockSpec(shape, lambda *_: (0,) * nd, pipeline_mode=pl.Buffered(1))


def _ada_kernel(c_ref, w_ref, b_ref, o_ref):
    c = c_ref[...]
    s = c * jax.nn.sigmoid(c)
    o_ref[...] = jnp.dot(s, w_ref[...], precision=HIGHEST, preferred_element_type=F32) + b_ref[...]


def _ada_params(c_pad, w, b):
    bp, d = c_pad.shape
    n = w.shape[1]
    tn = 1024
    return pl.pallas_call(
        _ada_kernel,
        grid=(n // tn,),
        in_specs=[_const_spec((bp, d)),
                  pl.BlockSpec((d, tn), lambda j: (0, j)),
                  pl.BlockSpec((1, tn), lambda j: (0, j))],
        out_specs=pl.BlockSpec((bp, tn), lambda j: (0, j)),
        out_shape=jax.ShapeDtypeStruct((bp, n), F32),
        compiler_params=_cparams(("parallel",)),
        name="ada_params",
    )(c_pad, w, b.reshape(1, n))


def _norm_mod(x, gain, shift, scale):
    ms = jnp.mean(x * x, axis=-1, keepdims=True)
    row_gain = gain * (1.0 + scale)
    return (x * lax.rsqrt(ms + RMS_EPS)) * row_gain + shift


def _two_source_specs(tm, d, n_first):
    return [pl.BlockSpec((tm, d), lambda i: (jnp.minimum(i, n_first - 1), 0)),
            pl.BlockSpec((tm, d), lambda i: (jnp.maximum(i - n_first, 0), 0))]


def _inproj_kernel(xa_ref, xb_ref, mod_ref, gain_ref, w_ref, q0_ref, q1_ref, q2_ref, hy_ref, scr_ref, *, n_first):
    tm = xa_ref.shape[0]
    x = jnp.where(pl.program_id(0) < n_first, xa_ref[...], xb_ref[...])
    h = _norm_mod(x, gain_ref[...], mod_ref[0, 0:1, :], mod_ref[0, 1:2, :])
    p = jnp.dot(h.astype(BF16), w_ref[...], preferred_element_type=F32)
    gw = 3 * GROUP_WIDTH
    q0_ref[...] = p[:, :gw].astype(BF16)
    for g, out_ref in ((1, q1_ref), (2, q2_ref)):
        dil = out_ref.shape[1]
        for cb in range(gw // LANES):
            cols = slice(cb * LANES, (cb + 1) * LANES)
            scr_ref[cb] = p[:, g * gw + cb * LANES:g * gw + (cb + 1) * LANES]
            for r in range(dil):
                out_ref[0, r, :, cols] = scr_ref[cb, pl.ds(r, tm // dil, stride=dil), :].astype(BF16)
    hy_ref[...] = p[:, N_PATTERNS * gw:]


def _in_proj(xa, xb, mods, gain, w_bf, seq, tm):
    d = xa.shape[1]
    rows = xa.shape[0] + xb.shape[0]
    batch = rows // seq
    n = w_bf.shape[1]
    gw = 3 * GROUP_WIDTH
    tpb = seq // tm
    dils = [dil for _, dil in ATTN_PATTERNS]
    assert dils[0] == 1

    def class_spec(dil):
        return pl.BlockSpec((1, dil, tm // dil, gw), lambda i: (i // tpb, 0, i % tpb, 0))

    return pl.pallas_call(
        functools.partial(_inproj_kernel, n_first=xa.shape[0] // tm),
        grid=(rows // tm,),
        in_specs=_two_source_specs(tm, d, xa.shape[0] // tm) + [
            pl.BlockSpec((1, 6, d), lambda i: (i // tpb, 0, 0)),
            _const_spec((1, d)),
            _const_spec((d, n))],
        out_specs=[pl.BlockSpec((tm, gw), lambda i: (i, 0)), class_spec(dils[1]), class_spec(dils[2]),
                   pl.BlockSpec((tm, n - N_PATTERNS * gw), lambda i: (i, 0))],
        out_shape=[jax.ShapeDtypeStruct((rows, gw), BF16)]
        + [jax.ShapeDtypeStruct((batch, dil, seq // dil, gw), BF16) for dil in dils[1:]]
        + [jax.ShapeDtypeStruct((rows, n - N_PATTERNS * gw), F32)],
        scratch_shapes=[pltpu.VMEM((gw // LANES, tm, LANES), F32)],
        compiler_params=_cparams(("parallel",)),
        name="in_proj",
    )(xa, xb, mods, gain.reshape(1, d), w_bf)


def _attn_kernel(q_ref, k_ref, v_ref, bias_ref, o_ref, lse_ref, kpad, vpad, *, n):
    nq = n // ATTN_TQ
    zeros = jnp.zeros((ATTN_RADIUS, GROUP_WIDTH), BF16)
    kpad[0:ATTN_RADIUS, :] = zeros
    vpad[0:ATTN_RADIUS, :] = zeros
    kpad[n + ATTN_RADIUS:n + 2 * ATTN_RADIUS, :] = zeros
    vpad[n + ATTN_RADIUS:n + 2 * ATTN_RADIUS, :] = zeros
    kpad[ATTN_RADIUS:n + ATTN_RADIUS, :] = k_ref[...]
    vpad[ATTN_RADIUS:n + ATTN_RADIUS, :] = v_ref[...]
    lane = lax.broadcasted_iota(jnp.int32, (1, LANES), 1)
    low = lane < HEAD_DIM

    def body(i, carry):
        r0 = pl.multiple_of(i * ATTN_TQ, ATTN_TQ)
        q = q_ref[pl.ds(r0, ATTN_TQ), :]
        kw = kpad[pl.ds(r0, ATTN_TK), :]
        vw = vpad[pl.ds(r0, ATTN_TK), :]
        sel = jnp.where(i == 0, 0, jnp.where(i == nq - 1, 2, 1))
        heads = range(HEADS_PER_GROUP)
        cols = [slice((h // 2) * LANES, (h // 2 + 1) * LANES) for h in heads]
        qm = [jnp.where(low if h % 2 == 0 else jnp.logical_not(low), q[:, cols[h]], jnp.zeros((), BF16))
              for h in heads]
        s = [lax.dot_general(qm[h], kw[:, cols[h]], (((1,), (1,)), ((), ())), preferred_element_type=F32)
             for h in heads]
        s = [s[h] * (HEAD_DIM ** -0.5) + bias_ref[h, sel] for h in heads]
        m = [jnp.max(s[h], axis=-1, keepdims=True) for h in heads]
        p = [jnp.exp(s[h] - m[h]) for h in heads]
        den = [jnp.sum(p[h], axis=-1, keepdims=True) for h in heads]
        o = [jnp.dot(p[h].astype(BF16), vw[:, cols[h]], preferred_element_type=F32) for h in heads]
        o = [o[h] / den[h] for h in heads]
        lse = [jnp.broadcast_to(m[h] + jnp.log(den[h]), (ATTN_TQ, LANES)) for h in heads]
        for pair in range(GROUP_WIDTH // LANES):
            o_ref[pl.ds(r0, ATTN_TQ), cols[2 * pair]] = jnp.where(low, o[2 * pair], o[2 * pair + 1]).astype(o_ref.dtype)
            lse_ref[pl.ds(r0, ATTN_TQ), cols[2 * pair]] = jnp.where(low, lse[2 * pair], lse[2 * pair + 1])
        return carry

    lax.fori_loop(0, nq, body, 0)


def _attn_bias(group, dilation):
    slopes = np.exp2(-ALIBI_MAX_EXP * (np.arange(N_ATTN_HEADS, dtype=np.float32) + 1.0) / N_ATTN_HEADS)
    slopes = slopes.reshape(N_PATTERNS, HEADS_PER_GROUP)[group].astype(np.float32)
    qi = np.arange(ATTN_TQ)[:, None]
    kj = np.arange(ATTN_TK)[None, :]
    rel = kj - ATTN_RADIUS - qi
    band = np.abs(rel) <= ATTN_RADIUS
    alibi = -slopes[:, None, None] * (np.abs(rel) * dilation).astype(np.float32)[None]
    kinds = []
    for lo, hi in ((ATTN_RADIUS, ATTN_TK), (0, ATTN_TK), (0, ATTN_TQ + ATTN_RADIUS)):
        valid = band & (kj >= lo) & (kj < hi)
        kinds.append(np.where(valid[None], alibi, np.float32(NEG_INF)))
    return jnp.asarray(np.stack(kinds, axis=1), dtype=F32)


def _attention_group(qkv, group):
    batch, dilation, n, _ = qkv.shape
    assert n % ATTN_TQ == 0 and n >= 2 * ATTN_TQ

    def part(p):
        return pl.BlockSpec((None, None, n, GROUP_WIDTH), lambda b, r: (b, r, 0, p))

    out_blk = pl.BlockSpec((None, None, n, GROUP_WIDTH), lambda b, r: (b, r, 0, 0))
    return pl.pallas_call(
        functools.partial(_attn_kernel, n=n),
        grid=(batch, dilation),
        in_specs=[part(0), part(1), part(2), _const_spec((HEADS_PER_GROUP, 3, ATTN_TQ, ATTN_TK))],
        out_specs=[out_blk, out_blk],
        out_shape=[jax.ShapeDtypeStruct((batch, dilation, n, GROUP_WIDTH), BF16),
                   jax.ShapeDtypeStruct((batch, dilation, n, GROUP_WIDTH), F32)],
        scratch_shapes=[pltpu.VMEM((n + 2 * ATTN_RADIUS, GROUP_WIDTH), BF16)] * 2,
        compiler_params=_cparams(("parallel", "parallel")),
        name=f"attn_g{group}",
    )(qkv, qkv, qkv, _attn_bias(group, dilation))


def _filter_kernel(bands_ref, deltas_ref, w1_ref, b1_ref, w2_ref, b2_ref, w3_ref, b3_ref, w4_ref, freq_ref,
                   ft_ref, *, seq, tl):
    i = pl.program_id(0)
    freq = freq_ref[...]
    lane = lax.broadcasted_iota(jnp.int32, (1, LANES), 1)
    row = lax.broadcasted_iota(jnp.int32, (tl, 1), 0) + i * tl

    def half_filter(pos, half):
        posf = pos.astype(F32)
        t = posf / float(seq - 1)
        z = bands_ref[...] * (2.0 * math.pi * posf / float(seq))
        feat = jnp.where(lane == 0, t,
                         jnp.where(lane <= HYENA_BANDS, jnp.cos(z),
                                   jnp.where(lane <= 2 * HYENA_BANDS, -jnp.sin(z), 0.0)))
        h = jnp.sin(freq * (_mm(feat, w1_ref[...]) + b1_ref[...]))
        h = jnp.sin(freq * (_mm(h, w2_ref[...]) + b2_ref[...]))
        h = jnp.sin(freq * (_mm(h, w3_ref[...]) + b3_ref[...]))
        h = _mm(h, w4_ref[:, half * HYENA_WIDTH:(half + 1) * HYENA_WIDTH])
        return h * jnp.exp(-t * jnp.abs(deltas_ref[...]))

    hf = half_filter(row, 0)
    hb = half_filter(jnp.where(row == 0, 0, seq - row), 1)
    hb = jnp.where(row == 0, 0.0, hb)

    @pl.when(i == 0)
    def _():
        ft_ref[:, 0:CONV_P] = jnp.zeros((HYENA_WIDTH, CONV_P), F32)

    c0 = pl.multiple_of(CONV_P + i * tl, LANES)
    ft_ref[:, pl.ds(c0, tl)] = hb.T
    c1 = pl.multiple_of(CONV_P + seq + i * tl, LANES)
    ft_ref[:, pl.ds(c1, tl)] = hf.T

    @pl.when(i == pl.num_programs(0) - 1)
    def _():
        full = ft_ref[...]
        norm = jnp.sum(jnp.abs(full), axis=1, keepdims=True)
        ft_ref[...] = full / norm


def _hyena_filter(seq, w1, b1, w2, b2, w3, b3, w4, freq):
    tl = 512
    f = jnp.linspace(1e-4, HYENA_BANDS - 1, HYENA_BANDS, dtype=F32)
    bands = jnp.concatenate([jnp.zeros((1,), F32), f, f, jnp.zeros((LANES - HYENA_EMB,), F32)]).reshape(1, LANES)
    max_decay = math.log(HYENA_TARGET) / HYENA_FAST_DECAY
    min_decay = math.log(HYENA_TARGET) / HYENA_SLOW_DECAY
    deltas = jnp.linspace(min_decay, max_decay, HYENA_WIDTH, dtype=F32).reshape(1, HYENA_WIDTH)
    w1p = jnp.pad(w1, ((0, LANES - HYENA_EMB), (0, 0)))
    row = lambda a: a.reshape(1, -1)
    args = (bands, deltas, w1p, row(b1), w2, row(b2), w3, row(b3), w4, row(freq))
    return pl.pallas_call(
        functools.partial(_filter_kernel, seq=seq, tl=tl),
        grid=(seq // tl,),
        in_specs=[_const_spec(a.shape) for a in args],
        out_specs=pl.BlockSpec((HYENA_WIDTH, CONV_P + 2 * seq), lambda i: (0, 0)),
        out_shape=jax.ShapeDtypeStruct((HYENA_WIDTH, CONV_P + 2 * seq), F32),
        compiler_params=_cparams(("arbitrary",)),
        name="hyena_filter",
    )(*args)


def _shift_rows(x, prev_row, next_row):
    n = x.shape[0]
    row = lax.broadcasted_iota(jnp.int32, (n, 1), 0)
    xm = jnp.where(row == 0, prev_row, pltpu.roll(x, 1, axis=0))
    xp = jnp.where(row == n - 1, next_row, pltpu.roll(x, n - 1, axis=0))
    return xm, xp


def _hyena_pre_kernel(x_ref, prev_ref, next_ref, w_ref, b_ref, zt_ref, x0t_ref):
    i = pl.program_id(1)
    x = x_ref[0]
    prev_row = jnp.where(i == 0, 0.0, prev_ref[0, SUBLANES - 1:SUBLANES, :])
    next_row = jnp.where(i == pl.num_programs(1) - 1, 0.0, next_ref[0, 0:1, :])
    xm, xp = _shift_rows(x, prev_row, next_row)
    u = xm * w_ref[0:1, :] + x * w_ref[1:2, :] + xp * w_ref[2:3, :] + b_ref[...]
    c = HYENA_WIDTH
    x0, x1, v = u[:, :c], u[:, c:2 * c], u[:, 2 * c:]
    zt_ref[0] = (v * x1).T
    x0t_ref[0] = x0.T


def _hyena_pre(hy3, short_w, short_b, tl):
    batch, seq, width = hy3.shape
    nt = seq // tl
    hb = tl // SUBLANES
    last = seq // SUBLANES - 1
    out_blk = pl.BlockSpec((1, HYENA_WIDTH, tl), lambda b, i: (b, 0, i))
    return pl.pallas_call(
        _hyena_pre_kernel,
        grid=(batch, nt),
        in_specs=[pl.BlockSpec((1, tl, width), lambda b, i: (b, i, 0)),
                  pl.BlockSpec((1, SUBLANES, width), lambda b, i: (b, jnp.maximum(i * hb - 1, 0), 0)),
                  pl.BlockSpec((1, SUBLANES, width), lambda b, i: (b, jnp.minimum((i + 1) * hb, last), 0)),
                  _const_spec((3, width)), _const_spec((1, width))],
        out_specs=[out_blk, out_blk],
        out_shape=[jax.ShapeDtypeStruct((batch, HYENA_WIDTH, seq), F32)] * 2,
        compiler_params=_cparams(("parallel", "parallel")),
        name="hyena_pre",
    )(hy3, hy3, hy3, short_w, short_b.reshape(1, width))


def _hyena_conv_kernel(bias_ref, zt_ref, x0t_ref, ft_ref, o_ref, troll, zpad, *, seq, cb):
    batch = zt_ref.shape[0]
    nb = seq // CONV_P
    p = CONV_P
    g = pl.program_id(0)
    zero_margin = jnp.zeros((batch, nb, p), F32)
    zpad[:, 0:nb, :] = zero_margin
    zpad[:, 2 * nb:3 * nb, :] = zero_margin
    chunk = 1024

    def channel(ci, carry):
        for w in range(2 * seq // chunk):
            a = w * chunk
            src = ft_ref[pl.ds(ci, 1), a:a + chunk + p]
            rolled = pltpu.roll(jnp.broadcast_to(src, (p, chunk + p)), 0, axis=1, stride=1, stride_axis=0)
            troll[:, a:a + chunk] = rolled[:, p:].astype(BF16)
        z = zt_ref[:, ci, :, :]
        zpad[:, nb:2 * nb, :] = z
        acc = jnp.zeros((batch * nb, p), F32)
        for d in range(-(nb - 1), nb):
            zs = zpad[:, nb - d:2 * nb - d, :].reshape(batch * nb, p).astype(BF16)
            t = troll[:, seq + d * p:seq + (d + 1) * p]
            acc = acc + jnp.dot(zs, t, preferred_element_type=F32)
        bias = bias_ref[g * cb + ci]
        y = (acc.reshape(batch, nb, p) + z * bias) * x0t_ref[:, ci, :, :]
        o_ref[:, ci, :, :] = y
        return carry

    lax.fori_loop(0, cb, channel, 0)


def _hyena_conv(zt, x0t, ft, filt_bias, cb):
    batch, c, seq = zt.shape
    nb = seq // CONV_P
    z4 = zt.reshape(batch, c, nb, CONV_P)
    x4 = x0t.reshape(batch, c, nb, CONV_P)
    blk = pl.BlockSpec((batch, cb, nb, CONV_P), lambda g: (0, g, 0, 0))
    out = pl.pallas_call(
        functools.partial(_hyena_conv_kernel, seq=seq, cb=cb),
        grid=(c // cb,),
        in_specs=[pl.BlockSpec(memory_space=pltpu.SMEM), blk, blk,
                  pl.BlockSpec((cb, CONV_P + 2 * seq), lambda g: (g, 0))],
        out_specs=blk,
        out_shape=jax.ShapeDtypeStruct((batch, c, nb, CONV_P), F32),
        scratch_shapes=[pltpu.VMEM((CONV_P, 2 * seq), BF16),
                        pltpu.VMEM((batch, 3 * nb, CONV_P), F32)],
        compiler_params=_cparams(("parallel",)),
        name="hyena_conv",
    )(filt_bias, z4, x4, ft)
    return out.reshape(batch, c, seq)


def _interleave_classes(blk_ref, scr_ref):
    dil, per, width = blk_ref.shape
    slabs = width // LANES
    for r in range(dil):
        rows = blk_ref[r].astype(F32)
        for cb in range(slabs):
            scr_ref[cb, pl.ds(r, per, stride=dil), :] = rows[:, cb * LANES:(cb + 1) * LANES]
    return jnp.concatenate([scr_ref[cb] for cb in range(slabs)], axis=1)


def _outproj_kernel(o0, o1, o2, l0, l1, l2, hyt_ref, xa_ref, xb_ref, mod_ref, wa_ref, wh_ref, out_ref,
                    so1, so2, sl1, sl2, *, n_first):
    ls = [l0[0], _interleave_classes(l1, sl1), _interleave_classes(l2, sl2)]
    os_ = [o0[0].astype(F32), _interleave_classes(o1, so1), _interleave_classes(o2, so2)]
    x = jnp.where(pl.program_id(0) < n_first, xa_ref[...], xb_ref[...])
    m = jnp.maximum(jnp.maximum(ls[0], ls[1]), ls[2])
    es = [jnp.exp(l - m) for l in ls]
    den = es[0] + es[1] + es[2]
    attn = (es[0] * os_[0] + es[1] * os_[1] + es[2] * os_[2]) / den
    hy = hyt_ref[0].T
    mix = jnp.dot(attn.astype(BF16), wa_ref[...], preferred_element_type=F32)
    mix = mix + jnp.dot(hy.astype(BF16), wh_ref[...], preferred_element_type=F32)
    out_ref[...] = x + mod_ref[0, 2:3, :] * mix


def _out_proj(os_, ls, hyt, xa, xb, mods, w_out_bf, seq, tm):
    d = xa.shape[1]
    rows = xa.shape[0] + xb.shape[0]
    tpb = seq // tm
    wa, wh = w_out_bf[:GROUP_WIDTH], w_out_bf[GROUP_WIDTH:]

    def class_spec(a):
        dil = a.shape[1]
        return pl.BlockSpec((None, dil, tm // dil, GROUP_WIDTH), lambda i: (i // tpb, 0, i % tpb, 0))

    return pl.pallas_call(
        functools.partial(_outproj_kernel, n_first=xa.shape[0] // tm),
        grid=(rows // tm,),
        in_specs=[class_spec(a) for a in (*os_, *ls)] + [
            pl.BlockSpec((1, HYENA_WIDTH, tm), lambda i: (i // tpb, 0, i % tpb))]
        + _two_source_specs(tm, d, xa.shape[0] // tm) + [
            pl.BlockSpec((1, 6, d), lambda i: (i // tpb, 0, 0)),
            _const_spec(wa.shape), _const_spec(wh.shape)],
        out_specs=pl.BlockSpec((tm, d), lambda i: (i, 0)),
        out_shape=jax.ShapeDtypeStruct((rows, d), F32),
        scratch_shapes=[pltpu.VMEM((GROUP_WIDTH // LANES, tm, LANES), F32)] * 4,
        compiler_params=_cparams(("parallel",)),
        name="out_proj",
    )(*os_, *ls, hyt, xa, xb, mods, wa, wh)


def _halo_rows(x_ref, prev_ref, next_ref, tiles_per_seq):
    i = pl.program_id(0)
    first = (i % tiles_per_seq) == 0
    last = (i % tiles_per_seq) == tiles_per_seq - 1
    xe = jnp.concatenate([prev_ref[...], x_ref[...], next_ref[...]], axis=0)
    return xe, first, last


def _ffn_kernel(x_ref, prev_ref, next_ref, mod_ref, gain_ref, wa_ref, wg_ref, cw_ref, cb_ref, wd_ref, fin_ref,
                *outs_and_scratch, tiles_per_seq, fc, final_norm, n_first):
    *out_refs, act_ref, a0_ref, g0_ref, a1_ref, g1_ref = outs_and_scratch
    tm = x_ref.shape[0]
    x = x_ref[...]
    xe, first, last = _halo_rows(x_ref, prev_ref, next_ref, tiles_per_seq)
    he32 = _norm_mod(xe, gain_ref[...], mod_ref[0, 3:4, :], mod_ref[0, 4:5, :])
    he = he32.astype(BF16)
    h = he32[SUBLANES:SUBLANES + tm].astype(BF16)
    row = lax.broadcasted_iota(jnp.int32, (tm + 2 * SUBLANES, 1), 0)
    lo = jnp.where(first, SUBLANES, 0)
    hi = jnp.where(last, tm + SUBLANES, tm + 2 * SUBLANES)
    keep = jnp.logical_and(row >= lo, row < hi)
    nchunk = wa_ref.shape[1] // fc
    n_ext = tm + 2 * SUBLANES

    def up(j, bufs):
        c0 = pl.multiple_of(j * fc, fc)
        bufs[0][...] = jnp.dot(he, wa_ref[:, pl.ds(c0, fc)], preferred_element_type=F32)
        bufs[1][...] = jnp.dot(h, wg_ref[:, pl.ds(c0, fc)], preferred_element_type=F32)

    def activate(j, bufs):
        c0 = pl.multiple_of(j * fc, fc)
        a = jnp.where(keep, bufs[0][...], 0.0)
        am = pltpu.roll(a, 1, axis=0)[SUBLANES:SUBLANES + tm]
        ap = pltpu.roll(a, n_ext - 1, axis=0)[SUBLANES:SUBLANES + tm]
        ac = a[SUBLANES:SUBLANES + tm]
        cw = cw_ref[:, pl.ds(c0, fc)]
        conv = am * cw[0:1] + ac * cw[1:2] + ap * cw[2:3] + cb_ref[:, pl.ds(c0, fc)]
        act_ref[:, pl.ds(c0, fc)] = (jax.nn.gelu(conv) * bufs[1][...]).astype(BF16)

    even, odd = (a0_ref, g0_ref), (a1_ref, g1_ref)
    up(0, even)

    def body(j, carry):
        up(2 * j + 1, odd)
        activate(2 * j, even)
        up(2 * j + 2, even)
        activate(2 * j + 1, odd)
        return carry

    assert nchunk % 2 == 1
    lax.fori_loop(0, nchunk // 2, body, 0)
    activate(nchunk - 1, even)
    down = jnp.dot(act_ref[...], wd_ref[...], preferred_element_type=F32)
    y = x + mod_ref[0, 5:6, :] * down
    if final_norm:
        ms = jnp.mean(y * y, axis=-1, keepdims=True)
        y = y * lax.rsqrt(ms + RMS_EPS) * fin_ref[...]
    if n_first is None:
        out_refs[0][...] = y
    else:
        i = pl.program_id(0)

        @pl.when(i < n_first)
        def _():
            out_refs[0][...] = y

        @pl.when(i >= n_first)
        def _():
            out_refs[1][...] = y


def _halo_specs(tm, d, nrows):
    hb = tm // SUBLANES
    last = nrows // SUBLANES - 1
    return [pl.BlockSpec((tm, d), lambda i: (i, 0)),
            pl.BlockSpec((SUBLANES, d), lambda i: (jnp.maximum(i * hb - 1, 0), 0)),
            pl.BlockSpec((SUBLANES, d), lambda i: (jnp.minimum((i + 1) * hb, last), 0))]


def _conv_ffn(x2, mods, gain, up_bf, conv_w, conv_b, down_bf, fin_gain, seq, tm, final_norm, split_rows=None):
    rows, d = x2.shape
    dff = down_bf.shape[0]
    fc = 256
    tpb = seq // tm
    wa, wg = up_bf[:, :dff], up_bf[:, dff:]
    if split_rows is None:
        n_first = None
        out_specs = pl.BlockSpec((tm, d), lambda i: (i, 0))
        out_shape = jax.ShapeDtypeStruct((rows, d), F32)
    else:
        n_first = split_rows // tm
        out_specs = _two_source_specs(tm, d, n_first)
        out_shape = [jax.ShapeDtypeStruct((split_rows, d), F32), jax.ShapeDtypeStruct((rows - split_rows, d), F32)]
    return pl.pallas_call(
        functools.partial(_ffn_kernel, tiles_per_seq=tpb, fc=fc, final_norm=final_norm, n_first=n_first),
        grid=(rows // tm,),
        in_specs=_halo_specs(tm, d, rows) + [
            pl.BlockSpec((1, 6, d), lambda i: (i // tpb, 0, 0)),
            _const_spec((1, d)), _const_spec(wa.shape), _const_spec(wg.shape),
            _const_spec((3, dff)), _const_spec((1, dff)), _const_spec(down_bf.shape), _const_spec((1, d))],
        out_specs=out_specs,
        out_shape=out_shape,
        scratch_shapes=[pltpu.VMEM((tm, dff), BF16)]
        + [pltpu.VMEM((tm + 2 * SUBLANES, fc), F32), pltpu.VMEM((tm, fc), F32)] * 2,
        compiler_params=_cparams(("arbitrary",)),
        name="conv_ffn",
    )(x2, x2, x2, mods, gain.reshape(1, d), wa, wg, conv_w, conv_b.reshape(1, dff), down_bf,
      fin_gain.reshape(1, d))


def _head_sum(x, ones_bd):
    xb = x.astype(BF16)
    parts = [jnp.dot(xb[:, c * MXU_DIM:(c + 1) * MXU_DIM], ones_bd, preferred_element_type=F32)
             for c in range(x.shape[1] // MXU_DIM)]
    return jnp.concatenate(parts, axis=1)


def _sigmoid(x):
    return 0.5 * jnp.tanh(0.5 * x) + 0.5


def _ones_blockdiag():
    r = lax.broadcasted_iota(jnp.int32, (MXU_DIM, MXU_DIM), 0) // RWKV_HEAD
    c = lax.broadcasted_iota(jnp.int32, (MXU_DIM, MXU_DIM), 1) // RWKV_HEAD
    return jnp.where(r == c, 1.0, 0.0).astype(BF16)


def _rwkv_pre_kernel(x_ref, prev_ref, next_ref, mod_ref, gain_ref, mu_ref, wr_ref, wk_ref, wv_ref,
                     w1_ref, w2_ref, w0_ref, a1_ref, a2_ref, a0_ref, g1_ref, g2_ref, kk_w_ref, ka_ref, rk_ref,
                     r_out, v_out, kk_out, k_out, a0_out, a1_out, lw0_out, lw1_out, g_out, bonus_out, he_ref, hb_ref, xx_ref,
                     *, tiles_per_seq):
    tm, d = x_ref.shape
    i = pl.program_id(0)
    first = (i % tiles_per_seq) == 0
    last = (i % tiles_per_seq) == tiles_per_seq - 1
    norm = lambda x: _norm_mod(x, gain_ref[...], mod_ref[0, 0:1, :], mod_ref[0, 1:2, :])
    ones_bd = _ones_blockdiag()
    he_ref[0:SUBLANES, :] = jnp.where(first, 0.0, norm(prev_ref[...]))
    he_ref[SUBLANES + tm:, :] = jnp.where(last, 0.0, norm(next_ref[...]))

    def row_part(lo, n):
        rows = slice(lo, lo + n)
        mid = slice(SUBLANES + lo, SUBLANES + lo + n)
        h = norm(x_ref[rows, :])
        he_ref[mid, :] = h
        hb_ref[rows, :] = h.astype(BF16)
        yield
        hm = he_ref[SUBLANES + lo - 1:SUBLANES + lo - 1 + n, :]
        hp = he_ref[SUBLANES + lo + 1:SUBLANES + lo + 1 + n, :]
        xx_ref[rows, :] = (0.5 * (hm + hp) - he_ref[mid, :]).astype(BF16)
        bf16_rows = 2 * SUBLANES

        def mixed(i):
            mu_tile = jnp.broadcast_to(mu_ref[i:i + 1, :], (bf16_rows, d)).astype(BF16)
            return hb_ref[rows, :] + xx_ref[rows, :] * jnp.tile(mu_tile, (n // bf16_rows, 1))

        r = jnp.dot(mixed(0), wr_ref[...], preferred_element_type=F32)
        r_out[rows, :] = r.astype(BF16)
        yield
        k = jnp.dot(mixed(2), wk_ref[...], preferred_element_type=F32)
        k_out[rows, :] = k.astype(BF16)
        kk = k * kk_w_ref[...]
        kk_out[rows, :] = (kk * jnp.minimum(lax.rsqrt(_head_sum(kk * kk, ones_bd)), 1e12)).astype(BF16)
        yield
        v = jnp.dot(mixed(3), wv_ref[...], preferred_element_type=F32)
        v_out[rows, :] = v.astype(BF16)
        yield
        gl = _sigmoid(jnp.dot(mixed(5), g1_ref[...], preferred_element_type=F32))
        g_out[rows, :] = jnp.dot(gl.astype(BF16), g2_ref[...], preferred_element_type=F32)
        yield
        wl = jnp.tanh(jnp.dot(mixed(1), w1_ref[...], preferred_element_type=F32))
        wl = jnp.dot(wl.astype(BF16), w2_ref[...], preferred_element_type=F32)
        for direction, lw_o in enumerate((lw0_out, lw1_out)):
            u = w0_ref[direction:direction + 1, :] + wl[:, direction * d:(direction + 1) * d]
            lw_o[rows, :] = -_sigmoid(u) * math.exp(-0.5)
        yield
        al = jnp.dot(mixed(4), a1_ref[...], preferred_element_type=F32)
        al = jnp.dot(al.astype(BF16), a2_ref[...], preferred_element_type=F32)
        ksum = jnp.zeros_like(k)
        for direction, a_o in enumerate((a0_out, a1_out)):
            a = _sigmoid(a0_ref[direction:direction + 1, :] + al[:, direction * d:(direction + 1) * d])
            a_o[rows, :] = a.astype(BF16)
            ksum = ksum + _k_dir(k, a, ka_ref[...])
        bonus_out[rows, :] = _head_sum(r * (0.5 * ksum) * rk_ref[...], ones_bd) * v

    parts = 2
    _emit_interleaved(*[row_part(p * (tm // parts), tm // parts) for p in range(parts)])


def _k_dir(k, a, k_a):
    return k * (1.0 + (a - 1.0) * k_a)


def _blockdiag2(m):
    z = jnp.zeros_like(m[0])
    return jnp.concatenate([jnp.concatenate([m[0], z], axis=1), jnp.concatenate([z, m[1]], axis=1)], axis=0)


def _rwkv_pre(x2, mods, p, seq, tm):
    rows, d = x2.shape
    tpb = seq // tm
    bf = lambda a: a.astype(BF16)
    w1 = bf(jnp.concatenate([p['w1'][0], p['w1'][1]], axis=1))
    w2 = bf(_blockdiag2(p['w2']))
    a1 = bf(jnp.concatenate([p['a1'][0], p['a1'][1]], axis=1))
    a2 = bf(_blockdiag2(p['a2']))
    glora = p['g1'].shape[1]
    gpad = -glora % LANES
    g1 = bf(jnp.pad(p['g1'], ((0, 0), (0, gpad))))
    g2 = bf(jnp.pad(p['g2'], ((0, gpad), (0, 0))))
    row = lambda a: a.reshape(1, d)
    args = (x2, x2, x2, mods, row(p['norm1']), p['mu'], bf(p['w_r']), bf(p['w_k']), bf(p['w_v']),
            w1, w2, p['w0'], a1, a2, p['a0'], g1, g2, row(p['k_k']), row(p['k_a']), row(p['r_k']))
    in_specs = _halo_specs(tm, d, rows) + [pl.BlockSpec((1, 6, d), lambda i: (i // tpb, 0, 0))]
    in_specs += [_const_spec(a.shape) for a in args[4:]]
    ob = pl.BlockSpec((tm, d), lambda i: (i, 0))
    return pl.pallas_call(
        functools.partial(_rwkv_pre_kernel, tiles_per_seq=tpb),
        grid=(rows // tm,),
        in_specs=in_specs,
        out_specs=[ob] * 10,
        out_shape=[jax.ShapeDtypeStruct((rows, d), BF16)] * 6 + [jax.ShapeDtypeStruct((rows, d), F32)] * 4,
        scratch_shapes=[pltpu.VMEM((tm + 2 * SUBLANES, d), F32), pltpu.VMEM((tm, d), BF16),
                        pltpu.VMEM((tm, d), BF16)],
        compiler_params=_cparams(("parallel",)),
        name="rwkv_pre",
    )(*args)


def _scan_masks(reverse):
    c = SCAN_C
    t = lax.broadcasted_iota(jnp.int32, (c, 4 * c), 0)
    s = lax.broadcasted_iota(jnp.int32, (c, 4 * c), 1) % c
    before = (s > t) if reverse else (s < t)
    upto = jnp.logical_or(before, s == t)
    eye = jnp.where(s == t, 1.0, 0.0)
    rr = lax.broadcasted_iota(jnp.int32, (4 * c, SCAN_HG), 0)
    cc = lax.broadcasted_iota(jnp.int32, (4 * c, SCAN_HG), 1)
    same_head = jnp.where((rr // c) == (cc // RWKV_HEAD), 1.0, 0.0)
    eye_big = jnp.where(rr == cc, 1.0, 0.0)
    return before, upto, eye, same_head, same_head.astype(BF16), eye_big


def _bd(xb, head_mask_bf):
    return jnp.concatenate([xb] * 4, axis=0) * head_mask_bf


def _bd_t(x, head_mask_bf):
    xt = jnp.concatenate([x, x], axis=0).T.astype(BF16)
    return jnp.concatenate([xt, xt], axis=1) * head_mask_bf


def _mm(a, b):
    return jnp.dot(a, b, precision=HIGHEST, preferred_element_type=F32)


def _bdot(a, b):
    return jnp.dot(a, b, preferred_element_type=F32)


def _bdot_nt(a, b):
    return lax.dot_general(a, b, (((1,), (1,)), ((), ())), preferred_element_type=F32)


def _cumsum_rows(tri_bf, x):
    hi = x.astype(BF16)
    rest = x - hi.astype(F32)
    mid = rest.astype(BF16)
    lo = (rest - mid.astype(F32)).astype(BF16)
    return _bdot(tri_bf, hi) + _bdot(tri_bf, mid) + _bdot(tri_bf, lo)


def _scan_prepare_stages(chains, out):
    c = SCAN_C
    bf = lambda x: x.astype(BF16)
    idx = range(len(chains))
    r, v, kk, lw, k, a = [[ch[0][j] for ch in chains] for j in range(6)]
    tri, before, upto, eye = [[ch[1][j] for ch in chains] for j in range(4)]
    head_mask_bf = chains[0][1][5]
    rev = [ch[2] for ch in chains]
    bd = lambda x: _bd(bf(x), head_mask_bf)

    g = [_cumsum_rows(tri[i], lw[i]) for i in idx]
    gtot = [g[i][0:1] if rev[i] else g[i][c - 1:c] for i in idx]
    kkd = [kk[i] * jnp.exp(g[i] - lw[i]) for i in idx]
    rd = [r[i] * jnp.exp(g[i]) for i in idx]
    b = [kk[i] * a[i] for i in idx]
    ginv = [jnp.exp(-g[i]) for i in idx]
    lhs = [bf(jnp.concatenate([kkd[i], rd[i]], axis=0)) for i in idx]
    yield
    a1 = [_bdot(lhs[i], _bd_t(b[i] * ginv[i], head_mask_bf)) for i in idx]
    a2 = [_bdot(lhs[i], _bd_t(k[i] * ginv[i], head_mask_bf)) for i in idx]
    n_ab = [jnp.where(before[i], a1[i][:c], 0.0) for i in idx]
    a_rb = [jnp.where(upto[i], a1[i][c:], 0.0) for i in idx]
    a_k = [jnp.concatenate([jnp.where(before[i], a2[i][:c], 0.0), jnp.where(upto[i], a2[i][c:], 0.0)], axis=0)
           for i in idx]
    yield

    nb = [bf(n_ab[i]) for i in idx]
    npow = [_bdot(nb[i], _bd(nb[i], head_mask_bf)) for i in idx]
    tinv = [eye[i] - n_ab[i] for i in idx]
    yield
    steps = int(math.log2(c)) - 1
    for step in range(steps):
        pb = [bf(npow[i]) for i in idx]
        rhs = [_bd(pb[i], head_mask_bf) for i in idx]
        if step == steps - 1:
            tinv = [tinv[i] + _bdot(bf(tinv[i]), rhs[i]) for i in idx]
        else:
            prod = [_bdot(jnp.concatenate([pb[i], bf(tinv[i])], axis=0), rhs[i]) for i in idx]
            tinv = [tinv[i] + prod[i][c:] for i in idx]
            npow = [prod[i][:c] for i in idx]
        yield

    vb = [bf(v[i]) for i in idx]
    av = [_bdot(bf(a_k[i]), _bd(vb[i], head_mask_bf)) for i in idx]
    yield
    tw = [_bdot(bf(tinv[i]), jnp.concatenate([bd(kkd[i]), bd(av[i][:c])], axis=1)) for i in idx]
    kkt = [tw[i][:, :SCAN_HG] for i in idx]
    wt = [tw[i][:, SCAN_HG:] for i in idx]
    yield
    aw = [_bdot(bf(a_rb[i]), jnp.concatenate([bd(kkt[i]), bd(wt[i])], axis=1)) for i in idx]
    lhs_state = [bf(jnp.concatenate([rd[i] - aw[i][:, :SCAN_HG], kkt[i]], axis=0)) for i in idx]
    ypre = [av[i][c:] - aw[i][:, SCAN_HG:] for i in idx]
    yield
    gend =[jnp.exp(gtot[i] - g[i]) for i in idx]
    upd_t = [bf(jnp.concatenate([b[i] * gend[i], k[i] * gend[i]], axis=0).T) for i in idx]
    gcol_src = [jnp.concatenate([g[i], g[i]], axis=0).T for i in idx]
    gcol = [jnp.exp(gcol_src[i][:, 0:1] if rev[i] else gcol_src[i][:, c - 1:c]) for i in idx]
    out.extend((lhs_state[i], wt[i], ypre[i], upd_t[i], gcol[i], vb[i]) for i in idx)


def _scan_apply_stages(preps, states, head_mask, emit_y):
    c = SCAN_C
    for pos in range(0, len(preps), 2):
        pair = preps[pos:pos + 2]
        o = [_bdot(pair[i][0], states[i].astype(BF16)) for i in range(2)]
        yield
        for i in range(2):
            emit_y(pos + i, o[i][:c] + pair[i][2])
        u = [-(pair[i][1] + o[i][c:]) for i in range(2)]
        upd = [_bdot(pair[i][3], jnp.concatenate([u[i].astype(BF16), pair[i][5]], axis=0)) for i in range(2)]
        yield
        for i in range(2):
            states[i] = states[i] * pair[i][4] + head_mask * upd[i]


def _emit_interleaved(*stage_generators):
    pending = list(stage_generators)
    while pending:
        for gen in list(pending):
            try:
                next(gen)
            except StopIteration:
                pending.remove(gen)


def _scan_consts(reverse):
    c = SCAN_C
    before, upto, eye, head_mask, head_mask_bf, _ = _scan_masks(reverse)
    tt = lax.broadcasted_iota(jnp.int32, (c, c), 0)
    ss = lax.broadcasted_iota(jnp.int32, (c, c), 1)
    tri = jnp.where((ss >= tt) if reverse else (ss <= tt), 1.0, 0.0).astype(BF16)
    return tri, before, upto, eye, head_mask, head_mask_bf


def _scan_kernel(ka_ref, rf_ref, vf_ref, kkf_ref, kf_ref, af_ref, lwf_ref, rb_ref, vb_ref, kkb_ref, kb_ref, ab_ref,
                 lwb_ref, yf_ref, yb_ref, sf_ref, sb_ref):
    tb = rf_ref.shape[0]
    nchunk = tb // SCAN_C

    @pl.when(pl.program_id(2) == 0)
    def _():
        sf_ref[...] = jnp.zeros_like(sf_ref)
        sb_ref[...] = jnp.zeros_like(sb_ref)

    consts_f = _scan_consts(False)
    consts_b = _scan_consts(True)
    head_mask = consts_f[4]
    k_a = ka_ref[...]

    def operands(refs, rows):
        r, v, kk, k, a = (ref[rows, :].astype(F32) for ref in refs[:5])
        return r, v, kk, refs[5][rows, :], _k_dir(k, a, k_a), a

    fwd_refs = (rf_ref, vf_ref, kkf_ref, kf_ref, af_ref, lwf_ref)
    bwd_refs = (rb_ref, vb_ref, kkb_ref, kb_ref, ab_ref, lwb_ref)

    def emitter(rows):
        def emit_y(pos, y):
            ref = yb_ref if pos % 2 else yf_ref
            ref[rows[pos], :] = y.astype(ref.dtype)
        return emit_y

    states = [sf_ref[...], sb_ref[...]]
    pending = None
    for group in range(nchunk // SCAN_UNROLL):
        rows, chains = [], []
        for u in range(SCAN_UNROLL):
            j = group * SCAN_UNROLL + u
            rows_f = pl.ds(j * SCAN_C, SCAN_C)
            rows_b = pl.ds((nchunk - 1 - j) * SCAN_C, SCAN_C)
            rows += [rows_f, rows_b]
            chains.append((operands(fwd_refs, rows_f), consts_f, False))
            chains.append((operands(bwd_refs, rows_b), consts_b, True))
        preps = []
        stages = [_scan_prepare_stages(chains, preps)]
        if pending is not None:
            stages.append(_scan_apply_stages(pending[0], states, head_mask, emitter(pending[1])))
        _emit_interleaved(*stages)
        pending = (preps, rows)
    _emit_interleaved(_scan_apply_stages(pending[0], states, head_mask, emitter(pending[1])))
    sf_ref[...] = states[0]
    sb_ref[...] = states[1]


def _wkv_scan(r, v, kk, k, a0, a1, lw0, lw1, k_a, batch, seq, tb):
    rows, d = r.shape
    nt = seq // tb
    ng = d // SCAN_HG
    fwd = pl.BlockSpec((tb, SCAN_HG), lambda b, g, t: (b * nt + t, g))
    bwd = pl.BlockSpec((tb, SCAN_HG), lambda b, g, t: (b * nt + nt - 1 - t, g))
    return pl.pallas_call(
        _scan_kernel,
        grid=(batch, ng, nt),
        in_specs=[pl.BlockSpec((1, SCAN_HG), lambda b, g, t: (0, g))] + [fwd] * 6 + [bwd] * 6,
        out_specs=[fwd, bwd],
        out_shape=[jax.ShapeDtypeStruct((rows, d), BF16)] * 2,
        scratch_shapes=[pltpu.VMEM((SCAN_HG, SCAN_HG), F32)] * 2,
        compiler_params=_cparams(("parallel", "parallel", "arbitrary")),
        name="wkv_scan",
    )(k_a.reshape(1, d), r, v, kk, k, a0, lw0, r, v, kk, k, a1, lw1)


def _rwkv_post_kernel(yf_ref, yb_ref, bonus_ref, g_ref, x_ref, mod_ref, lnw_ref, lnb_ref, wo_ref, out_ref):
    y = yf_ref[...].astype(F32) + yb_ref[...].astype(F32)
    ones_bd = _ones_blockdiag()
    mu = _head_sum(y, ones_bd) * (1.0 / RWKV_HEAD)
    yc = y - mu
    var = _head_sum(yc * yc, ones_bd) * (1.0 / RWKV_HEAD)
    yn = yc * lax.rsqrt(var + GN_EPS) * lnw_ref[...] + lnb_ref[...]
    out = ((yn + bonus_ref[...]) * g_ref[...]).astype(BF16)
    mix = jnp.dot(out, wo_ref[...], preferred_element_type=F32)
    out_ref[...] = x_ref[...] + mod_ref[0, 2:3, :] * mix


def _rwkv_post(yf, yb, bonus, g, x2, mods, ln_w, ln_b, wo_bf, seq, tm):
    rows, d = x2.shape
    tpb = seq // tm
    rb = pl.BlockSpec((tm, d), lambda i: (i, 0))
    return pl.pallas_call(
        _rwkv_post_kernel,
        grid=(rows // tm,),
        in_specs=[rb] * 5 + [pl.BlockSpec((1, 6, d), lambda i: (i // tpb, 0, 0)),
                             _const_spec((1, d)), _const_spec((1, d)), _const_spec((d, d))],
        out_specs=rb,
        out_shape=jax.ShapeDtypeStruct((rows, d), F32),
        compiler_params=_cparams(("parallel",)),
        name="rwkv_post",
    )(yf, yb, bonus, g, x2, mods, ln_w.reshape(1, d), ln_b.reshape(1, d), wo_bf)


def _mods(c, w, b):
    batch, d = c.shape
    pad = -batch % SUBLANES
    m = _ada_params(jnp.pad(c, ((0, pad), (0, 0))), w, b)[:batch]
    return m.reshape(batch, 6, d)


def _group_major_w_in(w_in):
    d = w_in.shape[0]
    qkv = w_in[:, :3 * ATTN_WIDTH].reshape(d, 3, N_PATTERNS, GROUP_WIDTH)
    qkv = jnp.transpose(qkv, (0, 2, 1, 3)).reshape(d, 3 * ATTN_WIDTH)
    return jnp.concatenate([qkv, w_in[:, 3 * ATTN_WIDTH:]], axis=1)


def _trunk(xa, xb, c, l0, l1, final_norm):
    seq, d = xa.shape[1:]
    assert xb.shape[1:] == (seq, d)
    batch = xa.shape[0] + xb.shape[0]
    xa2, xb2 = xa.reshape(-1, d), xb.reshape(-1, d)
    bf = lambda a: a.astype(BF16)
    tm = 512

    mods0 = _mods(c, l0['ada_w'], l0['ada_b'])
    q0, q1, q2, hy = _in_proj(xa2, xb2, mods0, l0['norm1'], bf(_group_major_w_in(l0['w_in'])), seq, tm)
    os_, ls = [], []
    for group, qkv in enumerate((q0.reshape(batch, 1, seq, 3 * GROUP_WIDTH), q1, q2)):
        o, lse = _attention_group(qkv, group)
        os_.append(o)
        ls.append(lse)
    ft = _hyena_filter(seq, l0['filt_w1'], l0['filt_b1'], l0['filt_w2'], l0['filt_b2'], l0['filt_w3'],
                       l0['filt_b3'], l0['filt_w4'], l0['filt_freq'])
    zt, x0t = _hyena_pre(hy.reshape(batch, seq, 3 * HYENA_WIDTH), l0['short_w'], l0['short_b'], tl=512)
    hyt = _hyena_conv(zt, x0t, ft, l0['filt_bias'], cb=8)
    x2 = _out_proj(os_, ls, hyt, xa2, xb2, mods0, bf(l0['w_out']), seq, tm)
    x2 = _conv_ffn(x2, mods0, l0['norm2'], bf(l0['ffn_up']), l0['ffn_conv_w'], l0['ffn_conv_b'],
                   bf(l0['ffn_down']), final_norm, seq, tm, final_norm=False)

    mods1 = _mods(c, l1['ada_w'], l1['ada_b'])
    r, v, kk, k, a0, a1, lw0, lw1, g, bonus = _rwkv_pre(x2, mods1, l1, seq, tm=512)
    yf, yb = _wkv_scan(r, v, kk, k, a0, a1, lw0, lw1, l1['k_a'], batch, seq, tb=1024)
    x2 = _rwkv_post(yf, yb, bonus, g, x2, mods1, l1['ln_w'], l1['ln_b'], bf(l1['w_o']), seq, tm)
    ya, yb_ = _conv_ffn(x2, mods1, l1['norm2'], bf(l1['ffn_up']), l1['ffn_conv_w'], l1['ffn_conv_b'],
                        bf(l1['ffn_down']), final_norm, seq, tm, final_norm=True, split_rows=xa2.shape[0])
    return ya.reshape(xa.shape), yb_.reshape(xb.shape)


def kernel(x_prompt, x_sample, c_prompt, c_sample, l0_ada_w, l0_ada_b, l0_norm1, l0_norm2, l0_w_in, l0_short_w, l0_short_b, l0_filt_w1, l0_filt_b1, l0_filt_w2, l0_filt_b2, l0_filt_w3, l0_filt_b3, l0_filt_w4, l0_filt_freq, l0_filt_bias, l0_w_out, l0_ffn_up, l0_ffn_conv_w, l0_ffn_conv_b, l0_ffn_down, l1_ada_w, l1_ada_b, l1_norm1, l1_norm2, l1_mu, l1_w_r, l1_w_k, l1_w_v, l1_w_o, l1_w0, l1_w1, l1_w2, l1_a0, l1_a1, l1_a2, l1_g1, l1_g2, l1_k_k, l1_k_a, l1_r_k, l1_ln_w, l1_ln_b, l1_ffn_up, l1_ffn_conv_w, l1_ffn_conv_b, l1_ffn_down, final_norm):
    layer0 = dict(ada_w=l0_ada_w, ada_b=l0_ada_b, norm1=l0_norm1, norm2=l0_norm2, w_in=l0_w_in,
                  short_w=l0_short_w, short_b=l0_short_b, filt_w1=l0_filt_w1, filt_b1=l0_filt_b1,
                  filt_w2=l0_filt_w2, filt_b2=l0_filt_b2, filt_w3=l0_filt_w3, filt_b3=l0_filt_b3,
                  filt_w4=l0_filt_w4, filt_freq=l0_filt_freq, filt_bias=l0_filt_bias, w_out=l0_w_out,
                  ffn_up=l0_ffn_up, ffn_conv_w=l0_ffn_conv_w, ffn_conv_b=l0_ffn_conv_b, ffn_down=l0_ffn_down)
    layer1 = dict(ada_w=l1_ada_w, ada_b=l1_ada_b, norm1=l1_norm1, norm2=l1_norm2, mu=l1_mu,
                  w_r=l1_w_r, w_k=l1_w_k, w_v=l1_w_v, w_o=l1_w_o, w0=l1_w0, w1=l1_w1, w2=l1_w2,
                  a0=l1_a0, a1=l1_a1, a2=l1_a2, g1=l1_g1, g2=l1_g2, k_k=l1_k_k, k_a=l1_k_a,
                  r_k=l1_r_k.reshape(-1), ln_w=l1_ln_w, ln_b=l1_ln_b, ffn_up=l1_ffn_up,
                  ffn_conv_w=l1_ffn_conv_w, ffn_conv_b=l1_ffn_conv_b, ffn_down=l1_ffn_down)
    c = jnp.concatenate([c_prompt, c_sample], axis=0)
    return _trunk(x_prompt, x_sample, c, layer0, layer1, final_norm)
```

```python
import functools
import math

import jax
import jax.numpy as jnp
import numpy as np
from jax import lax
from jax.experimental import pallas as pl
from jax.experimental.pallas import tpu as pltpu

F32 = jnp.float32
BF16 = jnp.bfloat16
HIGHEST = lax.Precision.HIGHEST

HEAD_DIM = 64
ATTN_PATTERNS = ((128, 1), (512, 4), (2048, 16))
N_PATTERNS = 3
HEADS_PER_GROUP = 4
N_ATTN_HEADS = 12
ATTN_WIDTH = 768
GROUP_WIDTH = HEADS_PER_GROUP * HEAD_DIM
ALIBI_MAX_EXP = 8.0
HYENA_WIDTH = 256
HYENA_BANDS = 16
HYENA_EMB = 33
HYENA_FFN = 64
HYENA_TARGET = 1e-2
HYENA_FAST_DECAY = 0.3
HYENA_SLOW_DECAY = 1.5
RWKV_HEAD = 64
RMS_EPS = 1e-6
GN_EPS = 64e-5
NEG_INF = -1e30

LANES = 128
SUBLANES = 8
MXU_DIM = 256
VMEM_LIMIT = 56 * 1024 * 1024

ATTN_RADIUS = 64
ATTN_TQ = 128
ATTN_TK = ATTN_TQ + 2 * ATTN_RADIUS
CONV_P = 256
SCAN_C = 64
SCAN_HG = 256
SCAN_UNROLL = 4


def _cparams(sem):
    return pltpu.CompilerParams(dimension_semantics=sem, vmem_limit_bytes=VMEM_LIMIT)


def _const_spec(shape):
    nd = len(shape)
    return pl.BlockSpec(shape, lambda *_: (0,) * nd, pipeline_mode=pl.Buffered(1))


def _ada_kernel(c_ref, w_ref, b_ref, o_ref):
    c = c_ref[...]
    s = c * jax.nn.sigmoid(c)
    o_ref[...] = jnp.dot(s, w_ref[...], precision=HIGHEST, preferred_element_type=F32) + b_ref[...]


def _ada_params(c_pad, w, b):
    bp, d = c_pad.shape
    n = w.shape[1]
    tn = 1024
    return pl.pallas_call(
        _ada_kernel,
        grid=(n // tn,),
        in_specs=[_const_spec((bp, d)),
                  pl.BlockSpec((d, tn), lambda j: (0, j)),
                  pl.BlockSpec((1, tn), lambda j: (0, j))],
        out_specs=pl.BlockSpec((bp, tn), lambda j: (0, j)),
        out_shape=jax.ShapeDtypeStruct((bp, n), F32),
        compiler_params=_cparams(("parallel",)),
        name="ada_params",
    )(c_pad, w, b.reshape(1, n))


def _norm_mod(x, gain, shift, scale):
    ms = jnp.mean(x * x, axis=-1, keepdims=True)
    y = x * lax.rsqrt(ms + RMS_EPS) * gain
    return y * (1.0 + scale) + shift


def _two_source_specs(tm, d, n_first):
    return [pl.BlockSpec((tm, d), lambda i: (jnp.minimum(i, n_first - 1), 0)),
            pl.BlockSpec((tm, d), lambda i: (jnp.maximum(i - n_first, 0), 0))]


def _inproj_kernel(xa_ref, xb_ref, mod_ref, gain_ref, w_ref, q0_ref, q1_ref, q2_ref, hy_ref, scr_ref, *, n_first):
    tm = xa_ref.shape[0]
    x = jnp.where(pl.program_id(0) < n_first, xa_ref[...], xb_ref[...])
    h = _norm_mod(x, gain_ref[...], mod_ref[0, 0:1, :], mod_ref[0, 1:2, :])
    p = jnp.dot(h.astype(BF16), w_ref[...], preferred_element_type=F32)
    gw = 3 * GROUP_WIDTH
    q0_ref[...] = p[:, :gw].astype(BF16)
    for g, out_ref in ((1, q1_ref), (2, q2_ref)):
        dil = out_ref.shape[1]
        for cb in range(gw // LANES):
            cols = slice(cb * LANES, (cb + 1) * LANES)
            scr_ref[cb] = p[:, g * gw + cb * LANES:g * gw + (cb + 1) * LANES]
            for r in range(dil):
                out_ref[0, r, :, cols] = scr_ref[cb, pl.ds(r, tm // dil, stride=dil), :].astype(BF16)
    hy_ref[...] = p[:, N_PATTERNS * gw:]


def _in_proj(xa, xb, mods, gain, w_bf, seq, tm):
    d = xa.shape[1]
    rows = xa.shape[0] + xb.shape[0]
    batch = rows // seq
    n = w_bf.shape[1]
    gw = 3 * GROUP_WIDTH
    tpb = seq // tm
    dils = [dil for _, dil in ATTN_PATTERNS]
    assert dils[0] == 1

    def class_spec(dil):
        return pl.BlockSpec((1, dil, tm // dil, gw), lambda i: (i // tpb, 0, i % tpb, 0))

    return pl.pallas_call(
        functools.partial(_inproj_kernel, n_first=xa.shape[0] // tm),
        grid=(rows // tm,),
        in_specs=_two_source_specs(tm, d, xa.shape[0] // tm) + [
            pl.BlockSpec((1, 6, d), lambda i: (i // tpb, 0, 0)),
            _const_spec((1, d)),
            _const_spec((d, n))],
        out_specs=[pl.BlockSpec((tm, gw), lambda i: (i, 0)), class_spec(dils[1]), class_spec(dils[2]),
                   pl.BlockSpec((tm, n - N_PATTERNS * gw), lambda i: (i, 0))],
        out_shape=[jax.ShapeDtypeStruct((rows, gw), BF16)]
        + [jax.ShapeDtypeStruct((batch, dil, seq // dil, gw), BF16) for dil in dils[1:]]
        + [jax.ShapeDtypeStruct((rows, n - N_PATTERNS * gw), F32)],
        scratch_shapes=[pltpu.VMEM((gw // LANES, tm, LANES), F32)],
        compiler_params=_cparams(("parallel",)),
        name="in_proj",
    )(xa, xb, mods, gain.reshape(1, d), w_bf)


def _attn_kernel(q_ref, k_ref, v_ref, bias_ref, o_ref, lse_ref, kpad, vpad, *, n):
    nq = n // ATTN_TQ
    zeros = jnp.zeros((ATTN_RADIUS, GROUP_WIDTH), BF16)
    kpad[0:ATTN_RADIUS, :] = zeros
    vpad[0:ATTN_RADIUS, :] = zeros
    kpad[n + ATTN_RADIUS:n + 2 * ATTN_RADIUS, :] = zeros
    vpad[n + ATTN_RADIUS:n + 2 * ATTN_RADIUS, :] = zeros
    kpad[ATTN_RADIUS:n + ATTN_RADIUS, :] = k_ref[...]
    vpad[ATTN_RADIUS:n + ATTN_RADIUS, :] = v_ref[...]
    lane = lax.broadcasted_iota(jnp.int32, (1, LANES), 1)
    low = lane < HEAD_DIM

    def body(i, carry):
        r0 = pl.multiple_of(i * ATTN_TQ, ATTN_TQ)
        q = q_ref[pl.ds(r0, ATTN_TQ), :]
        kw = kpad[pl.ds(r0, ATTN_TK), :]
        vw = vpad[pl.ds(r0, ATTN_TK), :]
        sel = jnp.where(i == 0, 0, jnp.where(i == nq - 1, 2, 1))
        heads = range(HEADS_PER_GROUP)
        cols = [slice((h // 2) * LANES, (h // 2 + 1) * LANES) for h in heads]
        qm = [jnp.where(low if h % 2 == 0 else jnp.logical_not(low), q[:, cols[h]], jnp.zeros((), BF16))
              for h in heads]
        s = [lax.dot_general(qm[h], kw[:, cols[h]], (((1,), (1,)), ((), ())), preferred_element_type=F32)
             for h in heads]
        s = [s[h] * (HEAD_DIM ** -0.5) + bias_ref[h, sel] for h in heads]
        m = [jnp.max(s[h], axis=-1, keepdims=True) for h in heads]
        p = [jnp.exp(s[h] - m[h]) for h in heads]
        den = [jnp.sum(p[h], axis=-1, keepdims=True) for h in heads]
        o = [jnp.dot(p[h].astype(BF16), vw[:, cols[h]], preferred_element_type=F32) for h in heads]
        o = [o[h] / den[h] for h in heads]
        lse = [jnp.broadcast_to(m[h] + jnp.log(den[h]), (ATTN_TQ, LANES)) for h in heads]
        for pair in range(GROUP_WIDTH // LANES):
            o_ref[pl.ds(r0, ATTN_TQ), cols[2 * pair]] = jnp.where(low, o[2 * pair], o[2 * pair + 1]).astype(o_ref.dtype)
            lse_ref[pl.ds(r0, ATTN_TQ), cols[2 * pair]] = jnp.where(low, lse[2 * pair], lse[2 * pair + 1])
        return carry

    lax.fori_loop(0, nq, body, 0)


def _attn_bias(group, dilation):
    slopes = np.exp2(-ALIBI_MAX_EXP * (np.arange(N_ATTN_HEADS, dtype=np.float32) + 1.0) / N_ATTN_HEADS)
    slopes = slopes.reshape(N_PATTERNS, HEADS_PER_GROUP)[group].astype(np.float32)
    qi = np.arange(ATTN_TQ)[:, None]
    kj = np.arange(ATTN_TK)[None, :]
    rel = kj - ATTN_RADIUS - qi
    band = np.abs(rel) <= ATTN_RADIUS
    alibi = -slopes[:, None, None] * (np.abs(rel) * dilation).astype(np.float32)[None]
    kinds = []
    for lo, hi in ((ATTN_RADIUS, ATTN_TK), (0, ATTN_TK), (0, ATTN_TQ + ATTN_RADIUS)):
        valid = band & (kj >= lo) & (kj < hi)
        kinds.append(np.where(valid[None], alibi, np.float32(NEG_INF)))
    return jnp.asarray(np.stack(kinds, axis=1), dtype=F32)


def _attention_group(qkv, group):
    batch, dilation, n, _ = qkv.shape
    assert n % ATTN_TQ == 0 and n >= 2 * ATTN_TQ

    def part(p):
        return pl.BlockSpec((None, None, n, GROUP_WIDTH), lambda b, r: (b, r, 0, p))

    out_blk = pl.BlockSpec((None, None, n, GROUP_WIDTH), lambda b, r: (b, r, 0, 0))
    return pl.pallas_call(
        functools.partial(_attn_kernel, n=n),
        grid=(batch, dilation),
        in_specs=[part(0), part(1), part(2), _const_spec((HEADS_PER_GROUP, 3, ATTN_TQ, ATTN_TK))],
        out_specs=[out_blk, out_blk],
        out_shape=[jax.ShapeDtypeStruct((batch, dilation, n, GROUP_WIDTH), BF16),
                   jax.ShapeDtypeStruct((batch, dilation, n, GROUP_WIDTH), F32)],
        scratch_shapes=[pltpu.VMEM((n + 2 * ATTN_RADIUS, GROUP_WIDTH), BF16)] * 2,
        compiler_params=_cparams(("parallel", "parallel")),
        name=f"attn_g{group}",
    )(qkv, qkv, qkv, _attn_bias(group, dilation))


def _filter_kernel(bands_ref, deltas_ref, w1_ref, b1_ref, w2_ref, b2_ref, w3_ref, b3_ref, w4_ref, freq_ref,
                   ft_ref, *, seq, tl):
    i = pl.program_id(0)
    freq = freq_ref[...]
    lane = lax.broadcasted_iota(jnp.int32, (1, LANES), 1)
    row = lax.broadcasted_iota(jnp.int32, (tl, 1), 0) + i * tl

    def half_filter(pos, half):
        posf = pos.astype(F32)
        t = posf / float(seq - 1)
        z = bands_ref[...] * (2.0 * math.pi * posf / float(seq))
        feat = jnp.where(lane == 0, t,
                         jnp.where(lane <= HYENA_BANDS, jnp.cos(z),
                                   jnp.where(lane <= 2 * HYENA_BANDS, -jnp.sin(z), 0.0)))
        h = jnp.sin(freq * (_mm(feat, w1_ref[...]) + b1_ref[...]))
        h = jnp.sin(freq * (_mm(h, w2_ref[...]) + b2_ref[...]))
        h = jnp.sin(freq * (_mm(h, w3_ref[...]) + b3_ref[...]))
        h = _mm(h, w4_ref[:, half * HYENA_WIDTH:(half + 1) * HYENA_WIDTH])
        return h * jnp.exp(-t * jnp.abs(deltas_ref[...]))

    hf = half_filter(row, 0)
    hb = half_filter(jnp.where(row == 0, 0, seq - row), 1)
    hb = jnp.where(row == 0, 0.0, hb)

    @pl.when(i == 0)
    def _():
        ft_ref[:, 0:CONV_P] = jnp.zeros((HYENA_WIDTH, CONV_P), F32)

    c0 = pl.multiple_of(CONV_P + i * tl, LANES)
    ft_ref[:, pl.ds(c0, tl)] = hb.T
    c1 = pl.multiple_of(CONV_P + seq + i * tl, LANES)
    ft_ref[:, pl.ds(c1, tl)] = hf.T

    @pl.when(i == pl.num_programs(0) - 1)
    def _():
        full = ft_ref[...]
        norm = jnp.sum(jnp.abs(full), axis=1, keepdims=True)
        ft_ref[...] = full / norm


def _hyena_filter(seq, w1, b1, w2, b2, w3, b3, w4, freq):
    tl = 512
    f = jnp.linspace(1e-4, HYENA_BANDS - 1, HYENA_BANDS, dtype=F32)
    bands = jnp.concatenate([jnp.zeros((1,), F32), f, f, jnp.zeros((LANES - HYENA_EMB,), F32)]).reshape(1, LANES)
    max_decay = math.log(HYENA_TARGET) / HYENA_FAST_DECAY
    min_decay = math.log(HYENA_TARGET) / HYENA_SLOW_DECAY
    deltas = jnp.linspace(min_decay, max_decay, HYENA_WIDTH, dtype=F32).reshape(1, HYENA_WIDTH)
    w1p = jnp.pad(w1, ((0, LANES - HYENA_EMB), (0, 0)))
    row = lambda a: a.reshape(1, -1)
    args = (bands, deltas, w1p, row(b1), w2, row(b2), w3, row(b3), w4, row(freq))
    return pl.pallas_call(
        functools.partial(_filter_kernel, seq=seq, tl=tl),
        grid=(seq // tl,),
        in_specs=[_const_spec(a.shape) for a in args],
        out_specs=pl.BlockSpec((HYENA_WIDTH, CONV_P + 2 * seq), lambda i: (0, 0)),
        out_shape=jax.ShapeDtypeStruct((HYENA_WIDTH, CONV_P + 2 * seq), F32),
        compiler_params=_cparams(("arbitrary",)),
        name="hyena_filter",
    )(*args)


def _shift_rows(x, prev_row, next_row):
    n = x.shape[0]
    row = lax.broadcasted_iota(jnp.int32, (n, 1), 0)
    xm = jnp.where(row == 0, prev_row, pltpu.roll(x, 1, axis=0))
    xp = jnp.where(row == n - 1, next_row, pltpu.roll(x, n - 1, axis=0))
    return xm, xp


def _hyena_pre_kernel(x_ref, prev_ref, next_ref, w_ref, b_ref, zt_ref, x0t_ref):
    i = pl.program_id(1)
    x = x_ref[0]
    prev_row = jnp.where(i == 0, 0.0, prev_ref[0, SUBLANES - 1:SUBLANES, :])
    next_row = jnp.where(i == pl.num_programs(1) - 1, 0.0, next_ref[0, 0:1, :])
    xm, xp = _shift_rows(x, prev_row, next_row)
    u = xm * w_ref[0:1, :] + x * w_ref[1:2, :] + xp * w_ref[2:3, :] + b_ref[...]
    c = HYENA_WIDTH
    x0, x1, v = u[:, :c], u[:, c:2 * c], u[:, 2 * c:]
    zt_ref[0] = (v * x1).T
    x0t_ref[0] = x0.T


def _hyena_pre(hy3, short_w, short_b, tl):
    batch, seq, width = hy3.shape
    nt = seq // tl
    hb = tl // SUBLANES
    last = seq // SUBLANES - 1
    out_blk = pl.BlockSpec((1, HYENA_WIDTH, tl), lambda b, i: (b, 0, i))
    return pl.pallas_call(
        _hyena_pre_kernel,
        grid=(batch, nt),
        in_specs=[pl.BlockSpec((1, tl, width), lambda b, i: (b, i, 0)),
                  pl.BlockSpec((1, SUBLANES, width), lambda b, i: (b, jnp.maximum(i * hb - 1, 0), 0)),
                  pl.BlockSpec((1, SUBLANES, width), lambda b, i: (b, jnp.minimum((i + 1) * hb, last), 0)),
                  _const_spec((3, width)), _const_spec((1, width))],
        out_specs=[out_blk, out_blk],
        out_shape=[jax.ShapeDtypeStruct((batch, HYENA_WIDTH, seq), F32)] * 2,
        compiler_params=_cparams(("parallel", "parallel")),
        name="hyena_pre",
    )(hy3, hy3, hy3, short_w, short_b.reshape(1, width))


def _hyena_conv_kernel(bias_ref, zt_ref, x0t_ref, ft_ref, o_ref, troll, zpad, *, seq, cb):
    batch = zt_ref.shape[0]
    nb = seq // CONV_P
    p = CONV_P
    g = pl.program_id(0)
    zero_margin = jnp.zeros((batch, nb, p), F32)
    zpad[:, 0:nb, :] = zero_margin
    zpad[:, 2 * nb:3 * nb, :] = zero_margin
    chunk = 1024

    def channel(ci, carry):
        for w in range(2 * seq // chunk):
            a = w * chunk
            src = ft_ref[pl.ds(ci, 1), a:a + chunk + p]
            rolled = pltpu.roll(jnp.broadcast_to(src, (p, chunk + p)), 0, axis=1, stride=1, stride_axis=0)
            troll[:, a:a + chunk] = rolled[:, p:].astype(BF16)
        z = zt_ref[:, ci, :, :]
        zpad[:, nb:2 * nb, :] = z
        acc = jnp.zeros((batch * nb, p), F32)
        for d in range(-(nb - 1), nb):
            zs = zpad[:, nb - d:2 * nb - d, :].reshape(batch * nb, p).astype(BF16)
            t = troll[:, seq + d * p:seq + (d + 1) * p]
            acc = acc + jnp.dot(zs, t, preferred_element_type=F32)
        bias = bias_ref[g * cb + ci]
        y = (acc.reshape(batch, nb, p) + z * bias) * x0t_ref[:, ci, :, :]
        o_ref[:, ci, :, :] = y
        return carry

    lax.fori_loop(0, cb, channel, 0)


def _hyena_conv(zt, x0t, ft, filt_bias, cb):
    batch, c, seq = zt.shape
    nb = seq // CONV_P
    z4 = zt.reshape(batch, c, nb, CONV_P)
    x4 = x0t.reshape(batch, c, nb, CONV_P)
    blk = pl.BlockSpec((batch, cb, nb, CONV_P), lambda g: (0, g, 0, 0))
    out = pl.pallas_call(
        functools.partial(_hyena_conv_kernel, seq=seq, cb=cb),
        grid=(c // cb,),
        in_specs=[pl.BlockSpec(memory_space=pltpu.SMEM), blk, blk,
                  pl.BlockSpec((cb, CONV_P + 2 * seq), lambda g: (g, 0))],
        out_specs=blk,
        out_shape=jax.ShapeDtypeStruct((batch, c, nb, CONV_P), F32),
        scratch_shapes=[pltpu.VMEM((CONV_P, 2 * seq), BF16),
                        pltpu.VMEM((batch, 3 * nb, CONV_P), F32)],
        compiler_params=_cparams(("parallel",)),
        name="hyena_conv",
    )(filt_bias, z4, x4, ft)
    return out.reshape(batch, c, seq)


def _interleave_classes(blk_ref, scr_ref):
    dil, per, width = blk_ref.shape
    slabs = width // LANES
    for r in range(dil):
        rows = blk_ref[r].astype(F32)
        for cb in range(slabs):
            scr_ref[cb, pl.ds(r, per, stride=dil), :] = rows[:, cb * LANES:(cb + 1) * LANES]
    return jnp.concatenate([scr_ref[cb] for cb in range(slabs)], axis=1)


def _outproj_kernel(o0, o1, o2, l0, l1, l2, hyt_ref, xa_ref, xb_ref, mod_ref, wa_ref, wh_ref, out_ref,
                    so1, so2, sl1, sl2, *, n_first):
    ls = [l0[0], _interleave_classes(l1, sl1), _interleave_classes(l2, sl2)]
    os_ = [o0[0].astype(F32), _interleave_classes(o1, so1), _interleave_classes(o2, so2)]
    x = jnp.where(pl.program_id(0) < n_first, xa_ref[...], xb_ref[...])
    m = jnp.maximum(jnp.maximum(ls[0], ls[1]), ls[2])
    es = [jnp.exp(l - m) for l in ls]
    den = es[0] + es[1] + es[2]
    attn = (es[0] * os_[0] + es[1] * os_[1] + es[2] * os_[2]) / den
    hy = hyt_ref[0].T
    mix = jnp.dot(attn.astype(BF16), wa_ref[...], preferred_element_type=F32)
    mix = mix + jnp.dot(hy.astype(BF16), wh_ref[...], preferred_element_type=F32)
    out_ref[...] = x + mod_ref[0, 2:3, :] * mix


def _out_proj(os_, ls, hyt, xa, xb, mods, w_out_bf, seq, tm):
    d = xa.shape[1]
    rows = xa.shape[0] + xb.shape[0]
    tpb = seq // tm
    wa, wh = w_out_bf[:GROUP_WIDTH], w_out_bf[GROUP_WIDTH:]

    def class_spec(a):
        dil = a.shape[1]
        return pl.BlockSpec((None, dil, tm // dil, GROUP_WIDTH), lambda i: (i // tpb, 0, i % tpb, 0))

    return pl.pallas_call(
        functools.partial(_outproj_kernel, n_first=xa.shape[0] // tm),
        grid=(rows // tm,),
        in_specs=[class_spec(a) for a in (*os_, *ls)] + [
            pl.BlockSpec((1, HYENA_WIDTH, tm), lambda i: (i // tpb, 0, i % tpb))]
        + _two_source_specs(tm, d, xa.shape[0] // tm) + [
            pl.BlockSpec((1, 6, d), lambda i: (i // tpb, 0, 0)),
            _const_spec(wa.shape), _const_spec(wh.shape)],
        out_specs=pl.BlockSpec((tm, d), lambda i: (i, 0)),
        out_shape=jax.ShapeDtypeStruct((rows, d), F32),
        scratch_shapes=[pltpu.VMEM((GROUP_WIDTH // LANES, tm, LANES), F32)] * 4,
        compiler_params=_cparams(("parallel",)),
        name="out_proj",
    )(*os_, *ls, hyt, xa, xb, mods, wa, wh)


def _halo_rows(x_ref, prev_ref, next_ref, tiles_per_seq):
    i = pl.program_id(0)
    first = (i % tiles_per_seq) == 0
    last = (i % tiles_per_seq) == tiles_per_seq - 1
    xe = jnp.concatenate([prev_ref[...], x_ref[...], next_ref[...]], axis=0)
    return xe, first, last


def _ffn_kernel(x_ref, prev_ref, next_ref, mod_ref, gain_ref, wa_ref, wg_ref, cw_ref, cb_ref, wd_ref, fin_ref,
                *outs_and_scratch, tiles_per_seq, fc, final_norm, n_first):
    *out_refs, act_ref, a0_ref, g0_ref, a1_ref, g1_ref = outs_and_scratch
    tm = x_ref.shape[0]
    x = x_ref[...]
    xe, first, last = _halo_rows(x_ref, prev_ref, next_ref, tiles_per_seq)
    he32 = _norm_mod(xe, gain_ref[...], mod_ref[0, 3:4, :], mod_ref[0, 4:5, :])
    he = he32.astype(BF16)
    h = he32[SUBLANES:SUBLANES + tm].astype(BF16)
    row = lax.broadcasted_iota(jnp.int32, (tm + 2 * SUBLANES, 1), 0)
    lo = jnp.where(first, SUBLANES, 0)
    hi = jnp.where(last, tm + SUBLANES, tm + 2 * SUBLANES)
    keep = jnp.logical_and(row >= lo, row < hi)
    nchunk = wa_ref.shape[1] // fc
    n_ext = tm + 2 * SUBLANES

    def up(j, bufs):
        c0 = pl.multiple_of(j * fc, fc)
        bufs[0][...] = jnp.dot(he, wa_ref[:, pl.ds(c0, fc)], preferred_element_type=F32)
        bufs[1][...] = jnp.dot(h, wg_ref[:, pl.ds(c0, fc)], preferred_element_type=F32)

    def activate(j, bufs):
        c0 = pl.multiple_of(j * fc, fc)
        a = jnp.where(keep, bufs[0][...], 0.0)
        am = pltpu.roll(a, 1, axis=0)[SUBLANES:SUBLANES + tm]
        ap = pltpu.roll(a, n_ext - 1, axis=0)[SUBLANES:SUBLANES + tm]
        ac = a[SUBLANES:SUBLANES + tm]
        cw = cw_ref[:, pl.ds(c0, fc)]
        conv = am * cw[0:1] + ac * cw[1:2] + ap * cw[2:3] + cb_ref[:, pl.ds(c0, fc)]
        act_ref[:, pl.ds(c0, fc)] = (jax.nn.gelu(conv) * bufs[1][...]).astype(BF16)

    even, odd = (a0_ref, g0_ref), (a1_ref, g1_ref)
    up(0, even)

    def body(j, carry):
        up(2 * j + 1, odd)
        activate(2 * j, even)
        up(2 * j + 2, even)
        activate(2 * j + 1, odd)
        return carry

    assert nchunk % 2 == 1
    lax.fori_loop(0, nchunk // 2, body, 0)
    activate(nchunk - 1, even)
    down = jnp.dot(act_ref[...], wd_ref[...], preferred_element_type=F32)
    y = x + mod_ref[0, 5:6, :] * down
    if final_norm:
        ms = jnp.mean(y * y, axis=-1, keepdims=True)
        y = y * lax.rsqrt(ms + RMS_EPS) * fin_ref[...]
    if n_first is None:
        out_refs[0][...] = y
    else:
        i = pl.program_id(0)

        @pl.when(i < n_first)
        def _():
            out_refs[0][...] = y

        @pl.when(i >= n_first)
        def _():
            out_refs[1][...] = y


def _halo_specs(tm, d, nrows):
    hb = tm // SUBLANES
    last = nrows // SUBLANES - 1
    return [pl.BlockSpec((tm, d), lambda i: (i, 0)),
            pl.BlockSpec((SUBLANES, d), lambda i: (jnp.maximum(i * hb - 1, 0), 0)),
            pl.BlockSpec((SUBLANES, d), lambda i: (jnp.minimum((i + 1) * hb, last), 0))]


def _conv_ffn(x2, mods, gain, up_bf, conv_w, conv_b, down_bf, fin_gain, seq, tm, final_norm, split_rows=None):
    rows, d = x2.shape
    dff = down_bf.shape[0]
    fc = 256
    tpb = seq // tm
    wa, wg = up_bf[:, :dff], up_bf[:, dff:]
    if split_rows is None:
        n_first = None
        out_specs = pl.BlockSpec((tm, d), lambda i: (i, 0))
        out_shape = jax.ShapeDtypeStruct((rows, d), F32)
    else:
        n_first = split_rows // tm
        out_specs = _two_source_specs(tm, d, n_first)
        out_shape = [jax.ShapeDtypeStruct((split_rows, d), F32), jax.ShapeDtypeStruct((rows - split_rows, d), F32)]
    return pl.pallas_call(
        functools.partial(_ffn_kernel, tiles_per_seq=tpb, fc=fc, final_norm=final_norm, n_first=n_first),
        grid=(rows // tm,),
        in_specs=_halo_specs(tm, d, rows) + [
            pl.BlockSpec((1, 6, d), lambda i: (i // tpb, 0, 0)),
            _const_spec((1, d)), _const_spec(wa.shape), _const_spec(wg.shape),
            _const_spec((3, dff)), _const_spec((1, dff)), _const_spec(down_bf.shape), _const_spec((1, d))],
        out_specs=out_specs,
        out_shape=out_shape,
        scratch_shapes=[pltpu.VMEM((tm, dff), BF16)]
        + [pltpu.VMEM((tm + 2 * SUBLANES, fc), F32), pltpu.VMEM((tm, fc), F32)] * 2,
        compiler_params=_cparams(("arbitrary",)),
        name="conv_ffn",
    )(x2, x2, x2, mods, gain.reshape(1, d), wa, wg, conv_w, conv_b.reshape(1, dff), down_bf,
      fin_gain.reshape(1, d))


def _head_sum(x, ones_bd):
    xb = x.astype(BF16)
    parts = [jnp.dot(xb[:, c * MXU_DIM:(c + 1) * MXU_DIM], ones_bd, preferred_element_type=F32)
             for c in range(x.shape[1] // MXU_DIM)]
    return jnp.concatenate(parts, axis=1)


def _sigmoid(x):
    return 0.5 * jnp.tanh(0.5 * x) + 0.5


def _ones_blockdiag():
    r = lax.broadcasted_iota(jnp.int32, (MXU_DIM, MXU_DIM), 0) // RWKV_HEAD
    c = lax.broadcasted_iota(jnp.int32, (MXU_DIM, MXU_DIM), 1) // RWKV_HEAD
    return jnp.where(r == c, 1.0, 0.0).astype(BF16)


def _rwkv_pre_kernel(x_ref, prev_ref, next_ref, mod_ref, gain_ref, mu_ref, wr_ref, wk_ref, wv_ref,
                     w1_ref, w2_ref, w0_ref, a1_ref, a2_ref, a0_ref, g1_ref, g2_ref, kk_w_ref, ka_ref, rk_ref,
                     r_out, v_out, kk_out, k_out, a0_out, a1_out, lw0_out, lw1_out, g_out, bonus_out, he_ref, hb_ref, xx_ref,
                     *, tiles_per_seq):
    tm, d = x_ref.shape
    i = pl.program_id(0)
    first = (i % tiles_per_seq) == 0
    last = (i % tiles_per_seq) == tiles_per_seq - 1
    norm = lambda x: _norm_mod(x, gain_ref[...], mod_ref[0, 0:1, :], mod_ref[0, 1:2, :])
    ones_bd = _ones_blockdiag()
    he_ref[0:SUBLANES, :] = jnp.where(first, 0.0, norm(prev_ref[...]))
    he_ref[SUBLANES + tm:, :] = jnp.where(last, 0.0, norm(next_ref[...]))

    def row_part(lo, n):
        rows = slice(lo, lo + n)
        mid = slice(SUBLANES + lo, SUBLANES + lo + n)
        h = norm(x_ref[rows, :])
        he_ref[mid, :] = h
        hb_ref[rows, :] = h.astype(BF16)
        yield
        hm = he_ref[SUBLANES + lo - 1:SUBLANES + lo - 1 + n, :]
        hp = he_ref[SUBLANES + lo + 1:SUBLANES + lo + 1 + n, :]
        xx_ref[rows, :] = (0.5 * (hm + hp) - he_ref[mid, :]).astype(BF16)
        bf16_rows = 2 * SUBLANES

        def mixed(i):
            mu_tile = jnp.broadcast_to(mu_ref[i:i + 1, :], (bf16_rows, d)).astype(BF16)
            return hb_ref[rows, :] + xx_ref[rows, :] * jnp.tile(mu_tile, (n // bf16_rows, 1))

        r = jnp.dot(mixed(0), wr_ref[...], preferred_element_type=F32)
        r_out[rows, :] = r.astype(BF16)
        yield
        k = jnp.dot(mixed(2), wk_ref[...], preferred_element_type=F32)
        k_out[rows, :] = k.astype(BF16)
        kk = k * kk_w_ref[...]
        kk_out[rows, :] = (kk * jnp.minimum(lax.rsqrt(_head_sum(kk * kk, ones_bd)), 1e12)).astype(BF16)
        yield
        v = jnp.dot(mixed(3), wv_ref[...], preferred_element_type=F32)
        v_out[rows, :] = v.astype(BF16)
        yield
        gl = _sigmoid(jnp.dot(mixed(5), g1_ref[...], preferred_element_type=F32))
        g_out[rows, :] = jnp.dot(gl.astype(BF16), g2_ref[...], preferred_element_type=F32)
        yield
        wl = jnp.tanh(jnp.dot(mixed(1), w1_ref[...], preferred_element_type=F32))
        wl = jnp.dot(wl.astype(BF16), w2_ref[...], preferred_element_type=F32)
        for direction, lw_o in enumerate((lw0_out, lw1_out)):
            u = w0_ref[direction:direction + 1, :] + wl[:, direction * d:(direction + 1) * d]
            lw_o[rows, :] = -_sigmoid(u) * math.exp(-0.5)
        yield
        al = jnp.dot(mixed(4), a1_ref[...], preferred_element_type=F32)
        al = jnp.dot(al.astype(BF16), a2_ref[...], preferred_element_type=F32)
        ksum = jnp.zeros_like(k)
        for direction, a_o in enumerate((a0_out, a1_out)):
            a = _sigmoid(a0_ref[direction:direction + 1, :] + al[:, direction * d:(direction + 1) * d])
            a_o[rows, :] = a.astype(BF16)
            ksum = ksum + _k_dir(k, a, ka_ref[...])
        bonus_out[rows, :] = _head_sum(r * (0.5 * ksum) * rk_ref[...], ones_bd) * v

    parts = 2
    _emit_interleaved(*[row_part(p * (tm // parts), tm // parts) for p in range(parts)])


def _k_dir(k, a, k_a):
    return k * (1.0 + (a - 1.0) * k_a)


def _blockdiag2(m):
    z = jnp.zeros_like(m[0])
    return jnp.concatenate([jnp.concatenate([m[0], z], axis=1), jnp.concatenate([z, m[1]], axis=1)], axis=0)


def _rwkv_pre(x2, mods, p, seq, tm):
    rows, d = x2.shape
    tpb = seq // tm
    bf = lambda a: a.astype(BF16)
    w1 = bf(jnp.concatenate([p['w1'][0], p['w1'][1]], axis=1))
    w2 = bf(_blockdiag2(p['w2']))
    a1 = bf(jnp.concatenate([p['a1'][0], p['a1'][1]], axis=1))
    a2 = bf(_blockdiag2(p['a2']))
    glora = p['g1'].shape[1]
    gpad = -glora % LANES
    g1 = bf(jnp.pad(p['g1'], ((0, 0), (0, gpad))))
    g2 = bf(jnp.pad(p['g2'], ((0, gpad), (0, 0))))
    row = lambda a: a.reshape(1, d)
    args = (x2, x2, x2, mods, row(p['norm1']), p['mu'], bf(p['w_r']), bf(p['w_k']), bf(p['w_v']),
            w1, w2, p['w0'], a1, a2, p['a0'], g1, g2, row(p['k_k']), row(p['k_a']), row(p['r_k']))
    in_specs = _halo_specs(tm, d, rows) + [pl.BlockSpec((1, 6, d), lambda i: (i // tpb, 0, 0))]
    in_specs += [_const_spec(a.shape) for a in args[4:]]
    ob = pl.BlockSpec((tm, d), lambda i: (i, 0))
    return pl.pallas_call(
        functools.partial(_rwkv_pre_kernel, tiles_per_seq=tpb),
        grid=(rows // tm,),
        in_specs=in_specs,
        out_specs=[ob] * 10,
        out_shape=[jax.ShapeDtypeStruct((rows, d), BF16)] * 6 + [jax.ShapeDtypeStruct((rows, d), F32)] * 4,
        scratch_shapes=[pltpu.VMEM((tm + 2 * SUBLANES, d), F32), pltpu.VMEM((tm, d), BF16),
                        pltpu.VMEM((tm, d), BF16)],
        compiler_params=_cparams(("parallel",)),
        name="rwkv_pre",
    )(*args)


def _scan_masks(reverse):
    c = SCAN_C
    t = lax.broadcasted_iota(jnp.int32, (c, 4 * c), 0)
    s = lax.broadcasted_iota(jnp.int32, (c, 4 * c), 1) % c
    before = (s > t) if reverse else (s < t)
    upto = jnp.logical_or(before, s == t)
    eye = jnp.where(s == t, 1.0, 0.0)
    rr = lax.broadcasted_iota(jnp.int32, (4 * c, SCAN_HG), 0)
    cc = lax.broadcasted_iota(jnp.int32, (4 * c, SCAN_HG), 1)
    same_head = jnp.where((rr // c) == (cc // RWKV_HEAD), 1.0, 0.0)
    eye_big = jnp.where(rr == cc, 1.0, 0.0)
    return before, upto, eye, same_head, same_head.astype(BF16), eye_big


def _bd(xb, head_mask_bf):
    return jnp.concatenate([xb] * 4, axis=0) * head_mask_bf


def _bd_t(x, head_mask_bf):
    xt = jnp.concatenate([x, x], axis=0).T.astype(BF16)
    return jnp.concatenate([xt, xt], axis=1) * head_mask_bf


def _mm(a, b):
    return jnp.dot(a, b, precision=HIGHEST, preferred_element_type=F32)


def _bdot(a, b):
    return jnp.dot(a, b, preferred_element_type=F32)


def _bdot_nt(a, b):
    return lax.dot_general(a, b, (((1,), (1,)), ((), ())), preferred_element_type=F32)


def _cumsum_rows(tri_bf, x):
    hi = x.astype(BF16)
    lo = (x - hi.astype(F32)).astype(BF16)
    return _bdot(tri_bf, hi) + _bdot(tri_bf, lo)


def _scan_prologue(chains):
    c = SCAN_C
    bf = lambda x: x.astype(BF16)
    idx = range(len(chains))
    r, v, kk, lw, k, a = [[ch[0][j] for ch in chains] for j in range(6)]
    head_mask_bf = chains[0][1][5]
    rev = [ch[2] for ch in chains]
    g = [_cumsum_rows(chains[i][1][0], lw[i]) for i in idx]
    gtot = [g[i][0:1] if rev[i] else g[i][c - 1:c] for i in idx]
    kkd = [kk[i] * jnp.exp(g[i] - lw[i]) for i in idx]
    rd = [r[i] * jnp.exp(g[i]) for i in idx]
    b = [kk[i] * a[i] for i in idx]
    ginv = [jnp.exp(-g[i]) for i in idx]
    lhs = [bf(jnp.concatenate([kkd[i], rd[i]], axis=0)) for i in idx]
    bi_t = [_bd_t(b[i] * ginv[i], head_mask_bf) for i in idx]
    ki_t = [_bd_t(k[i] * ginv[i], head_mask_bf) for i in idx]
    return chains, (v, k, g, gtot, kkd, rd, b, lhs, bi_t, ki_t)


def _scan_prepare_stages(prologue, out):
    c = SCAN_C
    bf = lambda x: x.astype(BF16)
    chains, (v, k, g, gtot, kkd, rd, b, lhs, bi_t, ki_t) = prologue
    idx = range(len(chains))
    before, upto, eye = [[ch[1][j] for ch in chains] for j in range(1, 4)]
    head_mask_bf = chains[0][1][5]
    rev = [ch[2] for ch in chains]
    bd = lambda x: _bd(bf(x), head_mask_bf)

    a1 = [_bdot(lhs[i], bi_t[i]) for i in idx]
    a2 = [_bdot(lhs[i], ki_t[i]) for i in idx]
    n_ab = [jnp.where(before[i], a1[i][:c], 0.0) for i in idx]
    a_rb = [jnp.where(upto[i], a1[i][c:], 0.0) for i in idx]
    a_k = [jnp.concatenate([jnp.where(before[i], a2[i][:c], 0.0), jnp.where(upto[i], a2[i][c:], 0.0)], axis=0)
           for i in idx]
    yield

    nb = [bf(n_ab[i]) for i in idx]
    npow = [_bdot(nb[i], _bd(nb[i], head_mask_bf)) for i in idx]
    tinv = [eye[i] - n_ab[i] for i in idx]
    yield
    steps = int(math.log2(c)) - 1
    for step in range(steps):
        pb = [bf(npow[i]) for i in idx]
        rhs = [_bd(pb[i], head_mask_bf) for i in idx]
        if step == steps - 1:
            tinv = [tinv[i] + _bdot(bf(tinv[i]), rhs[i]) for i in idx]
        else:
            prod = [_bdot(jnp.concatenate([pb[i], bf(tinv[i])], axis=0), rhs[i]) for i in idx]
            tinv = [tinv[i] + prod[i][c:] for i in idx]
            npow = [prod[i][:c] for i in idx]
        yield

    vb = [bf(v[i]) for i in idx]
    av = [_bdot(bf(a_k[i]), _bd(vb[i], head_mask_bf)) for i in idx]
    yield
    tw = [_bdot(bf(tinv[i]), jnp.concatenate([bd(kkd[i]), bd(av[i][:c])], axis=1)) for i in idx]
    lhs_state = [bf(jnp.concatenate([rd[i], tw[i][:, :SCAN_HG]], axis=0)) for i in idx]
    wt = [tw[i][:, SCAN_HG:] for i in idx]
    yield
    gend = [jnp.exp(gtot[i] - g[i]) for i in idx]
    upd_t = [bf(jnp.concatenate([b[i] * gend[i], k[i] * gend[i]], axis=0).T) for i in idx]
    gcol_src = [jnp.concatenate([g[i], g[i]], axis=0).T for i in idx]
    gcol = [jnp.exp(gcol_src[i][:, 0:1] if rev[i] else gcol_src[i][:, c - 1:c]) for i in idx]
    out.extend((lhs_state[i], wt[i], bf(a_rb[i]), av[i][c:], upd_t[i], gcol[i], vb[i]) for i in idx)


def _scan_apply_stages(preps, states, head_mask, head_mask_bf, emit_y):
    c = SCAN_C
    for pos in range(0, len(preps), 2):
        lhs, wt, a_rb, av_r, upd_t, gcol, vb = zip(*preps[pos:pos + 2])
        o = [_bdot(lhs[i], states[i].astype(BF16)) for i in range(2)]
        yield
        ub = [(-(wt[i] + o[i][c:])).astype(BF16) for i in range(2)]
        upd = [_bdot(upd_t[i], jnp.concatenate([ub[i], vb[i]], axis=0)) for i in range(2)]
        y = [o[i][:c] + av_r[i] + _bdot(a_rb[i], _bd(ub[i], head_mask_bf)) for i in range(2)]
        yield
        for i in range(2):
            emit_y(pos + i, y[i])
            states[i] = states[i] * gcol[i] + head_mask * upd[i]


def _emit_interleaved(*stage_generators):
    pending = list(stage_generators)
    while pending:
        for gen in list(pending):
            try:
                next(gen)
            except StopIteration:
                pending.remove(gen)


def _scan_consts(reverse):
    c = SCAN_C
    before, upto, eye, head_mask, head_mask_bf, _ = _scan_masks(reverse)
    tt = lax.broadcasted_iota(jnp.int32, (c, c), 0)
    ss = lax.broadcasted_iota(jnp.int32, (c, c), 1)
    tri = jnp.where((ss >= tt) if reverse else (ss <= tt), 1.0, 0.0).astype(BF16)
    return tri, before, upto, eye, head_mask, head_mask_bf


def _scan_kernel(ka_ref, rf_ref, vf_ref, kkf_ref, kf_ref, af_ref, lwf_ref, rb_ref, vb_ref, kkb_ref, kb_ref, ab_ref,
                 lwb_ref, yf_ref, yb_ref, sf_ref, sb_ref):
    tb = rf_ref.shape[0]
    nchunk = tb // SCAN_C

    @pl.when(pl.program_id(2) == 0)
    def _():
        sf_ref[...] = jnp.zeros_like(sf_ref)
        sb_ref[...] = jnp.zeros_like(sb_ref)

    consts_f = _scan_consts(False)
    consts_b = _scan_consts(True)
    head_mask = consts_f[4]
    k_a = ka_ref[...]

    def operands(refs, rows):
        r, v, kk, k, a = (ref[rows, :].astype(F32) for ref in refs[:5])
        return r, v, kk, refs[5][rows, :], _k_dir(k, a, k_a), a

    fwd_refs = (rf_ref, vf_ref, kkf_ref, kf_ref, af_ref, lwf_ref)
    bwd_refs = (rb_ref, vb_ref, kkb_ref, kb_ref, ab_ref, lwb_ref)

    def emitter(rows):
        def emit_y(pos, y):
            ref = yb_ref if pos % 2 else yf_ref
            ref[rows[pos], :] = y.astype(ref.dtype)
        return emit_y

    def group_rows(group):
        rows = []
        for u in range(SCAN_UNROLL):
            j = group * SCAN_UNROLL + u
            rows += [pl.ds(j * SCAN_C, SCAN_C), pl.ds((nchunk - 1 - j) * SCAN_C, SCAN_C)]
        return rows

    def group_prologue(rows):
        chains = []
        for pos, rs in enumerate(rows):
            reverse = pos % 2 == 1
            chains.append((operands(bwd_refs if reverse else fwd_refs, rs), consts_b if reverse else consts_f,
                           reverse))
        return _scan_prologue(chains)

    def run_once(fn, box):
        box.append(fn())
        yield

    ngroups = nchunk // SCAN_UNROLL
    states = [sf_ref[...], sb_ref[...]]
    pending = None
    rows = group_rows(0)
    prologue = group_prologue(rows)
    for group in range(ngroups):
        preps, next_box = [], []
        stages = [_scan_prepare_stages(prologue, preps)]
        if pending is not None:
            stages.append(_scan_apply_stages(pending[0], states, head_mask, consts_f[5], emitter(pending[1])))
        if group + 1 < ngroups:
            next_rows = group_rows(group + 1)
            stages.append(run_once(functools.partial(group_prologue, next_rows), next_box))
        _emit_interleaved(*stages)
        pending = (preps, rows)
        if group + 1 < ngroups:
            rows, prologue = next_rows, next_box[0]
    _emit_interleaved(_scan_apply_stages(pending[0], states, head_mask, consts_f[5], emitter(pending[1])))
    sf_ref[...] = states[0]
    sb_ref[...] = states[1]


def _wkv_scan(r, v, kk, k, a0, a1, lw0, lw1, k_a, batch, seq, tb):
    rows, d = r.shape
    nt = seq // tb
    ng = d // SCAN_HG
    fwd = pl.BlockSpec((tb, SCAN_HG), lambda b, g, t: (b * nt + t, g))
    bwd = pl.BlockSpec((tb, SCAN_HG), lambda b, g, t: (b * nt + nt - 1 - t, g))
    return pl.pallas_call(
        _scan_kernel,
        grid=(batch, ng, nt),
        in_specs=[pl.BlockSpec((1, SCAN_HG), lambda b, g, t: (0, g))] + [fwd] * 6 + [bwd] * 6,
        out_specs=[fwd, bwd],
        out_shape=[jax.ShapeDtypeStruct((rows, d), BF16)] * 2,
        scratch_shapes=[pltpu.VMEM((SCAN_HG, SCAN_HG), F32)] * 2,
        compiler_params=_cparams(("parallel", "parallel", "arbitrary")),
        name="wkv_scan",
    )(k_a.reshape(1, d), r, v, kk, k, a0, lw0, r, v, kk, k, a1, lw1)


def _rwkv_post_kernel(yf_ref, yb_ref, bonus_ref, g_ref, x_ref, mod_ref, lnw_ref, lnb_ref, wo_ref, out_ref):
    y = yf_ref[...].astype(F32) + yb_ref[...].astype(F32)
    ones_bd = _ones_blockdiag()
    mu = _head_sum(y, ones_bd) * (1.0 / RWKV_HEAD)
    yc = y - mu
    var = _head_sum(yc * yc, ones_bd) * (1.0 / RWKV_HEAD)
    yn = yc * lax.rsqrt(var + GN_EPS) * lnw_ref[...] + lnb_ref[...]
    out = ((yn + bonus_ref[...]) * g_ref[...]).astype(BF16)
    mix = jnp.dot(out, wo_ref[...], preferred_element_type=F32)
    out_ref[...] = x_ref[...] + mod_ref[0, 2:3, :] * mix


def _rwkv_post(yf, yb, bonus, g, x2, mods, ln_w, ln_b, wo_bf, seq, tm):
    rows, d = x2.shape
    tpb = seq // tm
    rb = pl.BlockSpec((tm, d), lambda i: (i, 0))
    return pl.pallas_call(
        _rwkv_post_kernel,
        grid=(rows // tm,),
        in_specs=[rb] * 5 + [pl.BlockSpec((1, 6, d), lambda i: (i // tpb, 0, 0)),
                             _const_spec((1, d)), _const_spec((1, d)), _const_spec((d, d))],
        out_specs=rb,
        out_shape=jax.ShapeDtypeStruct((rows, d), F32),
        compiler_params=_cparams(("parallel",)),
        name="rwkv_post",
    )(yf, yb, bonus, g, x2, mods, ln_w.reshape(1, d), ln_b.reshape(1, d), wo_bf)


def _mods(c, w, b):
    batch, d = c.shape
    pad = -batch % SUBLANES
    m = _ada_params(jnp.pad(c, ((0, pad), (0, 0))), w, b)[:batch]
    return m.reshape(batch, 6, d)


def _group_major_w_in(w_in):
    d = w_in.shape[0]
    qkv = w_in[:, :3 * ATTN_WIDTH].reshape(d, 3, N_PATTERNS, GROUP_WIDTH)
    qkv = jnp.transpose(qkv, (0, 2, 1, 3)).reshape(d, 3 * ATTN_WIDTH)
    return jnp.concatenate([qkv, w_in[:, 3 * ATTN_WIDTH:]], axis=1)


def _trunk(xa, xb, c, l0, l1, final_norm):
    seq, d = xa.shape[1:]
    assert xb.shape[1:] == (seq, d)
    batch = xa.shape[0] + xb.shape[0]
    xa2, xb2 = xa.reshape(-1, d), xb.reshape(-1, d)
    bf = lambda a: a.astype(BF16)
    tm = 512

    mods0 = _mods(c, l0['ada_w'], l0['ada_b'])
    q0, q1, q2, hy = _in_proj(xa2, xb2, mods0, l0['norm1'], bf(_group_major_w_in(l0['w_in'])), seq, tm)
    os_, ls = [], []
    for group, qkv in enumerate((q0.reshape(batch, 1, seq, 3 * GROUP_WIDTH), q1, q2)):
        o, lse = _attention_group(qkv, group)
        os_.append(o)
        ls.append(lse)
    ft = _hyena_filter(seq, l0['filt_w1'], l0['filt_b1'], l0['filt_w2'], l0['filt_b2'], l0['filt_w3'],
                       l0['filt_b3'], l0['filt_w4'], l0['filt_freq'])
    zt, x0t = _hyena_pre(hy.reshape(batch, seq, 3 * HYENA_WIDTH), l0['short_w'], l0['short_b'], tl=512)
    hyt = _hyena_conv(zt, x0t, ft, l0['filt_bias'], cb=8)
    x2 = _out_proj(os_, ls, hyt, xa2, xb2, mods0, bf(l0['w_out']), seq, tm)
    x2 = _conv_ffn(x2, mods0, l0['norm2'], bf(l0['ffn_up']), l0['ffn_conv_w'], l0['ffn_conv_b'],
                   bf(l0['ffn_down']), final_norm, seq, tm, final_norm=False)

    mods1 = _mods(c, l1['ada_w'], l1['ada_b'])
    r, v, kk, k, a0, a1, lw0, lw1, g, bonus = _rwkv_pre(x2, mods1, l1, seq, tm=512)
    yf, yb = _wkv_scan(r, v, kk, k, a0, a1, lw0, lw1, l1['k_a'], batch, seq, tb=1024)
    x2 = _rwkv_post(yf, yb, bonus, g, x2, mods1, l1['ln_w'], l1['ln_b'], bf(l1['w_o']), seq, tm)
    ya, yb_ = _conv_ffn(x2, mods1, l1['norm2'], bf(l1['ffn_up']), l1['ffn_conv_w'], l1['ffn_conv_b'],
                        bf(l1['ffn_down']), final_norm, seq, tm, final_norm=True, split_rows=xa2.shape[0])
    return ya.reshape(xa.shape), yb_.reshape(xb.shape)


def kernel(x_prompt, x_sample, c_prompt, c_sample, l0_ada_w, l0_ada_b, l0_norm1, l0_norm2, l0_w_in, l0_short_w, l0_short_b, l0_filt_w1, l0_filt_b1, l0_filt_w2, l0_filt_b2, l0_filt_w3, l0_filt_b3, l0_filt_w4, l0_filt_freq, l0_filt_bias, l0_w_out, l0_ffn_up, l0_ffn_conv_w, l0_ffn_conv_b, l0_ffn_down, l1_ada_w, l1_ada_b, l1_norm1, l1_norm2, l1_mu, l1_w_r, l1_w_k, l1_w_v, l1_w_o, l1_w0, l1_w1, l1_w2, l1_a0, l1_a1, l1_a2, l1_g1, l1_g2, l1_k_k, l1_k_a, l1_r_k, l1_ln_w, l1_ln_b, l1_ffn_up, l1_ffn_conv_w, l1_ffn_conv_b, l1_ffn_down, final_norm):
    layer0 = dict(ada_w=l0_ada_w, ada_b=l0_ada_b, norm1=l0_norm1, norm2=l0_norm2, w_in=l0_w_in,
                  short_w=l0_short_w, short_b=l0_short_b, filt_w1=l0_filt_w1, filt_b1=l0_filt_b1,
                  filt_w2=l0_filt_w2, filt_b2=l0_filt_b2, filt_w3=l0_filt_w3, filt_b3=l0_filt_b3,
                  filt_w4=l0_filt_w4, filt_freq=l0_filt_freq, filt_bias=l0_filt_bias, w_out=l0_w_out,
                  ffn_up=l0_ffn_up, ffn_conv_w=l0_ffn_conv_w, ffn_conv_b=l0_ffn_conv_b, ffn_down=l0_ffn_down)
    layer1 = dict(ada_w=l1_ada_w, ada_b=l1_ada_b, norm1=l1_norm1, norm2=l1_norm2, mu=l1_mu,
                  w_r=l1_w_r, w_k=l1_w_k, w_v=l1_w_v, w_o=l1_w_o, w0=l1_w0, w1=l1_w1, w2=l1_w2,
                  a0=l1_a0, a1=l1_a1, a2=l1_a2, g1=l1_g1, g2=l1_g2, k_k=l1_k_k, k_a=l1_k_a,
                  r_k=l1_r_k.reshape(-1), ln_w=l1_ln_w, ln_b=l1_ln_b, ffn_up=l1_ffn_up,
                  ffn_conv_w=l1_ffn_conv_w, ffn_conv_b=l1_ffn_conv_b, ffn_down=l1_ffn_down)
    c = jnp.concatenate([c_prompt, c_sample], axis=0)
    return _trunk(x_prompt, x_sample, c, layer0, layer1, final_norm)
```

```python
import functools
import math

import jax
import jax.numpy as jnp
import numpy as np
from jax import lax
from jax.experimental import pallas as pl
from jax.experimental.pallas import tpu as pltpu

F32 = jnp.float32
BF16 = jnp.bfloat16
HIGHEST = lax.Precision.HIGHEST

HEAD_DIM = 64
ATTN_PATTERNS = ((128, 1), (512, 4), (2048, 16))
N_PATTERNS = 3
HEADS_PER_GROUP = 4
N_ATTN_HEADS = 12
ATTN_WIDTH = 768
GROUP_WIDTH = HEADS_PER_GROUP * HEAD_DIM
ALIBI_MAX_EXP = 8.0
HYENA_WIDTH = 256
HYENA_BANDS = 16
HYENA_EMB = 33
HYENA_TARGET = 1e-2
HYENA_FAST_DECAY = 0.3
HYENA_SLOW_DECAY = 1.5
RWKV_HEAD = 64
RMS_EPS = 1e-6
GN_EPS = 64e-5
NEG_INF = -1e30

LANES = 128
SUBLANES = 8
MXU_DIM = 256
VMEM_LIMIT = 56 * 1024 * 1024

ATTN_RADIUS = 64
assert all(window // (2 * dilation) == ATTN_RADIUS for window, dilation in ATTN_PATTERNS)
ATTN_TQ = 128
ATTN_TK = ATTN_TQ + 2 * ATTN_RADIUS
CONV_P = 256
SCAN_C = 64
SCAN_HG = 256
SCAN_UNROLL = 4


def _cparams(sem):
    return pltpu.CompilerParams(dimension_semantics=sem, vmem_limit_bytes=VMEM_LIMIT)


def _const_spec(shape):
    nd = len(shape)
    return pl.BlockSpec(shape, lambda *_: (0,) * nd, pipeline_mode=pl.Buffered(1))


def _ada_kernel(c_ref, w_ref, b_ref, o_ref):
    c = c_ref[...]
    s = c * jax.nn.sigmoid(c)
    o_ref[...] = jnp.dot(s, w_ref[...], precision=HIGHEST, preferred_element_type=F32) + b_ref[...]


def _ada_params(c_pad, w, b):
    bp, d = c_pad.shape
    n = w.shape[1]
    tn = 1024
    return pl.pallas_call(
        _ada_kernel,
        grid=(n // tn,),
        in_specs=[_const_spec((bp, d)),
                  pl.BlockSpec((d, tn), lambda j: (0, j)),
                  pl.BlockSpec((1, tn), lambda j: (0, j))],
        out_specs=pl.BlockSpec((bp, tn), lambda j: (0, j)),
        out_shape=jax.ShapeDtypeStruct((bp, n), F32),
        compiler_params=_cparams(("parallel",)),
        name="ada_params",
    )(c_pad, w, b.reshape(1, n))


def _norm_mod(x, gain, shift, scale):
    ms = jnp.mean(x * x, axis=-1, keepdims=True)
    y = x * lax.rsqrt(ms + RMS_EPS) * gain
    return y * (1.0 + scale) + shift


def _two_source_specs(tm, d, n_first):
    return [pl.BlockSpec((tm, d), lambda i: (jnp.minimum(i, n_first - 1), 0)),
            pl.BlockSpec((tm, d), lambda i: (jnp.maximum(i - n_first, 0), 0))]


def _inproj_kernel(xa_ref, xb_ref, mod_ref, gain_ref, w_ref, q0_ref, q1_ref, q2_ref, hy_ref, scr_ref, *, n_first):
    tm = xa_ref.shape[0]
    x = jnp.where(pl.program_id(0) < n_first, xa_ref[...], xb_ref[...])
    h = _norm_mod(x, gain_ref[...], mod_ref[0, 0:1, :], mod_ref[0, 1:2, :])
    p = jnp.dot(h.astype(BF16), w_ref[...], preferred_element_type=F32)
    gw = 3 * GROUP_WIDTH
    q0_ref[...] = p[:, :gw].astype(BF16)
    for g, out_ref in ((1, q1_ref), (2, q2_ref)):
        dil = out_ref.shape[1]
        for cb in range(gw // LANES):
            cols = slice(cb * LANES, (cb + 1) * LANES)
            scr_ref[cb] = p[:, g * gw + cb * LANES:g * gw + (cb + 1) * LANES]
            for r in range(dil):
                out_ref[0, r, :, cols] = scr_ref[cb, pl.ds(r, tm // dil, stride=dil), :].astype(BF16)
    hy_ref[...] = p[:, N_PATTERNS * gw:]


def _in_proj(xa, xb, mods, gain, w_bf, seq, tm):
    d = xa.shape[1]
    rows = xa.shape[0] + xb.shape[0]
    batch = rows // seq
    n = w_bf.shape[1]
    gw = 3 * GROUP_WIDTH
    tpb = seq // tm
    dils = [dil for _, dil in ATTN_PATTERNS]
    assert dils[0] == 1

    def class_spec(dil):
        return pl.BlockSpec((1, dil, tm // dil, gw), lambda i: (i // tpb, 0, i % tpb, 0))

    return pl.pallas_call(
        functools.partial(_inproj_kernel, n_first=xa.shape[0] // tm),
        grid=(rows // tm,),
        in_specs=_two_source_specs(tm, d, xa.shape[0] // tm) + [
            pl.BlockSpec((1, 6, d), lambda i: (i // tpb, 0, 0)),
            _const_spec((1, d)),
            _const_spec((d, n))],
        out_specs=[pl.BlockSpec((tm, gw), lambda i: (i, 0)), class_spec(dils[1]), class_spec(dils[2]),
                   pl.BlockSpec((tm, n - N_PATTERNS * gw), lambda i: (i, 0))],
        out_shape=[jax.ShapeDtypeStruct((rows, gw), BF16)]
        + [jax.ShapeDtypeStruct((batch, dil, seq // dil, gw), BF16) for dil in dils[1:]]
        + [jax.ShapeDtypeStruct((rows, n - N_PATTERNS * gw), F32)],
        scratch_shapes=[pltpu.VMEM((gw // LANES, tm, LANES), F32)],
        compiler_params=_cparams(("parallel",)),
        name="in_proj",
    )(xa, xb, mods, gain.reshape(1, d), w_bf)


def _attn_kernel(q_ref, k_ref, v_ref, bias_ref, o_ref, lse_ref, kpad, vpad, *, n):
    nq = n // ATTN_TQ
    zeros = jnp.zeros((ATTN_RADIUS, GROUP_WIDTH), BF16)
    kpad[0:ATTN_RADIUS, :] = zeros
    vpad[0:ATTN_RADIUS, :] = zeros
    kpad[n + ATTN_RADIUS:n + 2 * ATTN_RADIUS, :] = zeros
    vpad[n + ATTN_RADIUS:n + 2 * ATTN_RADIUS, :] = zeros
    kpad[ATTN_RADIUS:n + ATTN_RADIUS, :] = k_ref[...]
    vpad[ATTN_RADIUS:n + ATTN_RADIUS, :] = v_ref[...]
    lane = lax.broadcasted_iota(jnp.int32, (1, LANES), 1)
    low = lane < HEAD_DIM

    blocks = 2
    assert nq % blocks == 0

    def body(it, carry):
        chains = [(blk, h) for blk in range(blocks) for h in range(HEADS_PER_GROUP)]
        idx = range(len(chains))
        i = [it * blocks + blk for blk in range(blocks)]
        r0 = [pl.multiple_of(i[blk] * ATTN_TQ, ATTN_TQ) for blk in range(blocks)]
        q = [q_ref[pl.ds(r0[blk], ATTN_TQ), :] for blk in range(blocks)]
        kw = [kpad[pl.ds(r0[blk], ATTN_TK), :] for blk in range(blocks)]
        vw = [vpad[pl.ds(r0[blk], ATTN_TK), :] for blk in range(blocks)]
        sel = [jnp.where(i[blk] == 0, 0, jnp.where(i[blk] == nq - 1, 2, 1)) for blk in range(blocks)]
        cols = [slice((h // 2) * LANES, (h // 2 + 1) * LANES) for _, h in chains]
        qm = [jnp.where(low if h % 2 == 0 else jnp.logical_not(low), q[blk][:, cols[c]], jnp.zeros((), BF16))
              for c, (blk, h) in enumerate(chains)]
        s = [lax.dot_general(qm[c], kw[chains[c][0]][:, cols[c]], (((1,), (1,)), ((), ())),
                             preferred_element_type=F32) for c in idx]
        s = [s[c] * (HEAD_DIM ** -0.5) + bias_ref[chains[c][1], sel[chains[c][0]]] for c in idx]
        m = [jnp.max(s[c], axis=-1, keepdims=True) for c in idx]
        p = [jnp.exp(s[c] - m[c]) for c in idx]
        den = [jnp.sum(p[c], axis=-1, keepdims=True) for c in idx]
        o = [jnp.dot(p[c].astype(BF16), vw[chains[c][0]][:, cols[c]], preferred_element_type=F32) for c in idx]
        o = [o[c] / den[c] for c in idx]
        lse = [jnp.broadcast_to(m[c] + jnp.log(den[c]), (ATTN_TQ, LANES)) for c in idx]
        for c in range(0, len(chains), 2):
            rows = pl.ds(r0[chains[c][0]], ATTN_TQ)
            o_ref[rows, cols[c]] = jnp.where(low, o[c], o[c + 1]).astype(o_ref.dtype)
            lse_ref[rows, cols[c]] = jnp.where(low, lse[c], lse[c + 1])
        return carry

    lax.fori_loop(0, nq // blocks, body, 0)


def _attn_bias(group, dilation):
    slopes = np.exp2(-ALIBI_MAX_EXP * (np.arange(N_ATTN_HEADS, dtype=np.float32) + 1.0) / N_ATTN_HEADS)
    slopes = slopes.reshape(N_PATTERNS, HEADS_PER_GROUP)[group].astype(np.float32)
    qi = np.arange(ATTN_TQ)[:, None]
    kj = np.arange(ATTN_TK)[None, :]
    rel = kj - ATTN_RADIUS - qi
    band = np.abs(rel) <= ATTN_RADIUS
    alibi = -slopes[:, None, None] * (np.abs(rel) * dilation).astype(np.float32)[None]
    kinds = []
    for lo, hi in ((ATTN_RADIUS, ATTN_TK), (0, ATTN_TK), (0, ATTN_TQ + ATTN_RADIUS)):
        valid = band & (kj >= lo) & (kj < hi)
        kinds.append(np.where(valid[None], alibi, np.float32(NEG_INF)))
    return jnp.asarray(np.stack(kinds, axis=1), dtype=F32)


def _attention_group(qkv, group):
    batch, dilation, n, _ = qkv.shape
    assert n % ATTN_TQ == 0 and n >= 2 * ATTN_TQ

    def part(p):
        return pl.BlockSpec((None, None, n, GROUP_WIDTH), lambda b, r: (b, r, 0, p))

    out_blk = pl.BlockSpec((None, None, n, GROUP_WIDTH), lambda b, r: (b, r, 0, 0))
    return pl.pallas_call(
        functools.partial(_attn_kernel, n=n),
        grid=(batch, dilation),
        in_specs=[part(0), part(1), part(2), _const_spec((HEADS_PER_GROUP, 3, ATTN_TQ, ATTN_TK))],
        out_specs=[out_blk, out_blk],
        out_shape=[jax.ShapeDtypeStruct((batch, dilation, n, GROUP_WIDTH), BF16),
                   jax.ShapeDtypeStruct((batch, dilation, n, GROUP_WIDTH), F32)],
        scratch_shapes=[pltpu.VMEM((n + 2 * ATTN_RADIUS, GROUP_WIDTH), BF16)] * 2,
        compiler_params=_cparams(("parallel", "parallel")),
        name=f"attn_g{group}",
    )(qkv, qkv, qkv, _attn_bias(group, dilation))


def _filter_kernel(bands_ref, deltas_ref, w1_ref, b1_ref, w2_ref, b2_ref, w3_ref, b3_ref, w4_ref, freq_ref,
                   ft_ref, *, seq, tl):
    i = pl.program_id(0)
    freq = freq_ref[...]
    lane = lax.broadcasted_iota(jnp.int32, (1, LANES), 1)
    row = lax.broadcasted_iota(jnp.int32, (tl, 1), 0) + i * tl

    def half_filter(pos, half):
        posf = pos.astype(F32)
        t = posf / float(seq - 1)
        z = bands_ref[...] * (2.0 * math.pi * posf / float(seq))
        feat = jnp.where(lane == 0, t,
                         jnp.where(lane <= HYENA_BANDS, jnp.cos(z),
                                   jnp.where(lane <= 2 * HYENA_BANDS, -jnp.sin(z), 0.0)))
        h = jnp.sin(freq * (_mm(feat, w1_ref[...]) + b1_ref[...]))
        h = jnp.sin(freq * (_mm(h, w2_ref[...]) + b2_ref[...]))
        h = jnp.sin(freq * (_mm(h, w3_ref[...]) + b3_ref[...]))
        h = _mm(h, w4_ref[:, half * HYENA_WIDTH:(half + 1) * HYENA_WIDTH])
        return h * jnp.exp(-t * jnp.abs(deltas_ref[...]))

    hf = half_filter(row, 0)
    hb = half_filter(jnp.where(row == 0, 0, seq - row), 1)
    hb = jnp.where(row == 0, 0.0, hb)

    @pl.when(i == 0)
    def _():
        ft_ref[:, 0:CONV_P] = jnp.zeros((HYENA_WIDTH, CONV_P), F32)

    c0 = pl.multiple_of(CONV_P + i * tl, LANES)
    ft_ref[:, pl.ds(c0, tl)] = hb.T
    c1 = pl.multiple_of(CONV_P + seq + i * tl, LANES)
    ft_ref[:, pl.ds(c1, tl)] = hf.T

    @pl.when(i == pl.num_programs(0) - 1)
    def _():
        full = ft_ref[...]
        norm = jnp.sum(jnp.abs(full), axis=1, keepdims=True)
        ft_ref[...] = full / norm


def _hyena_filter(seq, w1, b1, w2, b2, w3, b3, w4, freq):
    tl = 512
    f = jnp.linspace(1e-4, HYENA_BANDS - 1, HYENA_BANDS, dtype=F32)
    bands = jnp.concatenate([jnp.zeros((1,), F32), f, f, jnp.zeros((LANES - HYENA_EMB,), F32)]).reshape(1, LANES)
    max_decay = math.log(HYENA_TARGET) / HYENA_FAST_DECAY
    min_decay = math.log(HYENA_TARGET) / HYENA_SLOW_DECAY
    deltas = jnp.linspace(min_decay, max_decay, HYENA_WIDTH, dtype=F32).reshape(1, HYENA_WIDTH)
    w1p = jnp.pad(w1, ((0, LANES - HYENA_EMB), (0, 0)))
    row = lambda a: a.reshape(1, -1)
    args = (bands, deltas, w1p, row(b1), w2, row(b2), w3, row(b3), w4, row(freq))
    return pl.pallas_call(
        functools.partial(_filter_kernel, seq=seq, tl=tl),
        grid=(seq // tl,),
        in_specs=[_const_spec(a.shape) for a in args],
        out_specs=pl.BlockSpec((HYENA_WIDTH, CONV_P + 2 * seq), lambda i: (0, 0)),
        out_shape=jax.ShapeDtypeStruct((HYENA_WIDTH, CONV_P + 2 * seq), F32),
        compiler_params=_cparams(("arbitrary",)),
        name="hyena_filter",
    )(*args)


def _shift_rows(x, prev_row, next_row):
    n = x.shape[0]
    row = lax.broadcasted_iota(jnp.int32, (n, 1), 0)
    xm = jnp.where(row == 0, prev_row, pltpu.roll(x, 1, axis=0))
    xp = jnp.where(row == n - 1, next_row, pltpu.roll(x, n - 1, axis=0))
    return xm, xp


def _hyena_pre_kernel(x_ref, prev_ref, next_ref, w_ref, b_ref, zt_ref, x0t_ref):
    i = pl.program_id(1)
    x = x_ref[0]
    prev_row = jnp.where(i == 0, 0.0, prev_ref[0, SUBLANES - 1:SUBLANES, :])
    next_row = jnp.where(i == pl.num_programs(1) - 1, 0.0, next_ref[0, 0:1, :])
    xm, xp = _shift_rows(x, prev_row, next_row)
    u = xm * w_ref[0:1, :] + x * w_ref[1:2, :] + xp * w_ref[2:3, :] + b_ref[...]
    c = HYENA_WIDTH
    x0, x1, v = u[:, :c], u[:, c:2 * c], u[:, 2 * c:]
    zt_ref[0] = (v * x1).T
    x0t_ref[0] = x0.T


def _hyena_pre(hy3, short_w, short_b, tl):
    batch, seq, width = hy3.shape
    nt = seq // tl
    hb = tl // SUBLANES
    last = seq // SUBLANES - 1
    out_blk = pl.BlockSpec((1, HYENA_WIDTH, tl), lambda b, i: (b, 0, i))
    return pl.pallas_call(
        _hyena_pre_kernel,
        grid=(batch, nt),
        in_specs=[pl.BlockSpec((1, tl, width), lambda b, i: (b, i, 0)),
                  pl.BlockSpec((1, SUBLANES, width), lambda b, i: (b, jnp.maximum(i * hb - 1, 0), 0)),
                  pl.BlockSpec((1, SUBLANES, width), lambda b, i: (b, jnp.minimum((i + 1) * hb, last), 0)),
                  _const_spec((3, width)), _const_spec((1, width))],
        out_specs=[out_blk, out_blk],
        out_shape=[jax.ShapeDtypeStruct((batch, HYENA_WIDTH, seq), F32)] * 2,
        compiler_params=_cparams(("parallel", "parallel")),
        name="hyena_pre",
    )(hy3, hy3, hy3, short_w, short_b.reshape(1, width))


def _hyena_conv_kernel(bias_ref, zt_ref, x0t_ref, ft_ref, o_ref, troll, zpad, *, seq, cb):
    batch = zt_ref.shape[0]
    nb = seq // CONV_P
    p = CONV_P
    g = pl.program_id(0)
    zero_margin = jnp.zeros((batch, nb, p), F32)
    zpad[:, 0:nb, :] = zero_margin
    zpad[:, 2 * nb:3 * nb, :] = zero_margin
    chunk = 1024

    def channel(ci, carry):
        for w in range(2 * seq // chunk):
            a = w * chunk
            src = ft_ref[pl.ds(ci, 1), a:a + chunk + p]
            rolled = pltpu.roll(jnp.broadcast_to(src, (p, chunk + p)), 0, axis=1, stride=1, stride_axis=0)
            troll[:, a:a + chunk] = rolled[:, p:].astype(BF16)
        z = zt_ref[:, ci, :, :]
        zpad[:, nb:2 * nb, :] = z
        acc = jnp.zeros((batch * nb, p), F32)
        for d in range(-(nb - 1), nb):
            zs = zpad[:, nb - d:2 * nb - d, :].reshape(batch * nb, p).astype(BF16)
            t = troll[:, seq + d * p:seq + (d + 1) * p]
            acc = acc + jnp.dot(zs, t, preferred_element_type=F32)
        bias = bias_ref[g * cb + ci]
        y = (acc.reshape(batch, nb, p) + z * bias) * x0t_ref[:, ci, :, :]
        o_ref[:, ci, :, :] = y
        return carry

    lax.fori_loop(0, cb, channel, 0)


def _hyena_conv(zt, x0t, ft, filt_bias, cb):
    batch, c, seq = zt.shape
    nb = seq // CONV_P
    z4 = zt.reshape(batch, c, nb, CONV_P)
    x4 = x0t.reshape(batch, c, nb, CONV_P)
    blk = pl.BlockSpec((batch, cb, nb, CONV_P), lambda g: (0, g, 0, 0))
    out = pl.pallas_call(
        functools.partial(_hyena_conv_kernel, seq=seq, cb=cb),
        grid=(c // cb,),
        in_specs=[pl.BlockSpec(memory_space=pltpu.SMEM), blk, blk,
                  pl.BlockSpec((cb, CONV_P + 2 * seq), lambda g: (g, 0))],
        out_specs=blk,
        out_shape=jax.ShapeDtypeStruct((batch, c, nb, CONV_P), F32),
        scratch_shapes=[pltpu.VMEM((CONV_P, 2 * seq), BF16),
                        pltpu.VMEM((batch, 3 * nb, CONV_P), F32)],
        compiler_params=_cparams(("parallel",)),
        name="hyena_conv",
    )(filt_bias, z4, x4, ft)
    return out.reshape(batch, c, seq)


def _interleave_classes(blk_ref, scr_ref):
    dil, per, width = blk_ref.shape
    slabs = width // LANES
    for r in range(dil):
        rows = blk_ref[r].astype(F32)
        for cb in range(slabs):
            scr_ref[cb, pl.ds(r, per, stride=dil), :] = rows[:, cb * LANES:(cb + 1) * LANES]
    return jnp.concatenate([scr_ref[cb] for cb in range(slabs)], axis=1)


def _outproj_kernel(o0, o1, o2, l0, l1, l2, hyt_ref, xa_ref, xb_ref, mod_ref, wa_ref, wh_ref, out_ref,
                    so1, so2, sl1, sl2, *, n_first):
    ls = [l0[0], _interleave_classes(l1, sl1), _interleave_classes(l2, sl2)]
    os_ = [o0[0].astype(F32), _interleave_classes(o1, so1), _interleave_classes(o2, so2)]
    x = jnp.where(pl.program_id(0) < n_first, xa_ref[...], xb_ref[...])
    m = jnp.maximum(jnp.maximum(ls[0], ls[1]), ls[2])
    es = [jnp.exp(l - m) for l in ls]
    den = es[0] + es[1] + es[2]
    attn = (es[0] * os_[0] + es[1] * os_[1] + es[2] * os_[2]) / den
    hy = hyt_ref[0].T
    mix = jnp.dot(attn.astype(BF16), wa_ref[...], preferred_element_type=F32)
    mix = mix + jnp.dot(hy.astype(BF16), wh_ref[...], preferred_element_type=F32)
    out_ref[...] = x + mod_ref[0, 2:3, :] * mix


def _out_proj(os_, ls, hyt, xa, xb, mods, w_out_bf, seq, tm):
    d = xa.shape[1]
    rows = xa.shape[0] + xb.shape[0]
    tpb = seq // tm
    wa, wh = w_out_bf[:GROUP_WIDTH], w_out_bf[GROUP_WIDTH:]

    def class_spec(a):
        dil = a.shape[1]
        return pl.BlockSpec((None, dil, tm // dil, GROUP_WIDTH), lambda i: (i // tpb, 0, i % tpb, 0))

    return pl.pallas_call(
        functools.partial(_outproj_kernel, n_first=xa.shape[0] // tm),
        grid=(rows // tm,),
        in_specs=[class_spec(a) for a in (*os_, *ls)] + [
            pl.BlockSpec((1, HYENA_WIDTH, tm), lambda i: (i // tpb, 0, i % tpb))]
        + _two_source_specs(tm, d, xa.shape[0] // tm) + [
            pl.BlockSpec((1, 6, d), lambda i: (i // tpb, 0, 0)),
            _const_spec(wa.shape), _const_spec(wh.shape)],
        out_specs=pl.BlockSpec((tm, d), lambda i: (i, 0)),
        out_shape=jax.ShapeDtypeStruct((rows, d), F32),
        scratch_shapes=[pltpu.VMEM((GROUP_WIDTH // LANES, tm, LANES), F32)] * 4,
        compiler_params=_cparams(("parallel",)),
        name="out_proj",
    )(*os_, *ls, hyt, xa, xb, mods, wa, wh)


def _halo_rows(x_ref, prev_ref, next_ref, tiles_per_seq):
    i = pl.program_id(0)
    first = (i % tiles_per_seq) == 0
    last = (i % tiles_per_seq) == tiles_per_seq - 1
    xe = jnp.concatenate([prev_ref[...], x_ref[...], next_ref[...]], axis=0)
    return xe, first, last


def _ffn_kernel(x_ref, prev_ref, next_ref, mod_ref, gain_ref, wa_ref, wg_ref, cw_ref, cb_ref, wd_ref, fin_ref,
                *outs_and_scratch, tiles_per_seq, fc, final_norm, n_first):
    *out_refs, act_ref, a0_ref, g0_ref, a1_ref, g1_ref = outs_and_scratch
    tm = x_ref.shape[0]
    x = x_ref[...]
    xe, first, last = _halo_rows(x_ref, prev_ref, next_ref, tiles_per_seq)
    he32 = _norm_mod(xe, gain_ref[...], mod_ref[0, 3:4, :], mod_ref[0, 4:5, :])
    he = he32.astype(BF16)
    h = he32[SUBLANES:SUBLANES + tm].astype(BF16)
    row = lax.broadcasted_iota(jnp.int32, (tm + 2 * SUBLANES, 1), 0)
    lo = jnp.where(first, SUBLANES, 0)
    hi = jnp.where(last, tm + SUBLANES, tm + 2 * SUBLANES)
    keep = jnp.logical_and(row >= lo, row < hi)
    nchunk = wa_ref.shape[1] // fc
    n_ext = tm + 2 * SUBLANES

    def up(j, bufs):
        c0 = pl.multiple_of(j * fc, fc)
        bufs[0][...] = jnp.dot(he, wa_ref[:, pl.ds(c0, fc)], preferred_element_type=F32)
        bufs[1][...] = jnp.dot(h, wg_ref[:, pl.ds(c0, fc)], preferred_element_type=F32)

    def activate(j, bufs):
        c0 = pl.multiple_of(j * fc, fc)
        a = jnp.where(keep, bufs[0][...], 0.0)
        am = pltpu.roll(a, 1, axis=0)[SUBLANES:SUBLANES + tm]
        ap = pltpu.roll(a, n_ext - 1, axis=0)[SUBLANES:SUBLANES + tm]
        ac = a[SUBLANES:SUBLANES + tm]
        cw = cw_ref[:, pl.ds(c0, fc)]
        conv = am * cw[0:1] + ac * cw[1:2] + ap * cw[2:3] + cb_ref[:, pl.ds(c0, fc)]
        act_ref[:, pl.ds(c0, fc)] = (jax.nn.gelu(conv) * bufs[1][...]).astype(BF16)

    even, odd = (a0_ref, g0_ref), (a1_ref, g1_ref)
    up(0, even)

    def body(j, carry):
        up(2 * j + 1, odd)
        activate(2 * j, even)
        up(2 * j + 2, even)
        activate(2 * j + 1, odd)
        return carry

    assert nchunk % 2 == 1
    lax.fori_loop(0, nchunk // 2, body, 0)
    activate(nchunk - 1, even)
    down = jnp.dot(act_ref[...], wd_ref[...], preferred_element_type=F32)
    y = x + mod_ref[0, 5:6, :] * down
    if final_norm:
        ms = jnp.mean(y * y, axis=-1, keepdims=True)
        y = y * lax.rsqrt(ms + RMS_EPS) * fin_ref[...]
    if n_first is None:
        out_refs[0][...] = y
    else:
        i = pl.program_id(0)

        @pl.when(i < n_first)
        def _():
            out_refs[0][...] = y

        @pl.when(i >= n_first)
        def _():
            out_refs[1][...] = y


def _halo_specs(tm, d, nrows):
    hb = tm // SUBLANES
    last = nrows // SUBLANES - 1
    return [pl.BlockSpec((tm, d), lambda i: (i, 0)),
            pl.BlockSpec((SUBLANES, d), lambda i: (jnp.maximum(i * hb - 1, 0), 0)),
            pl.BlockSpec((SUBLANES, d), lambda i: (jnp.minimum((i + 1) * hb, last), 0))]


def _conv_ffn(x2, mods, gain, up_bf, conv_w, conv_b, down_bf, fin_gain, seq, tm, final_norm, split_rows=None):
    rows, d = x2.shape
    dff = down_bf.shape[0]
    fc = 256
    tpb = seq // tm
    wa, wg = up_bf[:, :dff], up_bf[:, dff:]
    if split_rows is None:
        n_first = None
        out_specs = pl.BlockSpec((tm, d), lambda i: (i, 0))
        out_shape = jax.ShapeDtypeStruct((rows, d), F32)
    else:
        n_first = split_rows // tm
        out_specs = _two_source_specs(tm, d, n_first)
        out_shape = [jax.ShapeDtypeStruct((split_rows, d), F32), jax.ShapeDtypeStruct((rows - split_rows, d), F32)]
    return pl.pallas_call(
        functools.partial(_ffn_kernel, tiles_per_seq=tpb, fc=fc, final_norm=final_norm, n_first=n_first),
        grid=(rows // tm,),
        in_specs=_halo_specs(tm, d, rows) + [
            pl.BlockSpec((1, 6, d), lambda i: (i // tpb, 0, 0)),
            _const_spec((1, d)), _const_spec(wa.shape), _const_spec(wg.shape),
            _const_spec((3, dff)), _const_spec((1, dff)), _const_spec(down_bf.shape), _const_spec((1, d))],
        out_specs=out_specs,
        out_shape=out_shape,
        scratch_shapes=[pltpu.VMEM((tm, dff), BF16)]
        + [pltpu.VMEM((tm + 2 * SUBLANES, fc), F32), pltpu.VMEM((tm, fc), F32)] * 2,
        compiler_params=_cparams(("arbitrary",)),
        name="conv_ffn",
    )(x2, x2, x2, mods, gain.reshape(1, d), wa, wg, conv_w, conv_b.reshape(1, dff), down_bf,
      fin_gain.reshape(1, d))


def _head_sum(x, ones_bd):
    xb = x.astype(BF16)
    parts = [jnp.dot(xb[:, c * MXU_DIM:(c + 1) * MXU_DIM], ones_bd, preferred_element_type=F32)
             for c in range(x.shape[1] // MXU_DIM)]
    return jnp.concatenate(parts, axis=1)


def _sigmoid(x):
    return 0.5 * jnp.tanh(0.5 * x) + 0.5


def _ones_blockdiag():
    r = lax.broadcasted_iota(jnp.int32, (MXU_DIM, MXU_DIM), 0) // RWKV_HEAD
    c = lax.broadcasted_iota(jnp.int32, (MXU_DIM, MXU_DIM), 1) // RWKV_HEAD
    return jnp.where(r == c, 1.0, 0.0).astype(BF16)


def _rwkv_pre_kernel(x_ref, prev_ref, next_ref, mod_ref, gain_ref, mu_ref, wr_ref, wk_ref, wv_ref,
                     w1_ref, w2_ref, w0_ref, a1_ref, a2_ref, a0_ref, g1_ref, g2_ref, kk_w_ref, ka_ref, rk_ref,
                     r_out, v_out, kk_out, k_out, a0_out, a1_out, lw0_out, lw1_out, g_out, bonus_out, he_ref, hb_ref, xx_ref,
                     *, tiles_per_seq):
    tm, d = x_ref.shape
    i = pl.program_id(0)
    first = (i % tiles_per_seq) == 0
    last = (i % tiles_per_seq) == tiles_per_seq - 1
    norm = lambda x: _norm_mod(x, gain_ref[...], mod_ref[0, 0:1, :], mod_ref[0, 1:2, :])
    ones_bd = _ones_blockdiag()
    he_ref[0:SUBLANES, :] = jnp.where(first, 0.0, norm(prev_ref[...]))
    he_ref[SUBLANES + tm:, :] = jnp.where(last, 0.0, norm(next_ref[...]))

    def row_part(lo, n):
        rows = slice(lo, lo + n)
        mid = slice(SUBLANES + lo, SUBLANES + lo + n)
        h = norm(x_ref[rows, :])
        he_ref[mid, :] = h
        hb_ref[rows, :] = h.astype(BF16)
        yield
        hm = he_ref[SUBLANES + lo - 1:SUBLANES + lo - 1 + n, :]
        hp = he_ref[SUBLANES + lo + 1:SUBLANES + lo + 1 + n, :]
        xx_ref[rows, :] = (0.5 * (hm + hp) - he_ref[mid, :]).astype(BF16)
        bf16_rows = 2 * SUBLANES

        def mixed(i):
            mu_tile = jnp.broadcast_to(mu_ref[i:i + 1, :], (bf16_rows, d)).astype(BF16)
            return hb_ref[rows, :] + xx_ref[rows, :] * jnp.tile(mu_tile, (n // bf16_rows, 1))

        r = jnp.dot(mixed(0), wr_ref[...], preferred_element_type=F32)
        r_out[rows, :] = r.astype(BF16)
        yield
        k = jnp.dot(mixed(2), wk_ref[...], preferred_element_type=F32)
        k_out[rows, :] = k.astype(BF16)
        kk = k * kk_w_ref[...]
        kk_out[rows, :] = (kk * jnp.minimum(lax.rsqrt(_head_sum(kk * kk, ones_bd)), 1e12)).astype(BF16)
        yield
        v = jnp.dot(mixed(3), wv_ref[...], preferred_element_type=F32)
        v_out[rows, :] = v.astype(BF16)
        yield
        gl = _sigmoid(jnp.dot(mixed(5), g1_ref[...], preferred_element_type=F32))
        g_out[rows, :] = jnp.dot(gl.astype(BF16), g2_ref[...], preferred_element_type=F32)
        yield
        wl = jnp.tanh(jnp.dot(mixed(1), w1_ref[...], preferred_element_type=F32))
        wl = jnp.dot(wl.astype(BF16), w2_ref[...], preferred_element_type=F32)
        for direction, lw_o in enumerate((lw0_out, lw1_out)):
            u = w0_ref[direction:direction + 1, :] + wl[:, direction * d:(direction + 1) * d]
            lw_o[rows, :] = -_sigmoid(u) * math.exp(-0.5)
        yield
        al = jnp.dot(mixed(4), a1_ref[...], preferred_element_type=F32)
        al = jnp.dot(al.astype(BF16), a2_ref[...], preferred_element_type=F32)
        ksum = jnp.zeros_like(k)
        for direction, a_o in enumerate((a0_out, a1_out)):
            a = _sigmoid(a0_ref[direction:direction + 1, :] + al[:, direction * d:(direction + 1) * d])
            a_o[rows, :] = a.astype(BF16)
            ksum = ksum + _k_dir(k, a, ka_ref[...])
        bonus_out[rows, :] = _head_sum(r * (0.5 * ksum) * rk_ref[...], ones_bd) * v

    parts = 2
    _emit_interleaved(*[row_part(p * (tm // parts), tm // parts) for p in range(parts)])


def _k_dir(k, a, k_a):
    return k * (1.0 + (a - 1.0) * k_a)


def _blockdiag2(m):
    z = jnp.zeros_like(m[0])
    return jnp.concatenate([jnp.concatenate([m[0], z], axis=1), jnp.concatenate([z, m[1]], axis=1)], axis=0)


def _rwkv_pre(x2, mods, p, seq, tm):
    rows, d = x2.shape
    tpb = seq // tm
    bf = lambda a: a.astype(BF16)
    w1 = bf(jnp.concatenate([p['w1'][0], p['w1'][1]], axis=1))
    w2 = bf(_blockdiag2(p['w2']))
    a1 = bf(jnp.concatenate([p['a1'][0], p['a1'][1]], axis=1))
    a2 = bf(_blockdiag2(p['a2']))
    glora = p['g1'].shape[1]
    gpad = -glora % LANES
    g1 = bf(jnp.pad(p['g1'], ((0, 0), (0, gpad))))
    g2 = bf(jnp.pad(p['g2'], ((0, gpad), (0, 0))))
    row = lambda a: a.reshape(1, d)
    args = (x2, x2, x2, mods, row(p['norm1']), p['mu'], bf(p['w_r']), bf(p['w_k']), bf(p['w_v']),
            w1, w2, p['w0'], a1, a2, p['a0'], g1, g2, row(p['k_k']), row(p['k_a']), row(p['r_k']))
    in_specs = _halo_specs(tm, d, rows) + [pl.BlockSpec((1, 6, d), lambda i: (i // tpb, 0, 0))]
    in_specs += [_const_spec(a.shape) for a in args[4:]]
    ob = pl.BlockSpec((tm, d), lambda i: (i, 0))
    return pl.pallas_call(
        functools.partial(_rwkv_pre_kernel, tiles_per_seq=tpb),
        grid=(rows // tm,),
        in_specs=in_specs,
        out_specs=[ob] * 10,
        out_shape=[jax.ShapeDtypeStruct((rows, d), BF16)] * 6 + [jax.ShapeDtypeStruct((rows, d), F32)] * 4,
        scratch_shapes=[pltpu.VMEM((tm + 2 * SUBLANES, d), F32), pltpu.VMEM((tm, d), BF16),
                        pltpu.VMEM((tm, d), BF16)],
        compiler_params=_cparams(("parallel",)),
        name="rwkv_pre",
    )(*args)


def _scan_masks(reverse):
    c = SCAN_C
    assert c == RWKV_HEAD
    t = lax.broadcasted_iota(jnp.int32, (c, 4 * c), 0)
    s = lax.broadcasted_iota(jnp.int32, (c, 4 * c), 1) % c
    before = (s > t) if reverse else (s < t)
    upto = jnp.logical_or(before, s == t)
    eye = jnp.where(s == t, 1.0, 0.0)
    rr = lax.broadcasted_iota(jnp.int32, (4 * c, SCAN_HG), 0)
    cc = lax.broadcasted_iota(jnp.int32, (4 * c, SCAN_HG), 1)
    same_head = jnp.where((rr // c) == (cc // RWKV_HEAD), 1.0, 0.0)
    return before, upto, eye, same_head, same_head.astype(BF16)


def _bd(xb, head_mask_bf):
    return jnp.concatenate([xb] * 4, axis=0) * head_mask_bf


def _bd_t(x, head_mask_bf):
    xt = jnp.concatenate([x, x], axis=0).T.astype(BF16)
    return jnp.concatenate([xt, xt], axis=1) * head_mask_bf


def _mm(a, b):
    return jnp.dot(a, b, precision=HIGHEST, preferred_element_type=F32)


def _bdot(a, b):
    return jnp.dot(a, b, preferred_element_type=F32)


def _cumsum_rows(tri_bf, x):
    hi = x.astype(BF16)
    lo = (x - hi.astype(F32)).astype(BF16)
    return _bdot(tri_bf, hi) + _bdot(tri_bf, lo)


def _scan_prologue(chains):
    c = SCAN_C
    bf = lambda x: x.astype(BF16)
    idx = range(len(chains))
    r, v, kk, lw, k, a = [[ch[0][j] for ch in chains] for j in range(6)]
    head_mask_bf = chains[0][1][5]
    rev = [ch[2] for ch in chains]
    g = [_cumsum_rows(chains[i][1][0], lw[i]) for i in idx]
    gtot = [g[i][0:1] if rev[i] else g[i][c - 1:c] for i in idx]
    kkd = [kk[i] * jnp.exp(g[i] - lw[i]) for i in idx]
    rd = [r[i] * jnp.exp(g[i]) for i in idx]
    b = [kk[i] * a[i] for i in idx]
    ginv = [jnp.exp(-g[i]) for i in idx]
    lhs = [bf(jnp.concatenate([kkd[i], rd[i]], axis=0)) for i in idx]
    bi_t = [_bd_t(b[i] * ginv[i], head_mask_bf) for i in idx]
    ki_t = [_bd_t(k[i] * ginv[i], head_mask_bf) for i in idx]
    return chains, (v, k, g, gtot, kkd, rd, b, lhs, bi_t, ki_t)


def _scan_prepare_stages(prologue, out):
    c = SCAN_C
    bf = lambda x: x.astype(BF16)
    chains, (v, k, g, gtot, kkd, rd, b, lhs, bi_t, ki_t) = prologue
    idx = range(len(chains))
    before, upto, eye = [[ch[1][j] for ch in chains] for j in range(1, 4)]
    head_mask_bf = chains[0][1][5]
    rev = [ch[2] for ch in chains]
    bd = lambda x: _bd(bf(x), head_mask_bf)

    a1 = [_bdot(lhs[i], bi_t[i]) for i in idx]
    a2 = [_bdot(lhs[i], ki_t[i]) for i in idx]
    n_ab = [jnp.where(before[i], a1[i][:c], 0.0) for i in idx]
    a_rb = [jnp.where(upto[i], a1[i][c:], 0.0) for i in idx]
    a_k = [jnp.concatenate([jnp.where(before[i], a2[i][:c], 0.0), jnp.where(upto[i], a2[i][c:], 0.0)], axis=0)
           for i in idx]
    yield

    nb = [bf(n_ab[i]) for i in idx]
    npow = [_bdot(nb[i], _bd(nb[i], head_mask_bf)) for i in idx]
    tinv = [eye[i] - n_ab[i] for i in idx]
    yield
    steps = int(math.log2(c)) - 1
    for step in range(steps):
        pb = [bf(npow[i]) for i in idx]
        rhs = [_bd(pb[i], head_mask_bf) for i in idx]
        if step == steps - 1:
            tinv = [tinv[i] + _bdot(bf(tinv[i]), rhs[i]) for i in idx]
        else:
            prod = [_bdot(jnp.concatenate([pb[i], bf(tinv[i])], axis=0), rhs[i]) for i in idx]
            tinv = [tinv[i] + prod[i][c:] for i in idx]
            npow = [prod[i][:c] for i in idx]
        yield

    vb = [bf(v[i]) for i in idx]
    av = [_bdot(bf(a_k[i]), _bd(vb[i], head_mask_bf)) for i in idx]
    yield
    tw = [_bdot(bf(tinv[i]), jnp.concatenate([bd(kkd[i]), bd(av[i][:c])], axis=1)) for i in idx]
    lhs_state = [bf(jnp.concatenate([rd[i], tw[i][:, :SCAN_HG]], axis=0)) for i in idx]
    wt = [tw[i][:, SCAN_HG:] for i in idx]
    yield
    gend = [jnp.exp(gtot[i] - g[i]) for i in idx]
    upd_t = [bf(jnp.concatenate([b[i] * gend[i], k[i] * gend[i]], axis=0).T) for i in idx]
    gcol_src = [jnp.concatenate([g[i], g[i]], axis=0).T for i in idx]
    gcol = [jnp.exp(gcol_src[i][:, 0:1] if rev[i] else gcol_src[i][:, c - 1:c]) for i in idx]
    out.extend((lhs_state[i], wt[i], bf(a_rb[i]), av[i][c:], upd_t[i], gcol[i], vb[i]) for i in idx)


def _scan_apply_stages(preps, states, head_mask, head_mask_bf, emit_y):
    c = SCAN_C
    for pos in range(0, len(preps), 2):
        lhs, wt, a_rb, av_r, upd_t, gcol, vb = zip(*preps[pos:pos + 2])
        o = [_bdot(lhs[i], states[i].astype(BF16)) for i in range(2)]
        yield
        ub = [(-(wt[i] + o[i][c:])).astype(BF16) for i in range(2)]
        upd = [_bdot(upd_t[i], jnp.concatenate([ub[i], vb[i]], axis=0)) for i in range(2)]
        y = [o[i][:c] + av_r[i] + _bdot(a_rb[i], _bd(ub[i], head_mask_bf)) for i in range(2)]
        yield
        for i in range(2):
            emit_y(pos + i, y[i])
            states[i] = states[i] * gcol[i] + head_mask * upd[i]


def _emit_interleaved(*stage_generators):
    pending = list(stage_generators)
    while pending:
        for gen in list(pending):
            try:
                next(gen)
            except StopIteration:
                pending.remove(gen)


def _scan_consts(reverse):
    c = SCAN_C
    before, upto, eye, head_mask, head_mask_bf = _scan_masks(reverse)
    tt = lax.broadcasted_iota(jnp.int32, (c, c), 0)
    ss = lax.broadcasted_iota(jnp.int32, (c, c), 1)
    tri = jnp.where((ss >= tt) if reverse else (ss <= tt), 1.0, 0.0).astype(BF16)
    return tri, before, upto, eye, head_mask, head_mask_bf


def _scan_kernel(ka_ref, rf_ref, vf_ref, kkf_ref, kf_ref, af_ref, lwf_ref, rb_ref, vb_ref, kkb_ref, kb_ref, ab_ref,
                 lwb_ref, yf_ref, yb_ref, sf_ref, sb_ref):
    tb = rf_ref.shape[0]
    nchunk = tb // SCAN_C

    @pl.when(pl.program_id(2) == 0)
    def _():
        sf_ref[...] = jnp.zeros_like(sf_ref)
        sb_ref[...] = jnp.zeros_like(sb_ref)

    consts_f = _scan_consts(False)
    consts_b = _scan_consts(True)
    head_mask = consts_f[4]
    k_a = ka_ref[...]

    def operands(refs, rows):
        r, v, kk, k, a = (ref[rows, :].astype(F32) for ref in refs[:5])
        return r, v, kk, refs[5][rows, :], _k_dir(k, a, k_a), a

    fwd_refs = (rf_ref, vf_ref, kkf_ref, kf_ref, af_ref, lwf_ref)
    bwd_refs = (rb_ref, vb_ref, kkb_ref, kb_ref, ab_ref, lwb_ref)

    def emitter(rows):
        def emit_y(pos, y):
            ref = yb_ref if pos % 2 else yf_ref
            ref[rows[pos], :] = y.astype(ref.dtype)
        return emit_y

    def group_rows(group):
        rows = []
        for u in range(SCAN_UNROLL):
            j = group * SCAN_UNROLL + u
            rows += [pl.ds(j * SCAN_C, SCAN_C), pl.ds((nchunk - 1 - j) * SCAN_C, SCAN_C)]
        return rows

    def group_prologue(rows):
        chains = []
        for pos, rs in enumerate(rows):
            reverse = pos % 2 == 1
            chains.append((operands(bwd_refs if reverse else fwd_refs, rs), consts_b if reverse else consts_f,
                           reverse))
        return _scan_prologue(chains)

    def run_once(fn, box):
        box.append(fn())
        yield

    ngroups = nchunk // SCAN_UNROLL
    states = [sf_ref[...], sb_ref[...]]
    pending = None
    rows = group_rows(0)
    prologue = group_prologue(rows)
    for group in range(ngroups):
        preps, next_box = [], []
        stages = [_scan_prepare_stages(prologue, preps)]
        if pending is not None:
            stages.append(_scan_apply_stages(pending[0], states, head_mask, consts_f[5], emitter(pending[1])))
        if group + 1 < ngroups:
            next_rows = group_rows(group + 1)
            stages.append(run_once(functools.partial(group_prologue, next_rows), next_box))
        _emit_interleaved(*stages)
        pending = (preps, rows)
        if group + 1 < ngroups:
            rows, prologue = next_rows, next_box[0]
    _emit_interleaved(_scan_apply_stages(pending[0], states, head_mask, consts_f[5], emitter(pending[1])))
    sf_ref[...] = states[0]
    sb_ref[...] = states[1]


def _wkv_scan(r, v, kk, k, a0, a1, lw0, lw1, k_a, batch, seq, tb):
    rows, d = r.shape
    nt = seq // tb
    ng = d // SCAN_HG
    fwd = pl.BlockSpec((tb, SCAN_HG), lambda b, g, t: (b * nt + t, g))
    bwd = pl.BlockSpec((tb, SCAN_HG), lambda b, g, t: (b * nt + nt - 1 - t, g))
    return pl.pallas_call(
        _scan_kernel,
        grid=(batch, ng, nt),
        in_specs=[pl.BlockSpec((1, SCAN_HG), lambda b, g, t: (0, g))] + [fwd] * 6 + [bwd] * 6,
        out_specs=[fwd, bwd],
        out_shape=[jax.ShapeDtypeStruct((rows, d), BF16)] * 2,
        scratch_shapes=[pltpu.VMEM((SCAN_HG, SCAN_HG), F32)] * 2,
        compiler_params=_cparams(("parallel", "parallel", "arbitrary")),
        name="wkv_scan",
    )(k_a.reshape(1, d), r, v, kk, k, a0, lw0, r, v, kk, k, a1, lw1)


def _rwkv_post_kernel(yf_ref, yb_ref, bonus_ref, g_ref, x_ref, mod_ref, lnw_ref, lnb_ref, wo_ref, out_ref):
    y = yf_ref[...].astype(F32) + yb_ref[...].astype(F32)
    ones_bd = _ones_blockdiag()
    mu = _head_sum(y, ones_bd) * (1.0 / RWKV_HEAD)
    yc = y - mu
    var = _head_sum(yc * yc, ones_bd) * (1.0 / RWKV_HEAD)
    yn = yc * lax.rsqrt(var + GN_EPS) * lnw_ref[...] + lnb_ref[...]
    out = ((yn + bonus_ref[...]) * g_ref[...]).astype(BF16)
    mix = jnp.dot(out, wo_ref[...], preferred_element_type=F32)
    out_ref[...] = x_ref[...] + mod_ref[0, 2:3, :] * mix


def _rwkv_post(yf, yb, bonus, g, x2, mods, ln_w, ln_b, wo_bf, seq, tm):
    rows, d = x2.shape
    tpb = seq // tm
    rb = pl.BlockSpec((tm, d), lambda i: (i, 0))
    return pl.pallas_call(
        _rwkv_post_kernel,
        grid=(rows // tm,),
        in_specs=[rb] * 5 + [pl.BlockSpec((1, 6, d), lambda i: (i // tpb, 0, 0)),
                             _const_spec((1, d)), _const_spec((1, d)), _const_spec((d, d))],
        out_specs=rb,
        out_shape=jax.ShapeDtypeStruct((rows, d), F32),
        compiler_params=_cparams(("parallel",)),
        name="rwkv_post",
    )(yf, yb, bonus, g, x2, mods, ln_w.reshape(1, d), ln_b.reshape(1, d), wo_bf)


def _mods(c, w, b):
    batch, d = c.shape
    pad = -batch % SUBLANES
    m = _ada_params(jnp.pad(c, ((0, pad), (0, 0))), w, b)[:batch]
    return m.reshape(batch, 6, d)


def _group_major_w_in(w_in):
    d = w_in.shape[0]
    qkv = w_in[:, :3 * ATTN_WIDTH].reshape(d, 3, N_PATTERNS, GROUP_WIDTH)
    qkv = jnp.transpose(qkv, (0, 2, 1, 3)).reshape(d, 3 * ATTN_WIDTH)
    return jnp.concatenate([qkv, w_in[:, 3 * ATTN_WIDTH:]], axis=1)


def _trunk(xa, xb, c, l0, l1, final_norm):
    seq, d = xa.shape[1:]
    assert xb.shape[1:] == (seq, d)
    batch = xa.shape[0] + xb.shape[0]
    xa2, xb2 = xa.reshape(-1, d), xb.reshape(-1, d)
    bf = lambda a: a.astype(BF16)
    tm = 512

    mods0 = _mods(c, l0['ada_w'], l0['ada_b'])
    q0, q1, q2, hy = _in_proj(xa2, xb2, mods0, l0['norm1'], bf(_group_major_w_in(l0['w_in'])), seq, tm)
    os_, ls = [], []
    for group, qkv in enumerate((q0.reshape(batch, 1, seq, 3 * GROUP_WIDTH), q1, q2)):
        o, lse = _attention_group(qkv, group)
        os_.append(o)
        ls.append(lse)
    ft = _hyena_filter(seq, l0['filt_w1'], l0['filt_b1'], l0['filt_w2'], l0['filt_b2'], l0['filt_w3'],
                       l0['filt_b3'], l0['filt_w4'], l0['filt_freq'])
    zt, x0t = _hyena_pre(hy.reshape(batch, seq, 3 * HYENA_WIDTH), l0['short_w'], l0['short_b'], tl=512)
    hyt = _hyena_conv(zt, x0t, ft, l0['filt_bias'], cb=8)
    x2 = _out_proj(os_, ls, hyt, xa2, xb2, mods0, bf(l0['w_out']), seq, tm)
    x2 = _conv_ffn(x2, mods0, l0['norm2'], bf(l0['ffn_up']), l0['ffn_conv_w'], l0['ffn_conv_b'],
                   bf(l0['ffn_down']), final_norm, seq, tm, final_norm=False)

    mods1 = _mods(c, l1['ada_w'], l1['ada_b'])
    r, v, kk, k, a0, a1, lw0, lw1, g, bonus = _rwkv_pre(x2, mods1, l1, seq, tm=512)
    yf, yb = _wkv_scan(r, v, kk, k, a0, a1, lw0, lw1, l1['k_a'], batch, seq, tb=1024)
    x2 = _rwkv_post(yf, yb, bonus, g, x2, mods1, l1['ln_w'], l1['ln_b'], bf(l1['w_o']), seq, tm)
    ya, yb_ = _conv_ffn(x2, mods1, l1['norm2'], bf(l1['ffn_up']), l1['ffn_conv_w'], l1['ffn_conv_b'],
                        bf(l1['ffn_down']), final_norm, seq, tm, final_norm=True, split_rows=xa2.shape[0])
    return ya.reshape(xa.shape), yb_.reshape(xb.shape)


def kernel(x_prompt, x_sample, c_prompt, c_sample, l0_ada_w, l0_ada_b, l0_norm1, l0_norm2, l0_w_in, l0_short_w, l0_short_b, l0_filt_w1, l0_filt_b1, l0_filt_w2, l0_filt_b2, l0_filt_w3, l0_filt_b3, l0_filt_w4, l0_filt_freq, l0_filt_bias, l0_w_out, l0_ffn_up, l0_ffn_conv_w, l0_ffn_conv_b, l0_ffn_down, l1_ada_w, l1_ada_b, l1_norm1, l1_norm2, l1_mu, l1_w_r, l1_w_k, l1_w_v, l1_w_o, l1_w0, l1_w1, l1_w2, l1_a0, l1_a1, l1_a2, l1_g1, l1_g2, l1_k_k, l1_k_a, l1_r_k, l1_ln_w, l1_ln_b, l1_ffn_up, l1_ffn_conv_w, l1_ffn_conv_b, l1_ffn_down, final_norm):
    layer0 = dict(ada_w=l0_ada_w, ada_b=l0_ada_b, norm1=l0_norm1, norm2=l0_norm2, w_in=l0_w_in,
                  short_w=l0_short_w, short_b=l0_short_b, filt_w1=l0_filt_w1, filt_b1=l0_filt_b1,
                  filt_w2=l0_filt_w2, filt_b2=l0_filt_b2, filt_w3=l0_filt_w3, filt_b3=l0_filt_b3,
                  filt_w4=l0_filt_w4, filt_freq=l0_filt_freq, filt_bias=l0_filt_bias, w_out=l0_w_out,
                  ffn_up=l0_ffn_up, ffn_conv_w=l0_ffn_conv_w, ffn_conv_b=l0_ffn_conv_b, ffn_down=l0_ffn_down)
    layer1 = dict(ada_w=l1_ada_w, ada_b=l1_ada_b, norm1=l1_norm1, norm2=l1_norm2, mu=l1_mu,
                  w_r=l1_w_r, w_k=l1_w_k, w_v=l1_w_v, w_o=l1_w_o, w0=l1_w0, w1=l1_w1, w2=l1_w2,
                  a0=l1_a0, a1=l1_a1, a2=l1_a2, g1=l1_g1, g2=l1_g2, k_k=l1_k_k, k_a=l1_k_a,
                  r_k=l1_r_k.reshape(-1), ln_w=l1_ln_w, ln_b=l1_ln_b, ffn_up=l1_ffn_up,
                  ffn_conv_w=l1_ffn_conv_w, ffn_conv_b=l1_ffn_conv_b, ffn_down=l1_ffn_down)
    c = jnp.concatenate([c_prompt, c_sample], axis=0)
    return _trunk(x_prompt, x_sample, c, layer0, layer1, final_norm)
```

```python
import functools
import math

import jax
import jax.numpy as jnp
import numpy as np
from jax import lax
from jax.experimental import pallas as pl
from jax.experimental.pallas import tpu as pltpu

F32 = jnp.float32
BF16 = jnp.bfloat16
HIGHEST = lax.Precision.HIGHEST

HEAD_DIM = 64
ATTN_PATTERNS = ((128, 1), (512, 4), (2048, 16))
N_PATTERNS = 3
HEADS_PER_GROUP = 4
N_ATTN_HEADS = 12
ATTN_WIDTH = 768
GROUP_WIDTH = HEADS_PER_GROUP * HEAD_DIM
ALIBI_MAX_EXP = 8.0
HYENA_WIDTH = 256
HYENA_BANDS = 16
HYENA_EMB = 33
HYENA_TARGET = 1e-2
HYENA_FAST_DECAY = 0.3
HYENA_SLOW_DECAY = 1.5
RWKV_HEAD = 64
RMS_EPS = 1e-6
GN_EPS = 64e-5
NEG_INF = -1e30

LANES = 128
SUBLANES = 8
MXU_DIM = 256
VMEM_LIMIT = 56 * 1024 * 1024

ATTN_RADIUS = 64
assert all(window // (2 * dilation) == ATTN_RADIUS for window, dilation in ATTN_PATTERNS)
ATTN_TQ = 128
ATTN_TK = ATTN_TQ + 2 * ATTN_RADIUS
CONV_P = 256
SCAN_C = 64
SCAN_HG = 256
SCAN_UNROLL = 4


def _cparams(sem):
    return pltpu.CompilerParams(dimension_semantics=sem, vmem_limit_bytes=VMEM_LIMIT)


def _const_spec(shape):
    nd = len(shape)
    return pl.BlockSpec(shape, lambda *_: (0,) * nd, pipeline_mode=pl.Buffered(1))


def _ada_kernel(c_ref, w_ref, b_ref, o_ref):
    c = c_ref[...]
    s = c * jax.nn.sigmoid(c)
    o_ref[...] = jnp.dot(s, w_ref[...], precision=HIGHEST, preferred_element_type=F32) + b_ref[...]


def _ada_params(c_pad, w, b):
    bp, d = c_pad.shape
    n = w.shape[1]
    tn = 1024
    return pl.pallas_call(
        _ada_kernel,
        grid=(n // tn,),
        in_specs=[_const_spec((bp, d)),
                  pl.BlockSpec((d, tn), lambda j: (0, j)),
                  pl.BlockSpec((1, tn), lambda j: (0, j))],
        out_specs=pl.BlockSpec((bp, tn), lambda j: (0, j)),
        out_shape=jax.ShapeDtypeStruct((bp, n), F32),
        compiler_params=_cparams(("parallel",)),
        name="ada_params",
    )(c_pad, w, b.reshape(1, n))


def _norm_mod(x, gain, shift, scale):
    ms = jnp.mean(x * x, axis=-1, keepdims=True)
    y = x * lax.rsqrt(ms + RMS_EPS) * gain
    return y * (1.0 + scale) + shift


def _two_source_specs(tm, d, n_first):
    return [pl.BlockSpec((tm, d), lambda i: (jnp.minimum(i, n_first - 1), 0)),
            pl.BlockSpec((tm, d), lambda i: (jnp.maximum(i - n_first, 0), 0))]


def _inproj_kernel(xa_ref, xb_ref, mod_ref, gain_ref, w_ref, q0_ref, q1_ref, q2_ref, hy_ref, scr_ref, *, n_first):
    tm = xa_ref.shape[0]
    gw = 3 * GROUP_WIDTH
    from_first = pl.program_id(0) < n_first
    halves = [slice(p * (tm // 2), (p + 1) * (tm // 2)) for p in range(2)]
    hs = [_norm_mod(jnp.where(from_first, xa_ref[rows, :], xb_ref[rows, :]), gain_ref[...],
                    mod_ref[0, 0:1, :], mod_ref[0, 1:2, :]).astype(BF16) for rows in halves]
    ps = [jnp.dot(h, w_ref[...], preferred_element_type=F32) for h in hs]
    for rows, p in zip(halves, ps):
        q0_ref[rows, :] = p[:, :gw].astype(BF16)
        hy_ref[rows, :] = p[:, N_PATTERNS * gw:]
    for g, out_ref in ((1, q1_ref), (2, q2_ref)):
        dil = out_ref.shape[1]
        for cb in range(gw // LANES):
            cols = slice(cb * LANES, (cb + 1) * LANES)
            for rows, p in zip(halves, ps):
                scr_ref[cb, rows, :] = p[:, g * gw + cb * LANES:g * gw + (cb + 1) * LANES]
            for r in range(dil):
                out_ref[0, r, :, cols] = scr_ref[cb, pl.ds(r, tm // dil, stride=dil), :].astype(BF16)


def _in_proj(xa, xb, mods, gain, w_bf, seq, tm):
    d = xa.shape[1]
    rows = xa.shape[0] + xb.shape[0]
    batch = rows // seq
    n = w_bf.shape[1]
    gw = 3 * GROUP_WIDTH
    tpb = seq // tm
    dils = [dil for _, dil in ATTN_PATTERNS]
    assert dils[0] == 1

    def class_spec(dil):
        return pl.BlockSpec((1, dil, tm // dil, gw), lambda i: (i // tpb, 0, i % tpb, 0))

    return pl.pallas_call(
        functools.partial(_inproj_kernel, n_first=xa.shape[0] // tm),
        grid=(rows // tm,),
        in_specs=_two_source_specs(tm, d, xa.shape[0] // tm) + [
            pl.BlockSpec((1, 6, d), lambda i: (i // tpb, 0, 0)),
            _const_spec((1, d)),
            _const_spec((d, n))],
        out_specs=[pl.BlockSpec((tm, gw), lambda i: (i, 0)), class_spec(dils[1]), class_spec(dils[2]),
                   pl.BlockSpec((tm, n - N_PATTERNS * gw), lambda i: (i, 0))],
        out_shape=[jax.ShapeDtypeStruct((rows, gw), BF16)]
        + [jax.ShapeDtypeStruct((batch, dil, seq // dil, gw), BF16) for dil in dils[1:]]
        + [jax.ShapeDtypeStruct((rows, n - N_PATTERNS * gw), F32)],
        scratch_shapes=[pltpu.VMEM((gw // LANES, tm, LANES), F32)],
        compiler_params=_cparams(("parallel",)),
        name="in_proj",
    )(xa, xb, mods, gain.reshape(1, d), w_bf)


def _attn_kernel(q_ref, k_ref, v_ref, bias_ref, o_ref, lse_ref, kpad, vpad, *, n):
    nq = n // ATTN_TQ
    zeros = jnp.zeros((ATTN_RADIUS, GROUP_WIDTH), BF16)
    kpad[0:ATTN_RADIUS, :] = zeros
    vpad[0:ATTN_RADIUS, :] = zeros
    kpad[n + ATTN_RADIUS:n + 2 * ATTN_RADIUS, :] = zeros
    vpad[n + ATTN_RADIUS:n + 2 * ATTN_RADIUS, :] = zeros
    kpad[ATTN_RADIUS:n + ATTN_RADIUS, :] = k_ref[...]
    vpad[ATTN_RADIUS:n + ATTN_RADIUS, :] = v_ref[...]
    lane = lax.broadcasted_iota(jnp.int32, (1, LANES), 1)
    low = lane < HEAD_DIM

    blocks = 2
    assert nq % blocks == 0

    def body(it, carry):
        chains = [(blk, h) for blk in range(blocks) for h in range(HEADS_PER_GROUP)]
        idx = range(len(chains))
        i = [it * blocks + blk for blk in range(blocks)]
        r0 = [pl.multiple_of(i[blk] * ATTN_TQ, ATTN_TQ) for blk in range(blocks)]
        q = [q_ref[pl.ds(r0[blk], ATTN_TQ), :] for blk in range(blocks)]
        kw = [kpad[pl.ds(r0[blk], ATTN_TK), :] for blk in range(blocks)]
        vw = [vpad[pl.ds(r0[blk], ATTN_TK), :] for blk in range(blocks)]
        sel = [jnp.where(i[blk] == 0, 0, jnp.where(i[blk] == nq - 1, 2, 1)) for blk in range(blocks)]
        cols = [slice((h // 2) * LANES, (h // 2 + 1) * LANES) for _, h in chains]
        qm = [jnp.where(low if h % 2 == 0 else jnp.logical_not(low), q[blk][:, cols[c]], jnp.zeros((), BF16))
              for c, (blk, h) in enumerate(chains)]
        s = [lax.dot_general(qm[c], kw[chains[c][0]][:, cols[c]], (((1,), (1,)), ((), ())),
                             preferred_element_type=F32) for c in idx]
        s = [s[c] * (HEAD_DIM ** -0.5) + bias_ref[chains[c][1], sel[chains[c][0]]] for c in idx]
        m = [jnp.max(s[c], axis=-1, keepdims=True) for c in idx]
        p = [jnp.exp(s[c] - m[c]) for c in idx]
        den = [jnp.sum(p[c], axis=-1, keepdims=True) for c in idx]
        o = [jnp.dot(p[c].astype(BF16), vw[chains[c][0]][:, cols[c]], preferred_element_type=F32) for c in idx]
        o = [o[c] / den[c] for c in idx]
        lse = [jnp.broadcast_to(m[c] + jnp.log(den[c]), (ATTN_TQ, LANES)) for c in idx]
        for c in range(0, len(chains), 2):
            rows = pl.ds(r0[chains[c][0]], ATTN_TQ)
            o_ref[rows, cols[c]] = jnp.where(low, o[c], o[c + 1]).astype(o_ref.dtype)
            lse_ref[rows, cols[c]] = jnp.where(low, lse[c], lse[c + 1])
        return carry

    lax.fori_loop(0, nq // blocks, body, 0)


def _attn_bias(group, dilation):
    slopes = np.exp2(-ALIBI_MAX_EXP * (np.arange(N_ATTN_HEADS, dtype=np.float32) + 1.0) / N_ATTN_HEADS)
    slopes = slopes.reshape(N_PATTERNS, HEADS_PER_GROUP)[group].astype(np.float32)
    qi = np.arange(ATTN_TQ)[:, None]
    kj = np.arange(ATTN_TK)[None, :]
    rel = kj - ATTN_RADIUS - qi
    band = np.abs(rel) <= ATTN_RADIUS
    alibi = -slopes[:, None, None] * (np.abs(rel) * dilation).astype(np.float32)[None]
    kinds = []
    for lo, hi in ((ATTN_RADIUS, ATTN_TK), (0, ATTN_TK), (0, ATTN_TQ + ATTN_RADIUS)):
        valid = band & (kj >= lo) & (kj < hi)
        kinds.append(np.where(valid[None], alibi, np.float32(NEG_INF)))
    return jnp.asarray(np.stack(kinds, axis=1), dtype=F32)


def _attention_group(qkv, group):
    batch, dilation, n, _ = qkv.shape
    assert n % ATTN_TQ == 0 and n >= 2 * ATTN_TQ

    def part(p):
        return pl.BlockSpec((None, None, n, GROUP_WIDTH), lambda b, r: (b, r, 0, p))

    out_blk = pl.BlockSpec((None, None, n, GROUP_WIDTH), lambda b, r: (b, r, 0, 0))
    return pl.pallas_call(
        functools.partial(_attn_kernel, n=n),
        grid=(batch, dilation),
        in_specs=[part(0), part(1), part(2), _const_spec((HEADS_PER_GROUP, 3, ATTN_TQ, ATTN_TK))],
        out_specs=[out_blk, out_blk],
        out_shape=[jax.ShapeDtypeStruct((batch, dilation, n, GROUP_WIDTH), BF16),
                   jax.ShapeDtypeStruct((batch, dilation, n, GROUP_WIDTH), F32)],
        scratch_shapes=[pltpu.VMEM((n + 2 * ATTN_RADIUS, GROUP_WIDTH), BF16)] * 2,
        compiler_params=_cparams(("parallel", "parallel")),
        name=f"attn_g{group}",
    )(qkv, qkv, qkv, _attn_bias(group, dilation))


def _filter_kernel(bands_ref, deltas_ref, w1_ref, b1_ref, w2_ref, b2_ref, w3_ref, b3_ref, w4_ref, freq_ref,
                   ft_ref, *, seq, tl):
    i = pl.program_id(0)
    freq = freq_ref[...]
    lane = lax.broadcasted_iota(jnp.int32, (1, LANES), 1)
    row = lax.broadcasted_iota(jnp.int32, (tl, 1), 0) + i * tl

    def half_filter(pos, half):
        posf = pos.astype(F32)
        t = posf / float(seq - 1)
        z = bands_ref[...] * (2.0 * math.pi * posf / float(seq))
        feat = jnp.where(lane == 0, t,
                         jnp.where(lane <= HYENA_BANDS, jnp.cos(z),
                                   jnp.where(lane <= 2 * HYENA_BANDS, -jnp.sin(z), 0.0)))
        h = jnp.sin(freq * (_mm(feat, w1_ref[...]) + b1_ref[...]))
        h = jnp.sin(freq * (_mm(h, w2_ref[...]) + b2_ref[...]))
        h = jnp.sin(freq * (_mm(h, w3_ref[...]) + b3_ref[...]))
        h = _mm(h, w4_ref[:, half * HYENA_WIDTH:(half + 1) * HYENA_WIDTH])
        return h * jnp.exp(-t * jnp.abs(deltas_ref[...]))

    hf = half_filter(row, 0)
    hb = half_filter(jnp.where(row == 0, 0, seq - row), 1)
    hb = jnp.where(row == 0, 0.0, hb)

    @pl.when(i == 0)
    def _():
        ft_ref[:, 0:CONV_P] = jnp.zeros((HYENA_WIDTH, CONV_P), F32)

    c0 = pl.multiple_of(CONV_P + i * tl, LANES)
    ft_ref[:, pl.ds(c0, tl)] = hb.T
    c1 = pl.multiple_of(CONV_P + seq + i * tl, LANES)
    ft_ref[:, pl.ds(c1, tl)] = hf.T

    @pl.when(i == pl.num_programs(0) - 1)
    def _():
        full = ft_ref[...]
        norm = jnp.sum(jnp.abs(full), axis=1, keepdims=True)
        ft_ref[...] = full / norm


def _hyena_filter(seq, w1, b1, w2, b2, w3, b3, w4, freq):
    tl = 512
    f = jnp.linspace(1e-4, HYENA_BANDS - 1, HYENA_BANDS, dtype=F32)
    bands = jnp.concatenate([jnp.zeros((1,), F32), f, f, jnp.zeros((LANES - HYENA_EMB,), F32)]).reshape(1, LANES)
    max_decay = math.log(HYENA_TARGET) / HYENA_FAST_DECAY
    min_decay = math.log(HYENA_TARGET) / HYENA_SLOW_DECAY
    deltas = jnp.linspace(min_decay, max_decay, HYENA_WIDTH, dtype=F32).reshape(1, HYENA_WIDTH)
    w1p = jnp.pad(w1, ((0, LANES - HYENA_EMB), (0, 0)))
    row = lambda a: a.reshape(1, -1)
    args = (bands, deltas, w1p, row(b1), w2, row(b2), w3, row(b3), w4, row(freq))
    return pl.pallas_call(
        functools.partial(_filter_kernel, seq=seq, tl=tl),
        grid=(seq // tl,),
        in_specs=[_const_spec(a.shape) for a in args],
        out_specs=pl.BlockSpec((HYENA_WIDTH, CONV_P + 2 * seq), lambda i: (0, 0)),
        out_shape=jax.ShapeDtypeStruct((HYENA_WIDTH, CONV_P + 2 * seq), F32),
        compiler_params=_cparams(("arbitrary",)),
        name="hyena_filter",
    )(*args)


def _shift_rows(x, prev_row, next_row):
    n = x.shape[0]
    row = lax.broadcasted_iota(jnp.int32, (n, 1), 0)
    xm = jnp.where(row == 0, prev_row, pltpu.roll(x, 1, axis=0))
    xp = jnp.where(row == n - 1, next_row, pltpu.roll(x, n - 1, axis=0))
    return xm, xp


def _hyena_pre_kernel(x_ref, prev_ref, next_ref, w_ref, b_ref, zt_ref, x0t_ref):
    i = pl.program_id(1)
    x = x_ref[0]
    prev_row = jnp.where(i == 0, 0.0, prev_ref[0, SUBLANES - 1:SUBLANES, :])
    next_row = jnp.where(i == pl.num_programs(1) - 1, 0.0, next_ref[0, 0:1, :])
    xm, xp = _shift_rows(x, prev_row, next_row)
    u = xm * w_ref[0:1, :] + x * w_ref[1:2, :] + xp * w_ref[2:3, :] + b_ref[...]
    c = HYENA_WIDTH
    x0, x1, v = u[:, :c], u[:, c:2 * c], u[:, 2 * c:]
    zt_ref[0] = (v * x1).T
    x0t_ref[0] = x0.T


def _hyena_pre(hy3, short_w, short_b, tl):
    batch, seq, width = hy3.shape
    nt = seq // tl
    hb = tl // SUBLANES
    last = seq // SUBLANES - 1
    out_blk = pl.BlockSpec((1, HYENA_WIDTH, tl), lambda b, i: (b, 0, i))
    return pl.pallas_call(
        _hyena_pre_kernel,
        grid=(batch, nt),
        in_specs=[pl.BlockSpec((1, tl, width), lambda b, i: (b, i, 0)),
                  pl.BlockSpec((1, SUBLANES, width), lambda b, i: (b, jnp.maximum(i * hb - 1, 0), 0)),
                  pl.BlockSpec((1, SUBLANES, width), lambda b, i: (b, jnp.minimum((i + 1) * hb, last), 0)),
                  _const_spec((3, width)), _const_spec((1, width))],
        out_specs=[out_blk, out_blk],
        out_shape=[jax.ShapeDtypeStruct((batch, HYENA_WIDTH, seq), F32)] * 2,
        compiler_params=_cparams(("parallel", "parallel")),
        name="hyena_pre",
    )(hy3, hy3, hy3, short_w, short_b.reshape(1, width))


def _hyena_conv_kernel(bias_ref, zt_ref, x0t_ref, ft_ref, o_ref, troll, zpad, *, seq, cb):
    batch = zt_ref.shape[0]
    nb = seq // CONV_P
    p = CONV_P
    g = pl.program_id(0)
    zero_margin = jnp.zeros((batch, nb, p), F32)
    zpad[:, 0:nb, :] = zero_margin
    zpad[:, 2 * nb:3 * nb, :] = zero_margin
    chunk = 1024

    def channel(ci, carry):
        for w in range(2 * seq // chunk):
            a = w * chunk
            src = ft_ref[pl.ds(ci, 1), a:a + chunk + p]
            rolled = pltpu.roll(jnp.broadcast_to(src, (p, chunk + p)), 0, axis=1, stride=1, stride_axis=0)
            troll[:, a:a + chunk] = rolled[:, p:].astype(BF16)
        z = zt_ref[:, ci, :, :]
        zpad[:, nb:2 * nb, :] = z
        acc = jnp.zeros((batch * nb, p), F32)
        for d in range(-(nb - 1), nb):
            zs = zpad[:, nb - d:2 * nb - d, :].reshape(batch * nb, p).astype(BF16)
            t = troll[:, seq + d * p:seq + (d + 1) * p]
            acc = acc + jnp.dot(zs, t, preferred_element_type=F32)
        bias = bias_ref[g * cb + ci]
        y = (acc.reshape(batch, nb, p) + z * bias) * x0t_ref[:, ci, :, :]
        o_ref[:, ci, :, :] = y
        return carry

    lax.fori_loop(0, cb, channel, 0)


def _hyena_conv(zt, x0t, ft, filt_bias, cb):
    batch, c, seq = zt.shape
    nb = seq // CONV_P
    z4 = zt.reshape(batch, c, nb, CONV_P)
    x4 = x0t.reshape(batch, c, nb, CONV_P)
    blk = pl.BlockSpec((batch, cb, nb, CONV_P), lambda g: (0, g, 0, 0))
    out = pl.pallas_call(
        functools.partial(_hyena_conv_kernel, seq=seq, cb=cb),
        grid=(c // cb,),
        in_specs=[pl.BlockSpec(memory_space=pltpu.SMEM), blk, blk,
                  pl.BlockSpec((cb, CONV_P + 2 * seq), lambda g: (g, 0))],
        out_specs=blk,
        out_shape=jax.ShapeDtypeStruct((batch, c, nb, CONV_P), F32),
        scratch_shapes=[pltpu.VMEM((CONV_P, 2 * seq), BF16),
                        pltpu.VMEM((batch, 3 * nb, CONV_P), F32)],
        compiler_params=_cparams(("parallel",)),
        name="hyena_conv",
    )(filt_bias, z4, x4, ft)
    return out.reshape(batch, c, seq)


def _interleave_classes(blk_ref, scr_ref):
    dil, per, width = blk_ref.shape
    slabs = width // LANES
    for r in range(dil):
        rows = blk_ref[r].astype(F32)
        for cb in range(slabs):
            scr_ref[cb, pl.ds(r, per, stride=dil), :] = rows[:, cb * LANES:(cb + 1) * LANES]
    return jnp.concatenate([scr_ref[cb] for cb in range(slabs)], axis=1)


def _outproj_kernel(o0, o1, o2, l0, l1, l2, hyt_ref, xa_ref, xb_ref, mod_ref, wa_ref, wh_ref, out_ref,
                    so1, so2, sl1, sl2, *, n_first):
    ls = [l0[0], _interleave_classes(l1, sl1), _interleave_classes(l2, sl2)]
    os_ = [o0[0].astype(F32), _interleave_classes(o1, so1), _interleave_classes(o2, so2)]
    x = jnp.where(pl.program_id(0) < n_first, xa_ref[...], xb_ref[...])
    m = jnp.maximum(jnp.maximum(ls[0], ls[1]), ls[2])
    es = [jnp.exp(l - m) for l in ls]
    den = es[0] + es[1] + es[2]
    attn = (es[0] * os_[0] + es[1] * os_[1] + es[2] * os_[2]) / den
    hy = hyt_ref[0].T
    mix = jnp.dot(attn.astype(BF16), wa_ref[...], preferred_element_type=F32)
    mix = mix + jnp.dot(hy.astype(BF16), wh_ref[...], preferred_element_type=F32)
    out_ref[...] = x + mod_ref[0, 2:3, :] * mix


def _out_proj(os_, ls, hyt, xa, xb, mods, w_out_bf, seq, tm):
    d = xa.shape[1]
    rows = xa.shape[0] + xb.shape[0]
    tpb = seq // tm
    wa, wh = w_out_bf[:GROUP_WIDTH], w_out_bf[GROUP_WIDTH:]

    def class_spec(a):
        dil = a.shape[1]
        return pl.BlockSpec((None, dil, tm // dil, GROUP_WIDTH), lambda i: (i // tpb, 0, i % tpb, 0))

    return pl.pallas_call(
        functools.partial(_outproj_kernel, n_first=xa.shape[0] // tm),
        grid=(rows // tm,),
        in_specs=[class_spec(a) for a in (*os_, *ls)] + [
            pl.BlockSpec((1, HYENA_WIDTH, tm), lambda i: (i // tpb, 0, i % tpb))]
        + _two_source_specs(tm, d, xa.shape[0] // tm) + [
            pl.BlockSpec((1, 6, d), lambda i: (i // tpb, 0, 0)),
            _const_spec(wa.shape), _const_spec(wh.shape)],
        out_specs=pl.BlockSpec((tm, d), lambda i: (i, 0)),
        out_shape=jax.ShapeDtypeStruct((rows, d), F32),
        scratch_shapes=[pltpu.VMEM((GROUP_WIDTH // LANES, tm, LANES), F32)] * 4,
        compiler_params=_cparams(("parallel",)),
        name="out_proj",
    )(*os_, *ls, hyt, xa, xb, mods, wa, wh)


def _halo_rows(x_ref, prev_ref, next_ref, tiles_per_seq):
    i = pl.program_id(0)
    first = (i % tiles_per_seq) == 0
    last = (i % tiles_per_seq) == tiles_per_seq - 1
    xe = jnp.concatenate([prev_ref[...], x_ref[...], next_ref[...]], axis=0)
    return xe, first, last


def _ffn_kernel(x_ref, prev_ref, next_ref, mod_ref, gain_ref, wa_ref, wg_ref, cw_ref, cb_ref, wd_ref, fin_ref,
                *outs_and_scratch, tiles_per_seq, fc, final_norm, n_first):
    *out_refs, act_ref, a0_ref, g0_ref, a1_ref, g1_ref = outs_and_scratch
    tm = x_ref.shape[0]
    x = x_ref[...]
    xe, first, last = _halo_rows(x_ref, prev_ref, next_ref, tiles_per_seq)
    he32 = _norm_mod(xe, gain_ref[...], mod_ref[0, 3:4, :], mod_ref[0, 4:5, :])
    he = he32.astype(BF16)
    h = he32[SUBLANES:SUBLANES + tm].astype(BF16)
    row = lax.broadcasted_iota(jnp.int32, (tm + 2 * SUBLANES, 1), 0)
    lo = jnp.where(first, SUBLANES, 0)
    hi = jnp.where(last, tm + SUBLANES, tm + 2 * SUBLANES)
    keep = jnp.logical_and(row >= lo, row < hi)
    nchunk = wa_ref.shape[1] // fc
    n_ext = tm + 2 * SUBLANES

    def up(j, bufs):
        c0 = pl.multiple_of(j * fc, fc)
        bufs[0][...] = jnp.dot(he, wa_ref[:, pl.ds(c0, fc)], preferred_element_type=F32)
        bufs[1][...] = jnp.dot(h, wg_ref[:, pl.ds(c0, fc)], preferred_element_type=F32)

    def activate(j, bufs):
        c0 = pl.multiple_of(j * fc, fc)
        a = jnp.where(keep, bufs[0][...], 0.0)
        am = pltpu.roll(a, 1, axis=0)[SUBLANES:SUBLANES + tm]
        ap = pltpu.roll(a, n_ext - 1, axis=0)[SUBLANES:SUBLANES + tm]
        ac = a[SUBLANES:SUBLANES + tm]
        cw = cw_ref[:, pl.ds(c0, fc)]
        conv = am * cw[0:1] + ac * cw[1:2] + ap * cw[2:3] + cb_ref[:, pl.ds(c0, fc)]
        act_ref[:, pl.ds(c0, fc)] = (jax.nn.gelu(conv) * bufs[1][...]).astype(BF16)

    even, odd = (a0_ref, g0_ref), (a1_ref, g1_ref)
    up(0, even)

    def body(j, carry):
        up(2 * j + 1, odd)
        activate(2 * j, even)
        up(2 * j + 2, even)
        activate(2 * j + 1, odd)
        return carry

    assert nchunk % 2 == 1
    lax.fori_loop(0, nchunk // 2, body, 0)
    activate(nchunk - 1, even)
    down = jnp.dot(act_ref[...], wd_ref[...], preferred_element_type=F32)
    y = x + mod_ref[0, 5:6, :] * down
    if final_norm:
        ms = jnp.mean(y * y, axis=-1, keepdims=True)
        y = y * lax.rsqrt(ms + RMS_EPS) * fin_ref[...]
    if n_first is None:
        out_refs[0][...] = y
    else:
        i = pl.program_id(0)

        @pl.when(i < n_first)
        def _():
            out_refs[0][...] = y

        @pl.when(i >= n_first)
        def _():
            out_refs[1][...] = y


def _halo_specs(tm, d, nrows):
    hb = tm // SUBLANES
    last = nrows // SUBLANES - 1
    return [pl.BlockSpec((tm, d), lambda i: (i, 0)),
            pl.BlockSpec((SUBLANES, d), lambda i: (jnp.maximum(i * hb - 1, 0), 0)),
            pl.BlockSpec((SUBLANES, d), lambda i: (jnp.minimum((i + 1) * hb, last), 0))]


def _conv_ffn(x2, mods, gain, up_bf, conv_w, conv_b, down_bf, fin_gain, seq, tm, final_norm, split_rows=None):
    rows, d = x2.shape
    dff = down_bf.shape[0]
    fc = 256
    tpb = seq // tm
    wa, wg = up_bf[:, :dff], up_bf[:, dff:]
    if split_rows is None:
        n_first = None
        out_specs = pl.BlockSpec((tm, d), lambda i: (i, 0))
        out_shape = jax.ShapeDtypeStruct((rows, d), F32)
    else:
        n_first = split_rows // tm
        out_specs = _two_source_specs(tm, d, n_first)
        out_shape = [jax.ShapeDtypeStruct((split_rows, d), F32), jax.ShapeDtypeStruct((rows - split_rows, d), F32)]
    return pl.pallas_call(
        functools.partial(_ffn_kernel, tiles_per_seq=tpb, fc=fc, final_norm=final_norm, n_first=n_first),
        grid=(rows // tm,),
        in_specs=_halo_specs(tm, d, rows) + [
            pl.BlockSpec((1, 6, d), lambda i: (i // tpb, 0, 0)),
            _const_spec((1, d)), _const_spec(wa.shape), _const_spec(wg.shape),
            _const_spec((3, dff)), _const_spec((1, dff)), _const_spec(down_bf.shape), _const_spec((1, d))],
        out_specs=out_specs,
        out_shape=out_shape,
        scratch_shapes=[pltpu.VMEM((tm, dff), BF16)]
        + [pltpu.VMEM((tm + 2 * SUBLANES, fc), F32), pltpu.VMEM((tm, fc), F32)] * 2,
        compiler_params=_cparams(("arbitrary",)),
        name="conv_ffn",
    )(x2, x2, x2, mods, gain.reshape(1, d), wa, wg, conv_w, conv_b.reshape(1, dff), down_bf,
      fin_gain.reshape(1, d))


def _head_sum(x, ones_bd):
    xb = x.astype(BF16)
    parts = [jnp.dot(xb[:, c * MXU_DIM:(c + 1) * MXU_DIM], ones_bd, preferred_element_type=F32)
             for c in range(x.shape[1] // MXU_DIM)]
    return jnp.concatenate(parts, axis=1)


def _sigmoid(x):
    return 0.5 * jnp.tanh(0.5 * x) + 0.5


def _ones_blockdiag():
    r = lax.broadcasted_iota(jnp.int32, (MXU_DIM, MXU_DIM), 0) // RWKV_HEAD
    c = lax.broadcasted_iota(jnp.int32, (MXU_DIM, MXU_DIM), 1) // RWKV_HEAD
    return jnp.where(r == c, 1.0, 0.0).astype(BF16)


def _rwkv_pre_kernel(x_ref, prev_ref, next_ref, mod_ref, gain_ref, mu_ref, wr_ref, wk_ref, wv_ref,
                     w1_ref, w2_ref, w0_ref, a1_ref, a2_ref, a0_ref, g1_ref, g2_ref, kk_w_ref, ka_ref, rk_ref,
                     r_out, v_out, kk_out, k_out, a0_out, a1_out, lw0_out, lw1_out, g_out, bonus_out, he_ref, hb_ref, xx_ref,
                     *, tiles_per_seq):
    tm, d = x_ref.shape
    i = pl.program_id(0)
    first = (i % tiles_per_seq) == 0
    last = (i % tiles_per_seq) == tiles_per_seq - 1
    norm = lambda x: _norm_mod(x, gain_ref[...], mod_ref[0, 0:1, :], mod_ref[0, 1:2, :])
    ones_bd = _ones_blockdiag()
    he_ref[0:SUBLANES, :] = jnp.where(first, 0.0, norm(prev_ref[...]))
    he_ref[SUBLANES + tm:, :] = jnp.where(last, 0.0, norm(next_ref[...]))

    def row_part(lo, n):
        rows = slice(lo, lo + n)
        mid = slice(SUBLANES + lo, SUBLANES + lo + n)
        h = norm(x_ref[rows, :])
        he_ref[mid, :] = h
        hb_ref[rows, :] = h.astype(BF16)
        yield
        hm = he_ref[SUBLANES + lo - 1:SUBLANES + lo - 1 + n, :]
        hp = he_ref[SUBLANES + lo + 1:SUBLANES + lo + 1 + n, :]
        xx_ref[rows, :] = (0.5 * (hm + hp) - he_ref[mid, :]).astype(BF16)
        bf16_rows = 2 * SUBLANES

        def mixed(i):
            mu_tile = jnp.broadcast_to(mu_ref[i:i + 1, :], (bf16_rows, d)).astype(BF16)
            return hb_ref[rows, :] + xx_ref[rows, :] * jnp.tile(mu_tile, (n // bf16_rows, 1))

        r = jnp.dot(mixed(0), wr_ref[...], preferred_element_type=F32)
        r_out[rows, :] = r.astype(BF16)
        yield
        k = jnp.dot(mixed(2), wk_ref[...], preferred_element_type=F32)
        k_out[rows, :] = k.astype(BF16)
        kk = k * kk_w_ref[...]
        kk_out[rows, :] = (kk * jnp.minimum(lax.rsqrt(_head_sum(kk * kk, ones_bd)), 1e12)).astype(BF16)
        yield
        v = jnp.dot(mixed(3), wv_ref[...], preferred_element_type=F32)
        v_out[rows, :] = v.astype(BF16)
        yield
        gl = _sigmoid(jnp.dot(mixed(5), g1_ref[...], preferred_element_type=F32))
        g_out[rows, :] = jnp.dot(gl.astype(BF16), g2_ref[...], preferred_element_type=F32).astype(BF16)
        yield
        wl = jnp.tanh(jnp.dot(mixed(1), w1_ref[...], preferred_element_type=F32))
        wl = jnp.dot(wl.astype(BF16), w2_ref[...], preferred_element_type=F32)
        for direction, lw_o in enumerate((lw0_out, lw1_out)):
            u = w0_ref[direction:direction + 1, :] + wl[:, direction * d:(direction + 1) * d]
            lw_o[rows, :] = -_sigmoid(u) * math.exp(-0.5)
        yield
        al = jnp.dot(mixed(4), a1_ref[...], preferred_element_type=F32)
        al = jnp.dot(al.astype(BF16), a2_ref[...], preferred_element_type=F32)
        ksum = jnp.zeros_like(k)
        for direction, a_o in enumerate((a0_out, a1_out)):
            a = _sigmoid(a0_ref[direction:direction + 1, :] + al[:, direction * d:(direction + 1) * d])
            a_o[rows, :] = a.astype(BF16)
            ksum = ksum + _k_dir(k, a, ka_ref[...])
        bonus_out[rows, :] = (_head_sum(r * (0.5 * ksum) * rk_ref[...], ones_bd) * v).astype(BF16)

    parts = 2
    _emit_interleaved(*[row_part(p * (tm // parts), tm // parts) for p in range(parts)])


def _k_dir(k, a, k_a):
    return k * (1.0 + (a - 1.0) * k_a)


def _blockdiag2(m):
    z = jnp.zeros_like(m[0])
    return jnp.concatenate([jnp.concatenate([m[0], z], axis=1), jnp.concatenate([z, m[1]], axis=1)], axis=0)


def _rwkv_pre(x2, mods, p, seq, tm):
    rows, d = x2.shape
    tpb = seq // tm
    bf = lambda a: a.astype(BF16)
    w1 = bf(jnp.concatenate([p['w1'][0], p['w1'][1]], axis=1))
    w2 = bf(_blockdiag2(p['w2']))
    a1 = bf(jnp.concatenate([p['a1'][0], p['a1'][1]], axis=1))
    a2 = bf(_blockdiag2(p['a2']))
    glora = p['g1'].shape[1]
    gpad = -glora % LANES
    g1 = bf(jnp.pad(p['g1'], ((0, 0), (0, gpad))))
    g2 = bf(jnp.pad(p['g2'], ((0, gpad), (0, 0))))
    row = lambda a: a.reshape(1, d)
    args = (x2, x2, x2, mods, row(p['norm1']), p['mu'], bf(p['w_r']), bf(p['w_k']), bf(p['w_v']),
            w1, w2, p['w0'], a1, a2, p['a0'], g1, g2, row(p['k_k']), row(p['k_a']), row(p['r_k']))
    in_specs = _halo_specs(tm, d, rows) + [pl.BlockSpec((1, 6, d), lambda i: (i // tpb, 0, 0))]
    in_specs += [_const_spec(a.shape) for a in args[4:]]
    ob = pl.BlockSpec((tm, d), lambda i: (i, 0))
    return pl.pallas_call(
        functools.partial(_rwkv_pre_kernel, tiles_per_seq=tpb),
        grid=(rows // tm,),
        in_specs=in_specs,
        out_specs=[ob] * 10,
        out_shape=[jax.ShapeDtypeStruct((rows, d), BF16)] * 6 + [jax.ShapeDtypeStruct((rows, d), F32)] * 2
        + [jax.ShapeDtypeStruct((rows, d), BF16)] * 2,
        scratch_shapes=[pltpu.VMEM((tm + 2 * SUBLANES, d), F32), pltpu.VMEM((tm, d), BF16),
                        pltpu.VMEM((tm, d), BF16)],
        compiler_params=_cparams(("parallel",)),
        name="rwkv_pre",
    )(*args)


def _scan_masks(reverse):
    c = SCAN_C
    assert c == RWKV_HEAD
    t = lax.broadcasted_iota(jnp.int32, (c, 4 * c), 0)
    s = lax.broadcasted_iota(jnp.int32, (c, 4 * c), 1) % c
    before = (s > t) if reverse else (s < t)
    upto = jnp.logical_or(before, s == t)
    eye = jnp.where(s == t, 1.0, 0.0)
    rr = lax.broadcasted_iota(jnp.int32, (4 * c, SCAN_HG), 0)
    cc = lax.broadcasted_iota(jnp.int32, (4 * c, SCAN_HG), 1)
    same_head = jnp.where((rr // c) == (cc // RWKV_HEAD), 1.0, 0.0)
    return before, upto, eye, same_head, same_head.astype(BF16)


def _bd(xb, head_mask_bf):
    return jnp.concatenate([xb] * 4, axis=0) * head_mask_bf


def _bd_t(x, head_mask_bf):
    xt = jnp.concatenate([x, x], axis=0).T.astype(BF16)
    return jnp.concatenate([xt, xt], axis=1) * head_mask_bf


def _mm(a, b):
    return jnp.dot(a, b, precision=HIGHEST, preferred_element_type=F32)


def _bdot(a, b):
    return jnp.dot(a, b, preferred_element_type=F32)


def _cumsum_rows(tri_bf, x):
    hi = x.astype(BF16)
    lo = (x - hi.astype(F32)).astype(BF16)
    return _bdot(tri_bf, hi) + _bdot(tri_bf, lo)


def _scan_prologue(chains):
    c = SCAN_C
    bf = lambda x: x.astype(BF16)
    idx = range(len(chains))
    r, v, kk, lw, k, a = [[ch[0][j] for ch in chains] for j in range(6)]
    head_mask_bf = chains[0][1][5]
    rev = [ch[2] for ch in chains]
    g = [_cumsum_rows(chains[i][1][0], lw[i]) for i in idx]
    gtot = [g[i][0:1] if rev[i] else g[i][c - 1:c] for i in idx]
    kkd = [kk[i] * jnp.exp(g[i] - lw[i]) for i in idx]
    rd = [r[i] * jnp.exp(g[i]) for i in idx]
    b = [kk[i] * a[i] for i in idx]
    ginv = [jnp.exp(-g[i]) for i in idx]
    lhs = [bf(jnp.concatenate([kkd[i], rd[i]], axis=0)) for i in idx]
    bi_t = [_bd_t(b[i] * ginv[i], head_mask_bf) for i in idx]
    ki_t = [_bd_t(k[i] * ginv[i], head_mask_bf) for i in idx]
    gend = [jnp.exp(gtot[i] - g[i]) for i in idx]
    upd_t = [bf(jnp.concatenate([b[i] * gend[i], k[i] * gend[i]], axis=0).T) for i in idx]
    gcol_src = [jnp.concatenate([g[i], g[i]], axis=0).T for i in idx]
    gcol = [jnp.exp(gcol_src[i][:, 0:1] if rev[i] else gcol_src[i][:, c - 1:c]) for i in idx]
    vb = [bf(v[i]) for i in idx]
    return chains, (vb, lhs, bi_t, ki_t, upd_t, gcol)


def _scan_prepare_stages(prologue, out):
    c = SCAN_C
    bf = lambda x: x.astype(BF16)
    chains, (vb, lhs, bi_t, ki_t, upd_t, gcol) = prologue
    idx = range(len(chains))
    before, upto, eye = [[ch[1][j] for ch in chains] for j in range(1, 4)]
    head_mask_bf = chains[0][1][5]

    a1 = [_bdot(lhs[i], bi_t[i]) for i in idx]
    a2 = [_bdot(lhs[i], ki_t[i]) for i in idx]
    n_ab = [jnp.where(before[i], a1[i][:c], 0.0) for i in idx]
    a_rb = [jnp.where(upto[i], a1[i][c:], 0.0) for i in idx]
    a_k = [jnp.concatenate([jnp.where(before[i], a2[i][:c], 0.0), jnp.where(upto[i], a2[i][c:], 0.0)], axis=0)
           for i in idx]
    yield

    nb = [bf(n_ab[i]) for i in idx]
    npow = [_bdot(nb[i], _bd(nb[i], head_mask_bf)) for i in idx]
    tinv = [eye[i] - n_ab[i] for i in idx]
    yield
    steps = int(math.log2(c)) - 1
    for step in range(steps):
        pb = [bf(npow[i]) for i in idx]
        rhs = [_bd(pb[i], head_mask_bf) for i in idx]
        if step == steps - 1:
            tinv = [tinv[i] + _bdot(bf(tinv[i]), rhs[i]) for i in idx]
        else:
            prod = [_bdot(jnp.concatenate([pb[i], bf(tinv[i])], axis=0), rhs[i]) for i in idx]
            tinv = [tinv[i] + prod[i][c:] for i in idx]
            npow = [prod[i][:c] for i in idx]
        yield

    av = [_bdot(bf(a_k[i]), _bd(vb[i], head_mask_bf)) for i in idx]
    yield
    rhs = [jnp.concatenate([_bd(lhs[i][:c], head_mask_bf), _bd(bf(av[i][:c]), head_mask_bf)], axis=1) for i in idx]
    tw = [_bdot(bf(tinv[i]), rhs[i]) for i in idx]
    lhs_state = [jnp.concatenate([lhs[i][c:], bf(tw[i][:, :SCAN_HG])], axis=0) for i in idx]
    wt = [tw[i][:, SCAN_HG:] for i in idx]
    out.extend((lhs_state[i], wt[i], bf(a_rb[i]), av[i][c:], upd_t[i], gcol[i], vb[i]) for i in idx)


def _scan_apply_stages(preps, states, head_mask, head_mask_bf, emit_y):
    c = SCAN_C
    for pos in range(0, len(preps), 2):
        lhs, wt, a_rb, av_r, upd_t, gcol, vb = zip(*preps[pos:pos + 2])
        o = [_bdot(lhs[i], states[i].astype(BF16)) for i in range(2)]
        yield
        ub = [(-(wt[i] + o[i][c:])).astype(BF16) for i in range(2)]
        upd = [_bdot(upd_t[i], jnp.concatenate([ub[i], vb[i]], axis=0)) for i in range(2)]
        y = [o[i][:c] + av_r[i] + _bdot(a_rb[i], _bd(ub[i], head_mask_bf)) for i in range(2)]
        yield
        for i in range(2):
            emit_y(pos + i, y[i])
            states[i] = states[i] * gcol[i] + head_mask * upd[i]


def _emit_interleaved(*stage_generators):
    pending = list(stage_generators)
    while pending:
        for gen in list(pending):
            try:
                next(gen)
            except StopIteration:
                pending.remove(gen)


def _scan_consts(reverse):
    c = SCAN_C
    before, upto, eye, head_mask, head_mask_bf = _scan_masks(reverse)
    tt = lax.broadcasted_iota(jnp.int32, (c, c), 0)
    ss = lax.broadcasted_iota(jnp.int32, (c, c), 1)
    tri = jnp.where((ss >= tt) if reverse else (ss <= tt), 1.0, 0.0).astype(BF16)
    return tri, before, upto, eye, head_mask, head_mask_bf


def _scan_kernel(ka_ref, rf_ref, vf_ref, kkf_ref, kf_ref, af_ref, lwf_ref, rb_ref, vb_ref, kkb_ref, kb_ref, ab_ref,
                 lwb_ref, yf_ref, yb_ref, sf_ref, sb_ref):
    tb = rf_ref.shape[0]
    nchunk = tb // SCAN_C

    @pl.when(pl.program_id(2) == 0)
    def _():
        sf_ref[...] = jnp.zeros_like(sf_ref)
        sb_ref[...] = jnp.zeros_like(sb_ref)

    consts_f = _scan_consts(False)
    consts_b = _scan_consts(True)
    head_mask = consts_f[4]
    k_a = ka_ref[...]

    def operands(refs, rows):
        r, v, kk, k, a = (ref[rows, :].astype(F32) for ref in refs[:5])
        return r, v, kk, refs[5][rows, :], _k_dir(k, a, k_a), a

    fwd_refs = (rf_ref, vf_ref, kkf_ref, kf_ref, af_ref, lwf_ref)
    bwd_refs = (rb_ref, vb_ref, kkb_ref, kb_ref, ab_ref, lwb_ref)

    def emitter(rows):
        def emit_y(pos, y):
            ref = yb_ref if pos % 2 else yf_ref
            ref[rows[pos], :] = y.astype(ref.dtype)
        return emit_y

    def group_rows(group):
        rows = []
        for u in range(SCAN_UNROLL):
            j = group * SCAN_UNROLL + u
            rows += [pl.ds(j * SCAN_C, SCAN_C), pl.ds((nchunk - 1 - j) * SCAN_C, SCAN_C)]
        return rows

    def group_prologue(rows):
        chains = []
        for pos, rs in enumerate(rows):
            reverse = pos % 2 == 1
            chains.append((operands(bwd_refs if reverse else fwd_refs, rs), consts_b if reverse else consts_f,
                           reverse))
        return _scan_prologue(chains)

    def run_once(fn, box):
        box.append(fn())
        yield

    ngroups = nchunk // SCAN_UNROLL
    states = [sf_ref[...], sb_ref[...]]
    pending = None
    rows = group_rows(0)
    prologue = group_prologue(rows)
    for group in range(ngroups):
        preps, next_box = [], []
        stages = [_scan_prepare_stages(prologue, preps)]
        if pending is not None:
            stages.append(_scan_apply_stages(pending[0], states, head_mask, consts_f[5], emitter(pending[1])))
        if group + 1 < ngroups:
            next_rows = group_rows(group + 1)
            stages.append(run_once(functools.partial(group_prologue, next_rows), next_box))
        _emit_interleaved(*stages)
        pending = (preps, rows)
        if group + 1 < ngroups:
            rows, prologue = next_rows, next_box[0]
    _emit_interleaved(_scan_apply_stages(pending[0], states, head_mask, consts_f[5], emitter(pending[1])))
    sf_ref[...] = states[0]
    sb_ref[...] = states[1]


def _wkv_scan(r, v, kk, k, a0, a1, lw0, lw1, k_a, batch, seq, tb):
    rows, d = r.shape
    nt = seq // tb
    ng = d // SCAN_HG
    fwd = pl.BlockSpec((tb, SCAN_HG), lambda b, g, t: (b * nt + t, g))
    bwd = pl.BlockSpec((tb, SCAN_HG), lambda b, g, t: (b * nt + nt - 1 - t, g))
    return pl.pallas_call(
        _scan_kernel,
        grid=(batch, ng, nt),
        in_specs=[pl.BlockSpec((1, SCAN_HG), lambda b, g, t: (0, g))] + [fwd] * 6 + [bwd] * 6,
        out_specs=[fwd, bwd],
        out_shape=[jax.ShapeDtypeStruct((rows, d), BF16)] * 2,
        scratch_shapes=[pltpu.VMEM((SCAN_HG, SCAN_HG), F32)] * 2,
        compiler_params=_cparams(("parallel", "parallel", "arbitrary")),
        name="wkv_scan",
    )(k_a.reshape(1, d), r, v, kk, k, a0, lw0, r, v, kk, k, a1, lw1)


def _rwkv_post_kernel(yf_ref, yb_ref, bonus_ref, g_ref, x_ref, mod_ref, lnw_ref, lnb_ref, wo_ref, out_ref):
    tm = x_ref.shape[0]
    ones_bd = _ones_blockdiag()
    halves = [slice(p * (tm // 2), (p + 1) * (tm // 2)) for p in range(2)]
    y = [yf_ref[rows, :].astype(F32) + yb_ref[rows, :].astype(F32) for rows in halves]
    mu = [_head_sum(y[p], ones_bd) * (1.0 / RWKV_HEAD) for p in range(2)]
    yc = [y[p] - mu[p] for p in range(2)]
    var = [_head_sum(yc[p] * yc[p], ones_bd) * (1.0 / RWKV_HEAD) for p in range(2)]
    yn = [yc[p] * lax.rsqrt(var[p] + GN_EPS) * lnw_ref[...] + lnb_ref[...] for p in range(2)]
    out = [((yn[p] + bonus_ref[rows, :].astype(F32)) * g_ref[rows, :].astype(F32)).astype(BF16)
           for p, rows in enumerate(halves)]
    mix = [jnp.dot(out[p], wo_ref[...], preferred_element_type=F32) for p in range(2)]
    for p, rows in enumerate(halves):
        out_ref[rows, :] = x_ref[rows, :] + mod_ref[0, 2:3, :] * mix[p]


def _rwkv_post(yf, yb, bonus, g, x2, mods, ln_w, ln_b, wo_bf, seq, tm):
    rows, d = x2.shape
    tpb = seq // tm
    rb = pl.BlockSpec((tm, d), lambda i: (i, 0))
    return pl.pallas_call(
        _rwkv_post_kernel,
        grid=(rows // tm,),
        in_specs=[rb] * 5 + [pl.BlockSpec((1, 6, d), lambda i: (i // tpb, 0, 0)),
                             _const_spec((1, d)), _const_spec((1, d)), _const_spec((d, d))],
        out_specs=rb,
        out_shape=jax.ShapeDtypeStruct((rows, d), F32),
        compiler_params=_cparams(("parallel",)),
        name="rwkv_post",
    )(yf, yb, bonus, g, x2, mods, ln_w.reshape(1, d), ln_b.reshape(1, d), wo_bf)


def _mods(c, w, b):
    batch, d = c.shape
    pad = -batch % SUBLANES
    m = _ada_params(jnp.pad(c, ((0, pad), (0, 0))), w, b)[:batch]
    return m.reshape(batch, 6, d)


def _group_major_w_in(w_in):
    d = w_in.shape[0]
    qkv = w_in[:, :3 * ATTN_WIDTH].reshape(d, 3, N_PATTERNS, GROUP_WIDTH)
    qkv = jnp.transpose(qkv, (0, 2, 1, 3)).reshape(d, 3 * ATTN_WIDTH)
    return jnp.concatenate([qkv, w_in[:, 3 * ATTN_WIDTH:]], axis=1)


def _trunk(xa, xb, c, l0, l1, final_norm):
    seq, d = xa.shape[1:]
    assert xb.shape[1:] == (seq, d)
    batch = xa.shape[0] + xb.shape[0]
    xa2, xb2 = xa.reshape(-1, d), xb.reshape(-1, d)
    bf = lambda a: a.astype(BF16)
    tm = 512

    mods0 = _mods(c, l0['ada_w'], l0['ada_b'])
    q0, q1, q2, hy = _in_proj(xa2, xb2, mods0, l0['norm1'], bf(_group_major_w_in(l0['w_in'])), seq, tm)
    os_, ls = [], []
    for group, qkv in enumerate((q0.reshape(batch, 1, seq, 3 * GROUP_WIDTH), q1, q2)):
        o, lse = _attention_group(qkv, group)
        os_.append(o)
        ls.append(lse)
    ft = _hyena_filter(seq, l0['filt_w1'], l0['filt_b1'], l0['filt_w2'], l0['filt_b2'], l0['filt_w3'],
                       l0['filt_b3'], l0['filt_w4'], l0['filt_freq'])
    zt, x0t = _hyena_pre(hy.reshape(batch, seq, 3 * HYENA_WIDTH), l0['short_w'], l0['short_b'], tl=512)
    hyt = _hyena_conv(zt, x0t, ft, l0['filt_bias'], cb=8)
    x2 = _out_proj(os_, ls, hyt, xa2, xb2, mods0, bf(l0['w_out']), seq, tm)
    x2 = _conv_ffn(x2, mods0, l0['norm2'], bf(l0['ffn_up']), l0['ffn_conv_w'], l0['ffn_conv_b'],
                   bf(l0['ffn_down']), final_norm, seq, tm, final_norm=False)

    mods1 = _mods(c, l1['ada_w'], l1['ada_b'])
    r, v, kk, k, a0, a1, lw0, lw1, g, bonus = _rwkv_pre(x2, mods1, l1, seq, tm=512)
    yf, yb = _wkv_scan(r, v, kk, k, a0, a1, lw0, lw1, l1['k_a'], batch, seq, tb=1024)
    x2 = _rwkv_post(yf, yb, bonus, g, x2, mods1, l1['ln_w'], l1['ln_b'], bf(l1['w_o']), seq, tm)
    ya, yb_ = _conv_ffn(x2, mods1, l1['norm2'], bf(l1['ffn_up']), l1['ffn_conv_w'], l1['ffn_conv_b'],
                        bf(l1['ffn_down']), final_norm, seq, tm, final_norm=True, split_rows=xa2.shape[0])
    return ya.reshape(xa.shape), yb_.reshape(xb.shape)


def kernel(x_prompt, x_sample, c_prompt, c_sample, l0_ada_w, l0_ada_b, l0_norm1, l0_norm2, l0_w_in, l0_short_w, l0_short_b, l0_filt_w1, l0_filt_b1, l0_filt_w2, l0_filt_b2, l0_filt_w3, l0_filt_b3, l0_filt_w4, l0_filt_freq, l0_filt_bias, l0_w_out, l0_ffn_up, l0_ffn_conv_w, l0_ffn_conv_b, l0_ffn_down, l1_ada_w, l1_ada_b, l1_norm1, l1_norm2, l1_mu, l1_w_r, l1_w_k, l1_w_v, l1_w_o, l1_w0, l1_w1, l1_w2, l1_a0, l1_a1, l1_a2, l1_g1, l1_g2, l1_k_k, l1_k_a, l1_r_k, l1_ln_w, l1_ln_b, l1_ffn_up, l1_ffn_conv_w, l1_ffn_conv_b, l1_ffn_down, final_norm):
    layer0 = dict(ada_w=l0_ada_w, ada_b=l0_ada_b, norm1=l0_norm1, norm2=l0_norm2, w_in=l0_w_in,
                  short_w=l0_short_w, short_b=l0_short_b, filt_w1=l0_filt_w1, filt_b1=l0_filt_b1,
                  filt_w2=l0_filt_w2, filt_b2=l0_filt_b2, filt_w3=l0_filt_w3, filt_b3=l0_filt_b3,
                  filt_w4=l0_filt_w4, filt_freq=l0_filt_freq, filt_bias=l0_filt_bias, w_out=l0_w_out,
                  ffn_up=l0_ffn_up, ffn_conv_w=l0_ffn_conv_w, ffn_conv_b=l0_ffn_conv_b, ffn_down=l0_ffn_down)
    layer1 = dict(ada_w=l1_ada_w, ada_b=l1_ada_b, norm1=l1_norm1, norm2=l1_norm2, mu=l1_mu,
                  w_r=l1_w_r, w_k=l1_w_k, w_v=l1_w_v, w_o=l1_w_o, w0=l1_w0, w1=l1_w1, w2=l1_w2,
                  a0=l1_a0, a1=l1_a1, a2=l1_a2, g1=l1_g1, g2=l1_g2, k_k=l1_k_k, k_a=l1_k_a,
                  r_k=l1_r_k.reshape(-1), ln_w=l1_ln_w, ln_b=l1_ln_b, ffn_up=l1_ffn_up,
                  ffn_conv_w=l1_ffn_conv_w, ffn_conv_b=l1_ffn_conv_b, ffn_down=l1_ffn_down)
    c = jnp.concatenate([c_prompt, c_sample], axis=0)
    return _trunk(x_prompt, x_sample, c, layer0, layer1, final_norm)
```

```python
import functools
import math

import jax
import jax.numpy as jnp
import numpy as np
from jax import lax
from jax.experimental import pallas as pl
from jax.experimental.pallas import tpu as pltpu

F32 = jnp.float32
BF16 = jnp.bfloat16
HIGHEST = lax.Precision.HIGHEST

HEAD_DIM = 64
ATTN_PATTERNS = ((128, 1), (512, 4), (2048, 16))
N_PATTERNS = 3
HEADS_PER_GROUP = 4
N_ATTN_HEADS = 12
ATTN_WIDTH = 768
GROUP_WIDTH = HEADS_PER_GROUP * HEAD_DIM
ALIBI_MAX_EXP = 8.0
HYENA_WIDTH = 256
HYENA_BANDS = 16
HYENA_EMB = 33
HYENA_TARGET = 1e-2
HYENA_FAST_DECAY = 0.3
HYENA_SLOW_DECAY = 1.5
RWKV_HEAD = 64
RMS_EPS = 1e-6
GN_EPS = 64e-5
NEG_INF = -1e30

LANES = 128
SUBLANES = 8
MXU_DIM = 256
VMEM_LIMIT = 56 * 1024 * 1024

ATTN_RADIUS = 64
assert all(window // (2 * dilation) == ATTN_RADIUS for window, dilation in ATTN_PATTERNS)
ATTN_TQ = 128
ATTN_TK = ATTN_TQ + 2 * ATTN_RADIUS
CONV_P = 256
SCAN_C = 64
SCAN_HG = 256
SCAN_UNROLL = 4


def _cparams(sem):
    return pltpu.CompilerParams(dimension_semantics=sem, vmem_limit_bytes=VMEM_LIMIT)


def _const_spec(shape):
    nd = len(shape)
    return pl.BlockSpec(shape, lambda *_: (0,) * nd, pipeline_mode=pl.Buffered(1))


def _ada_kernel(c_ref, w_ref, b_ref, o_ref):
    c = c_ref[...]
    s = c * jax.nn.sigmoid(c)
    o_ref[...] = jnp.dot(s, w_ref[...], precision=HIGHEST, preferred_element_type=F32) + b_ref[...]


def _ada_params(c_pad, w, b):
    bp, d = c_pad.shape
    n = w.shape[1]
    tn = 1024
    return pl.pallas_call(
        _ada_kernel,
        grid=(n // tn,),
        in_specs=[_const_spec((bp, d)),
                  pl.BlockSpec((d, tn), lambda j: (0, j)),
                  pl.BlockSpec((1, tn), lambda j: (0, j))],
        out_specs=pl.BlockSpec((bp, tn), lambda j: (0, j)),
        out_shape=jax.ShapeDtypeStruct((bp, n), F32),
        compiler_params=_cparams(("parallel",)),
        name="ada_params",
    )(c_pad, w, b.reshape(1, n))


def _norm_mod(x, gain, shift, scale):
    ms = jnp.mean(x * x, axis=-1, keepdims=True)
    y = x * lax.rsqrt(ms + RMS_EPS) * gain
    return y * (1.0 + scale) + shift


def _two_source_specs(tm, d, n_first):
    return [pl.BlockSpec((tm, d), lambda i: (jnp.minimum(i, n_first - 1), 0)),
            pl.BlockSpec((tm, d), lambda i: (jnp.maximum(i - n_first, 0), 0))]


def _inproj_kernel(xa_ref, xb_ref, mod_ref, gain_ref, w_ref, q0_ref, q1_ref, q2_ref, hy_ref, scr_ref, *, n_first):
    tm = xa_ref.shape[0]
    gw = 3 * GROUP_WIDTH
    from_first = pl.program_id(0) < n_first
    halves = [slice(p * (tm // 2), (p + 1) * (tm // 2)) for p in range(2)]
    hs = [_norm_mod(jnp.where(from_first, xa_ref[rows, :], xb_ref[rows, :]), gain_ref[...],
                    mod_ref[0, 0:1, :], mod_ref[0, 1:2, :]).astype(BF16) for rows in halves]
    ps = [jnp.dot(h, w_ref[...], preferred_element_type=F32) for h in hs]
    for rows, p in zip(halves, ps):
        q0_ref[rows, :] = p[:, :gw].astype(BF16)
        hy_ref[rows, :] = p[:, N_PATTERNS * gw:]
    for g, out_ref in ((1, q1_ref), (2, q2_ref)):
        dil = out_ref.shape[1]
        for cb in range(gw // LANES):
            cols = slice(cb * LANES, (cb + 1) * LANES)
            for rows, p in zip(halves, ps):
                scr_ref[cb, rows, :] = p[:, g * gw + cb * LANES:g * gw + (cb + 1) * LANES]
            for r in range(dil):
                out_ref[0, r, :, cols] = scr_ref[cb, pl.ds(r, tm // dil, stride=dil), :].astype(BF16)


def _in_proj(xa, xb, mods, gain, w_bf, seq, tm):
    d = xa.shape[1]
    rows = xa.shape[0] + xb.shape[0]
    batch = rows // seq
    n = w_bf.shape[1]
    gw = 3 * GROUP_WIDTH
    tpb = seq // tm
    dils = [dil for _, dil in ATTN_PATTERNS]
    assert dils[0] == 1

    def class_spec(dil):
        return pl.BlockSpec((1, dil, tm // dil, gw), lambda i: (i // tpb, 0, i % tpb, 0))

    return pl.pallas_call(
        functools.partial(_inproj_kernel, n_first=xa.shape[0] // tm),
        grid=(rows // tm,),
        in_specs=_two_source_specs(tm, d, xa.shape[0] // tm) + [
            pl.BlockSpec((1, 6, d), lambda i: (i // tpb, 0, 0)),
            _const_spec((1, d)),
            _const_spec((d, n))],
        out_specs=[pl.BlockSpec((tm, gw), lambda i: (i, 0)), class_spec(dils[1]), class_spec(dils[2]),
                   pl.BlockSpec((tm, n - N_PATTERNS * gw), lambda i: (i, 0))],
        out_shape=[jax.ShapeDtypeStruct((rows, gw), BF16)]
        + [jax.ShapeDtypeStruct((batch, dil, seq // dil, gw), BF16) for dil in dils[1:]]
        + [jax.ShapeDtypeStruct((rows, n - N_PATTERNS * gw), F32)],
        scratch_shapes=[pltpu.VMEM((gw // LANES, tm, LANES), F32)],
        compiler_params=_cparams(("parallel",)),
        name="in_proj",
    )(xa, xb, mods, gain.reshape(1, d), w_bf)


def _attn_kernel(q_ref, k_ref, v_ref, bias_ref, o_ref, lse_ref, kpad, vpad, *, n):
    nq = n // ATTN_TQ
    zeros = jnp.zeros((ATTN_RADIUS, GROUP_WIDTH), BF16)
    kpad[0:ATTN_RADIUS, :] = zeros
    vpad[0:ATTN_RADIUS, :] = zeros
    kpad[n + ATTN_RADIUS:n + 2 * ATTN_RADIUS, :] = zeros
    vpad[n + ATTN_RADIUS:n + 2 * ATTN_RADIUS, :] = zeros
    kpad[ATTN_RADIUS:n + ATTN_RADIUS, :] = k_ref[...]
    vpad[ATTN_RADIUS:n + ATTN_RADIUS, :] = v_ref[...]
    lane = lax.broadcasted_iota(jnp.int32, (1, LANES), 1)
    low = lane < HEAD_DIM

    blocks = 2
    assert nq % blocks == 0

    def body(it, carry):
        chains = [(blk, h) for blk in range(blocks) for h in range(HEADS_PER_GROUP)]
        idx = range(len(chains))
        i = [it * blocks + blk for blk in range(blocks)]
        r0 = [pl.multiple_of(i[blk] * ATTN_TQ, ATTN_TQ) for blk in range(blocks)]
        q = [q_ref[pl.ds(r0[blk], ATTN_TQ), :] for blk in range(blocks)]
        kw = [kpad[pl.ds(r0[blk], ATTN_TK), :] for blk in range(blocks)]
        vw = [vpad[pl.ds(r0[blk], ATTN_TK), :] for blk in range(blocks)]
        sel = [jnp.where(i[blk] == 0, 0, jnp.where(i[blk] == nq - 1, 2, 1)) for blk in range(blocks)]
        cols = [slice((h // 2) * LANES, (h // 2 + 1) * LANES) for _, h in chains]
        qm = [jnp.where(low if h % 2 == 0 else jnp.logical_not(low), q[blk][:, cols[c]], jnp.zeros((), BF16))
              for c, (blk, h) in enumerate(chains)]
        s = [lax.dot_general(qm[c], kw[chains[c][0]][:, cols[c]], (((1,), (1,)), ((), ())),
                             preferred_element_type=F32) for c in idx]
        s = [s[c] * (HEAD_DIM ** -0.5) + bias_ref[chains[c][1], sel[chains[c][0]]] for c in idx]
        m = [jnp.max(s[c], axis=-1, keepdims=True) for c in idx]
        p = [jnp.exp(s[c] - m[c]) for c in idx]
        den = [jnp.sum(p[c], axis=-1, keepdims=True) for c in idx]
        o = [jnp.dot(p[c].astype(BF16), vw[chains[c][0]][:, cols[c]], preferred_element_type=F32) for c in idx]
        o = [o[c] / den[c] for c in idx]
        lse = [jnp.broadcast_to(m[c] + jnp.log(den[c]), (ATTN_TQ, LANES)) for c in idx]
        for c in range(0, len(chains), 2):
            rows = pl.ds(r0[chains[c][0]], ATTN_TQ)
            o_ref[rows, cols[c]] = jnp.where(low, o[c], o[c + 1]).astype(o_ref.dtype)
            lse_ref[rows, cols[c]] = jnp.where(low, lse[c], lse[c + 1])
        return carry

    lax.fori_loop(0, nq // blocks, body, 0)


def _attn_bias(group, dilation):
    slopes = np.exp2(-ALIBI_MAX_EXP * (np.arange(N_ATTN_HEADS, dtype=np.float32) + 1.0) / N_ATTN_HEADS)
    slopes = slopes.reshape(N_PATTERNS, HEADS_PER_GROUP)[group].astype(np.float32)
    qi = np.arange(ATTN_TQ)[:, None]
    kj = np.arange(ATTN_TK)[None, :]
    rel = kj - ATTN_RADIUS - qi
    band = np.abs(rel) <= ATTN_RADIUS
    alibi = -slopes[:, None, None] * (np.abs(rel) * dilation).astype(np.float32)[None]
    kinds = []
    for lo, hi in ((ATTN_RADIUS, ATTN_TK), (0, ATTN_TK), (0, ATTN_TQ + ATTN_RADIUS)):
        valid = band & (kj >= lo) & (kj < hi)
        kinds.append(np.where(valid[None], alibi, np.float32(NEG_INF)))
    return jnp.asarray(np.stack(kinds, axis=1), dtype=F32)


def _attention_group(qkv, group):
    batch, dilation, n, _ = qkv.shape
    assert n % ATTN_TQ == 0 and n >= 2 * ATTN_TQ

    def part(p):
        return pl.BlockSpec((None, None, n, GROUP_WIDTH), lambda b, r: (b, r, 0, p))

    out_blk = pl.BlockSpec((None, None, n, GROUP_WIDTH), lambda b, r: (b, r, 0, 0))
    return pl.pallas_call(
        functools.partial(_attn_kernel, n=n),
        grid=(batch, dilation),
        in_specs=[part(0), part(1), part(2), _const_spec((HEADS_PER_GROUP, 3, ATTN_TQ, ATTN_TK))],
        out_specs=[out_blk, out_blk],
        out_shape=[jax.ShapeDtypeStruct((batch, dilation, n, GROUP_WIDTH), BF16),
                   jax.ShapeDtypeStruct((batch, dilation, n, GROUP_WIDTH), F32)],
        scratch_shapes=[pltpu.VMEM((n + 2 * ATTN_RADIUS, GROUP_WIDTH), BF16)] * 2,
        compiler_params=_cparams(("parallel", "parallel")),
        name=f"attn_g{group}",
    )(qkv, qkv, qkv, _attn_bias(group, dilation))


def _filter_kernel(bands_ref, deltas_ref, w1_ref, b1_ref, w2_ref, b2_ref, w3_ref, b3_ref, w4_ref, freq_ref,
                   ft_ref, *, seq, tl):
    i = pl.program_id(0)
    freq = freq_ref[...]
    lane = lax.broadcasted_iota(jnp.int32, (1, LANES), 1)
    row = lax.broadcasted_iota(jnp.int32, (tl, 1), 0) + i * tl

    def half_filter(pos, half):
        posf = pos.astype(F32)
        t = posf / float(seq - 1)
        z = bands_ref[...] * (2.0 * math.pi * posf / float(seq))
        feat = jnp.where(lane == 0, t,
                         jnp.where(lane <= HYENA_BANDS, jnp.cos(z),
                                   jnp.where(lane <= 2 * HYENA_BANDS, -jnp.sin(z), 0.0)))
        h = jnp.sin(freq * (_mm(feat, w1_ref[...]) + b1_ref[...]))
        h = jnp.sin(freq * (_mm(h, w2_ref[...]) + b2_ref[...]))
        h = jnp.sin(freq * (_mm(h, w3_ref[...]) + b3_ref[...]))
        h = _mm(h, w4_ref[:, half * HYENA_WIDTH:(half + 1) * HYENA_WIDTH])
        return h * jnp.exp(-t * jnp.abs(deltas_ref[...]))

    hf = half_filter(row, 0)
    hb = half_filter(jnp.where(row == 0, 0, seq - row), 1)
    hb = jnp.where(row == 0, 0.0, hb)

    @pl.when(i == 0)
    def _():
        ft_ref[:, 0:CONV_P] = jnp.zeros((HYENA_WIDTH, CONV_P), F32)

    c0 = pl.multiple_of(CONV_P + i * tl, LANES)
    ft_ref[:, pl.ds(c0, tl)] = hb.T
    c1 = pl.multiple_of(CONV_P + seq + i * tl, LANES)
    ft_ref[:, pl.ds(c1, tl)] = hf.T

    @pl.when(i == pl.num_programs(0) - 1)
    def _():
        full = ft_ref[...]
        norm = jnp.sum(jnp.abs(full), axis=1, keepdims=True)
        ft_ref[...] = full / norm


def _hyena_filter(seq, w1, b1, w2, b2, w3, b3, w4, freq):
    tl = 512
    f = jnp.linspace(1e-4, HYENA_BANDS - 1, HYENA_BANDS, dtype=F32)
    bands = jnp.concatenate([jnp.zeros((1,), F32), f, f, jnp.zeros((LANES - HYENA_EMB,), F32)]).reshape(1, LANES)
    max_decay = math.log(HYENA_TARGET) / HYENA_FAST_DECAY
    min_decay = math.log(HYENA_TARGET) / HYENA_SLOW_DECAY
    deltas = jnp.linspace(min_decay, max_decay, HYENA_WIDTH, dtype=F32).reshape(1, HYENA_WIDTH)
    w1p = jnp.pad(w1, ((0, LANES - HYENA_EMB), (0, 0)))
    row = lambda a: a.reshape(1, -1)
    args = (bands, deltas, w1p, row(b1), w2, row(b2), w3, row(b3), w4, row(freq))
    return pl.pallas_call(
        functools.partial(_filter_kernel, seq=seq, tl=tl),
        grid=(seq // tl,),
        in_specs=[_const_spec(a.shape) for a in args],
        out_specs=pl.BlockSpec((HYENA_WIDTH, CONV_P + 2 * seq), lambda i: (0, 0)),
        out_shape=jax.ShapeDtypeStruct((HYENA_WIDTH, CONV_P + 2 * seq), F32),
        compiler_params=_cparams(("arbitrary",)),
        name="hyena_filter",
    )(*args)


def _shift_rows(x, prev_row, next_row):
    n = x.shape[0]
    row = lax.broadcasted_iota(jnp.int32, (n, 1), 0)
    xm = jnp.where(row == 0, prev_row, pltpu.roll(x, 1, axis=0))
    xp = jnp.where(row == n - 1, next_row, pltpu.roll(x, n - 1, axis=0))
    return xm, xp


def _hyena_pre_kernel(x_ref, prev_ref, next_ref, w_ref, b_ref, zt_ref, x0t_ref):
    i = pl.program_id(1)
    x = x_ref[0]
    prev_row = jnp.where(i == 0, 0.0, prev_ref[0, SUBLANES - 1:SUBLANES, :])
    next_row = jnp.where(i == pl.num_programs(1) - 1, 0.0, next_ref[0, 0:1, :])
    xm, xp = _shift_rows(x, prev_row, next_row)
    u = xm * w_ref[0:1, :] + x * w_ref[1:2, :] + xp * w_ref[2:3, :] + b_ref[...]
    c = HYENA_WIDTH
    x0, x1, v = u[:, :c], u[:, c:2 * c], u[:, 2 * c:]
    zt_ref[0] = (v * x1).T
    x0t_ref[0] = x0.T


def _hyena_pre(hy3, short_w, short_b, tl):
    batch, seq, width = hy3.shape
    nt = seq // tl
    hb = tl // SUBLANES
    last = seq // SUBLANES - 1
    out_blk = pl.BlockSpec((1, HYENA_WIDTH, tl), lambda b, i: (b, 0, i))
    return pl.pallas_call(
        _hyena_pre_kernel,
        grid=(batch, nt),
        in_specs=[pl.BlockSpec((1, tl, width), lambda b, i: (b, i, 0)),
                  pl.BlockSpec((1, SUBLANES, width), lambda b, i: (b, jnp.maximum(i * hb - 1, 0), 0)),
                  pl.BlockSpec((1, SUBLANES, width), lambda b, i: (b, jnp.minimum((i + 1) * hb, last), 0)),
                  _const_spec((3, width)), _const_spec((1, width))],
        out_specs=[out_blk, out_blk],
        out_shape=[jax.ShapeDtypeStruct((batch, HYENA_WIDTH, seq), F32)] * 2,
        compiler_params=_cparams(("parallel", "parallel")),
        name="hyena_pre",
    )(hy3, hy3, hy3, short_w, short_b.reshape(1, width))


def _hyena_conv_kernel(bias_ref, zt_ref, x0t_ref, ft_ref, o_ref, troll, zpad, *, seq, cb):
    batch = zt_ref.shape[0]
    nb = seq // CONV_P
    p = CONV_P
    g = pl.program_id(0)
    zero_margin = jnp.zeros((batch, nb, p), F32)
    zpad[:, 0:nb, :] = zero_margin
    zpad[:, 2 * nb:3 * nb, :] = zero_margin
    chunk = 1024

    def channel(ci, carry):
        for w in range(2 * seq // chunk):
            a = w * chunk
            src = ft_ref[pl.ds(ci, 1), a:a + chunk + p]
            rolled = pltpu.roll(jnp.broadcast_to(src, (p, chunk + p)), 0, axis=1, stride=1, stride_axis=0)
            troll[:, a:a + chunk] = rolled[:, p:].astype(BF16)
        z = zt_ref[:, ci, :, :]
        zpad[:, nb:2 * nb, :] = z
        acc = jnp.zeros((batch * nb, p), F32)
        for d in range(-(nb - 1), nb):
            zs = zpad[:, nb - d:2 * nb - d, :].reshape(batch * nb, p).astype(BF16)
            t = troll[:, seq + d * p:seq + (d + 1) * p]
            acc = acc + jnp.dot(zs, t, preferred_element_type=F32)
        bias = bias_ref[g * cb + ci]
        y = (acc.reshape(batch, nb, p) + z * bias) * x0t_ref[:, ci, :, :]
        o_ref[:, ci, :, :] = y
        return carry

    lax.fori_loop(0, cb, channel, 0)


def _hyena_conv(zt, x0t, ft, filt_bias, cb):
    batch, c, seq = zt.shape
    nb = seq // CONV_P
    z4 = zt.reshape(batch, c, nb, CONV_P)
    x4 = x0t.reshape(batch, c, nb, CONV_P)
    blk = pl.BlockSpec((batch, cb, nb, CONV_P), lambda g: (0, g, 0, 0))
    out = pl.pallas_call(
        functools.partial(_hyena_conv_kernel, seq=seq, cb=cb),
        grid=(c // cb,),
        in_specs=[pl.BlockSpec(memory_space=pltpu.SMEM), blk, blk,
                  pl.BlockSpec((cb, CONV_P + 2 * seq), lambda g: (g, 0))],
        out_specs=blk,
        out_shape=jax.ShapeDtypeStruct((batch, c, nb, CONV_P), F32),
        scratch_shapes=[pltpu.VMEM((CONV_P, 2 * seq), BF16),
                        pltpu.VMEM((batch, 3 * nb, CONV_P), F32)],
        compiler_params=_cparams(("parallel",)),
        name="hyena_conv",
    )(filt_bias, z4, x4, ft)
    return out.reshape(batch, c, seq)


def _interleave_classes(blk_ref, scr_ref):
    dil, per, width = blk_ref.shape
    slabs = width // LANES
    for r in range(dil):
        rows = blk_ref[r].astype(F32)
        for cb in range(slabs):
            scr_ref[cb, pl.ds(r, per, stride=dil), :] = rows[:, cb * LANES:(cb + 1) * LANES]
    return jnp.concatenate([scr_ref[cb] for cb in range(slabs)], axis=1)


def _outproj_kernel(o0, o1, o2, l0, l1, l2, hyt_ref, xa_ref, xb_ref, mod_ref, wa_ref, wh_ref, out_ref,
                    so1, so2, sl1, sl2, *, n_first):
    ls = [l0[0], _interleave_classes(l1, sl1), _interleave_classes(l2, sl2)]
    os_ = [o0[0].astype(F32), _interleave_classes(o1, so1), _interleave_classes(o2, so2)]
    x = jnp.where(pl.program_id(0) < n_first, xa_ref[...], xb_ref[...])
    m = jnp.maximum(jnp.maximum(ls[0], ls[1]), ls[2])
    es = [jnp.exp(l - m) for l in ls]
    den = es[0] + es[1] + es[2]
    attn = (es[0] * os_[0] + es[1] * os_[1] + es[2] * os_[2]) / den
    hy = hyt_ref[0].T
    mix = jnp.dot(attn.astype(BF16), wa_ref[...], preferred_element_type=F32)
    mix = mix + jnp.dot(hy.astype(BF16), wh_ref[...], preferred_element_type=F32)
    out_ref[...] = x + mod_ref[0, 2:3, :] * mix


def _out_proj(os_, ls, hyt, xa, xb, mods, w_out_bf, seq, tm):
    d = xa.shape[1]
    rows = xa.shape[0] + xb.shape[0]
    tpb = seq // tm
    wa, wh = w_out_bf[:GROUP_WIDTH], w_out_bf[GROUP_WIDTH:]

    def class_spec(a):
        dil = a.shape[1]
        return pl.BlockSpec((None, dil, tm // dil, GROUP_WIDTH), lambda i: (i // tpb, 0, i % tpb, 0))

    return pl.pallas_call(
        functools.partial(_outproj_kernel, n_first=xa.shape[0] // tm),
        grid=(rows // tm,),
        in_specs=[class_spec(a) for a in (*os_, *ls)] + [
            pl.BlockSpec((1, HYENA_WIDTH, tm), lambda i: (i // tpb, 0, i % tpb))]
        + _two_source_specs(tm, d, xa.shape[0] // tm) + [
            pl.BlockSpec((1, 6, d), lambda i: (i // tpb, 0, 0)),
            _const_spec(wa.shape), _const_spec(wh.shape)],
        out_specs=pl.BlockSpec((tm, d), lambda i: (i, 0)),
        out_shape=jax.ShapeDtypeStruct((rows, d), F32),
        scratch_shapes=[pltpu.VMEM((GROUP_WIDTH // LANES, tm, LANES), F32)] * 4,
        compiler_params=_cparams(("parallel",)),
        name="out_proj",
    )(*os_, *ls, hyt, xa, xb, mods, wa, wh)


def _halo_rows(x_ref, prev_ref, next_ref, tiles_per_seq):
    i = pl.program_id(0)
    first = (i % tiles_per_seq) == 0
    last = (i % tiles_per_seq) == tiles_per_seq - 1
    xe = jnp.concatenate([prev_ref[...], x_ref[...], next_ref[...]], axis=0)
    return xe, first, last


def _ffn_kernel(x_ref, prev_ref, next_ref, mod_ref, gain_ref, wa_ref, wg_ref, cw_ref, cb_ref, wd_ref, fin_ref,
                *outs_and_scratch, tiles_per_seq, fc, final_norm, n_first):
    *out_refs, act_ref, a0_ref, g0_ref, a1_ref, g1_ref = outs_and_scratch
    tm = x_ref.shape[0]
    x = x_ref[...]
    xe, first, last = _halo_rows(x_ref, prev_ref, next_ref, tiles_per_seq)
    he32 = _norm_mod(xe, gain_ref[...], mod_ref[0, 3:4, :], mod_ref[0, 4:5, :])
    he = he32.astype(BF16)
    h = he32[SUBLANES:SUBLANES + tm].astype(BF16)
    row = lax.broadcasted_iota(jnp.int32, (tm + 2 * SUBLANES, 1), 0)
    lo = jnp.where(first, SUBLANES, 0)
    hi = jnp.where(last, tm + SUBLANES, tm + 2 * SUBLANES)
    keep = jnp.logical_and(row >= lo, row < hi)
    nchunk = wa_ref.shape[1] // fc
    n_ext = tm + 2 * SUBLANES

    def up(j, bufs):
        c0 = pl.multiple_of(j * fc, fc)
        bufs[0][...] = jnp.dot(he, wa_ref[:, pl.ds(c0, fc)], preferred_element_type=F32)
        bufs[1][...] = jnp.dot(h, wg_ref[:, pl.ds(c0, fc)], preferred_element_type=F32)

    def activate(j, bufs):
        c0 = pl.multiple_of(j * fc, fc)
        a = jnp.where(keep, bufs[0][...], 0.0)
        am = pltpu.roll(a, 1, axis=0)[SUBLANES:SUBLANES + tm]
        ap = pltpu.roll(a, n_ext - 1, axis=0)[SUBLANES:SUBLANES + tm]
        ac = a[SUBLANES:SUBLANES + tm]
        cw = cw_ref[:, pl.ds(c0, fc)]
        conv = am * cw[0:1] + ac * cw[1:2] + ap * cw[2:3] + cb_ref[:, pl.ds(c0, fc)]
        act_ref[:, pl.ds(c0, fc)] = (jax.nn.gelu(conv) * bufs[1][...]).astype(BF16)

    even, odd = (a0_ref, g0_ref), (a1_ref, g1_ref)
    up(0, even)

    def body(j, carry):
        up(2 * j + 1, odd)
        activate(2 * j, even)
        up(2 * j + 2, even)
        activate(2 * j + 1, odd)
        return carry

    assert nchunk % 2 == 1
    lax.fori_loop(0, nchunk // 2, body, 0)
    activate(nchunk - 1, even)
    down = jnp.dot(act_ref[...], wd_ref[...], preferred_element_type=F32)
    y = x + mod_ref[0, 5:6, :] * down
    if final_norm:
        ms = jnp.mean(y * y, axis=-1, keepdims=True)
        y = y * lax.rsqrt(ms + RMS_EPS) * fin_ref[...]
    if n_first is None:
        out_refs[0][...] = y
    else:
        i = pl.program_id(0)

        @pl.when(i < n_first)
        def _():
            out_refs[0][...] = y

        @pl.when(i >= n_first)
        def _():
            out_refs[1][...] = y


def _halo_specs(tm, d, nrows):
    hb = tm // SUBLANES
    last = nrows // SUBLANES - 1
    return [pl.BlockSpec((tm, d), lambda i: (i, 0)),
            pl.BlockSpec((SUBLANES, d), lambda i: (jnp.maximum(i * hb - 1, 0), 0)),
            pl.BlockSpec((SUBLANES, d), lambda i: (jnp.minimum((i + 1) * hb, last), 0))]


def _conv_ffn(x2, mods, gain, up_bf, conv_w, conv_b, down_bf, fin_gain, seq, tm, final_norm, split_rows=None):
    rows, d = x2.shape
    dff = down_bf.shape[0]
    fc = 256
    tpb = seq // tm
    wa, wg = up_bf[:, :dff], up_bf[:, dff:]
    if split_rows is None:
        n_first = None
        out_specs = pl.BlockSpec((tm, d), lambda i: (i, 0))
        out_shape = jax.ShapeDtypeStruct((rows, d), F32)
    else:
        n_first = split_rows // tm
        out_specs = _two_source_specs(tm, d, n_first)
        out_shape = [jax.ShapeDtypeStruct((split_rows, d), F32), jax.ShapeDtypeStruct((rows - split_rows, d), F32)]
    return pl.pallas_call(
        functools.partial(_ffn_kernel, tiles_per_seq=tpb, fc=fc, final_norm=final_norm, n_first=n_first),
        grid=(rows // tm,),
        in_specs=_halo_specs(tm, d, rows) + [
            pl.BlockSpec((1, 6, d), lambda i: (i // tpb, 0, 0)),
            _const_spec((1, d)), _const_spec(wa.shape), _const_spec(wg.shape),
            _const_spec((3, dff)), _const_spec((1, dff)), _const_spec(down_bf.shape), _const_spec((1, d))],
        out_specs=out_specs,
        out_shape=out_shape,
        scratch_shapes=[pltpu.VMEM((tm, dff), BF16)]
        + [pltpu.VMEM((tm + 2 * SUBLANES, fc), F32), pltpu.VMEM((tm, fc), F32)] * 2,
        compiler_params=_cparams(("arbitrary",)),
        name="conv_ffn",
    )(x2, x2, x2, mods, gain.reshape(1, d), wa, wg, conv_w, conv_b.reshape(1, dff), down_bf,
      fin_gain.reshape(1, d))


def _head_sum(x, ones_bd):
    xb = x.astype(BF16)
    parts = [jnp.dot(xb[:, c * MXU_DIM:(c + 1) * MXU_DIM], ones_bd, preferred_element_type=F32)
             for c in range(x.shape[1] // MXU_DIM)]
    return jnp.concatenate(parts, axis=1)


def _sigmoid(x):
    return 0.5 * jnp.tanh(0.5 * x) + 0.5


def _ones_blockdiag():
    r = lax.broadcasted_iota(jnp.int32, (MXU_DIM, MXU_DIM), 0) // RWKV_HEAD
    c = lax.broadcasted_iota(jnp.int32, (MXU_DIM, MXU_DIM), 1) // RWKV_HEAD
    return jnp.where(r == c, 1.0, 0.0).astype(BF16)


def _rwkv_pre_kernel(x_ref, prev_ref, next_ref, mod_ref, gain_ref, mu_ref, wr_ref, wk_ref, wv_ref,
                     w1_ref, w2_ref, w0_ref, a1_ref, a2_ref, a0_ref, g1_ref, g2_ref, kk_w_ref, ka_ref, rk_ref,
                     r_out, v_out, kk_out, k_out, a0_out, a1_out, lw0_out, lw1_out, g_out, bonus_out, he_ref, hb_ref, xx_ref,
                     *, tiles_per_seq):
    tm, d = x_ref.shape
    i = pl.program_id(0)
    first = (i % tiles_per_seq) == 0
    last = (i % tiles_per_seq) == tiles_per_seq - 1
    norm = lambda x: _norm_mod(x, gain_ref[...], mod_ref[0, 0:1, :], mod_ref[0, 1:2, :])
    ones_bd = _ones_blockdiag()
    he_ref[0:SUBLANES, :] = jnp.where(first, 0.0, norm(prev_ref[...]))
    he_ref[SUBLANES + tm:, :] = jnp.where(last, 0.0, norm(next_ref[...]))

    def row_part(lo, n):
        rows = slice(lo, lo + n)
        mid = slice(SUBLANES + lo, SUBLANES + lo + n)
        h = norm(x_ref[rows, :])
        he_ref[mid, :] = h
        hb_ref[rows, :] = h.astype(BF16)
        yield
        hm = he_ref[SUBLANES + lo - 1:SUBLANES + lo - 1 + n, :]
        hp = he_ref[SUBLANES + lo + 1:SUBLANES + lo + 1 + n, :]
        xx_ref[rows, :] = (0.5 * (hm + hp) - he_ref[mid, :]).astype(BF16)
        bf16_rows = 2 * SUBLANES

        def mixed(i):
            mu_tile = jnp.broadcast_to(mu_ref[i:i + 1, :], (bf16_rows, d)).astype(BF16)
            return hb_ref[rows, :] + xx_ref[rows, :] * jnp.tile(mu_tile, (n // bf16_rows, 1))

        wl = jnp.tanh(jnp.dot(mixed(1), w1_ref[...], preferred_element_type=F32))
        wl = jnp.dot(wl.astype(BF16), w2_ref[...], preferred_element_type=F32)
        for direction, lw_o in enumerate((lw0_out, lw1_out)):
            u = w0_ref[direction:direction + 1, :] + wl[:, direction * d:(direction + 1) * d]
            lw_o[rows, :] = -_sigmoid(u) * math.exp(-0.5)
        yield
        al = jnp.dot(mixed(4), a1_ref[...], preferred_element_type=F32)
        al = jnp.dot(al.astype(BF16), a2_ref[...], preferred_element_type=F32)
        a_dir = [_sigmoid(a0_ref[direction:direction + 1, :] + al[:, direction * d:(direction + 1) * d])
                 for direction in range(2)]
        for a, a_o in zip(a_dir, (a0_out, a1_out)):
            a_o[rows, :] = a.astype(BF16)
        yield
        gl = _sigmoid(jnp.dot(mixed(5), g1_ref[...], preferred_element_type=F32))
        g_out[rows, :] = jnp.dot(gl.astype(BF16), g2_ref[...], preferred_element_type=F32).astype(BF16)
        yield
        k = jnp.dot(mixed(2), wk_ref[...], preferred_element_type=F32)
        k_out[rows, :] = k.astype(BF16)
        kk = k * kk_w_ref[...]
        kk_out[rows, :] = (kk * jnp.minimum(lax.rsqrt(_head_sum(kk * kk, ones_bd)), 1e12)).astype(BF16)
        k_mean = 0.5 * (_k_dir(k, a_dir[0], ka_ref[...]) + _k_dir(k, a_dir[1], ka_ref[...]))
        yield
        r = jnp.dot(mixed(0), wr_ref[...], preferred_element_type=F32)
        r_out[rows, :] = r.astype(BF16)
        bonus_scale = _head_sum(r * k_mean * rk_ref[...], ones_bd)
        yield
        v = jnp.dot(mixed(3), wv_ref[...], preferred_element_type=F32)
        v_out[rows, :] = v.astype(BF16)
        bonus_out[rows, :] = (bonus_scale * v).astype(BF16)

    parts = 2
    _emit_interleaved(*[row_part(p * (tm // parts), tm // parts) for p in range(parts)])


def _k_dir(k, a, k_a):
    return k * (1.0 + (a - 1.0) * k_a)


def _blockdiag2(m):
    z = jnp.zeros_like(m[0])
    return jnp.concatenate([jnp.concatenate([m[0], z], axis=1), jnp.concatenate([z, m[1]], axis=1)], axis=0)


def _rwkv_pre(x2, mods, p, seq, tm):
    rows, d = x2.shape
    tpb = seq // tm
    bf = lambda a: a.astype(BF16)
    w1 = bf(jnp.concatenate([p['w1'][0], p['w1'][1]], axis=1))
    w2 = bf(_blockdiag2(p['w2']))
    a1 = bf(jnp.concatenate([p['a1'][0], p['a1'][1]], axis=1))
    a2 = bf(_blockdiag2(p['a2']))
    glora = p['g1'].shape[1]
    gpad = -glora % LANES
    g1 = bf(jnp.pad(p['g1'], ((0, 0), (0, gpad))))
    g2 = bf(jnp.pad(p['g2'], ((0, gpad), (0, 0))))
    row = lambda a: a.reshape(1, d)
    args = (x2, x2, x2, mods, row(p['norm1']), p['mu'], bf(p['w_r']), bf(p['w_k']), bf(p['w_v']),
            w1, w2, p['w0'], a1, a2, p['a0'], g1, g2, row(p['k_k']), row(p['k_a']), row(p['r_k']))
    in_specs = _halo_specs(tm, d, rows) + [pl.BlockSpec((1, 6, d), lambda i: (i // tpb, 0, 0))]
    in_specs += [_const_spec(a.shape) for a in args[4:]]
    ob = pl.BlockSpec((tm, d), lambda i: (i, 0))
    return pl.pallas_call(
        functools.partial(_rwkv_pre_kernel, tiles_per_seq=tpb),
        grid=(rows // tm,),
        in_specs=in_specs,
        out_specs=[ob] * 10,
        out_shape=[jax.ShapeDtypeStruct((rows, d), BF16)] * 6 + [jax.ShapeDtypeStruct((rows, d), F32)] * 2
        + [jax.ShapeDtypeStruct((rows, d), BF16)] * 2,
        scratch_shapes=[pltpu.VMEM((tm + 2 * SUBLANES, d), F32), pltpu.VMEM((tm, d), BF16),
                        pltpu.VMEM((tm, d), BF16)],
        compiler_params=_cparams(("parallel",)),
        name="rwkv_pre",
    )(*args)


def _scan_masks(reverse):
    c = SCAN_C
    assert c == RWKV_HEAD
    t = lax.broadcasted_iota(jnp.int32, (c, 4 * c), 0)
    s = lax.broadcasted_iota(jnp.int32, (c, 4 * c), 1) % c
    before = (s > t) if reverse else (s < t)
    upto = jnp.logical_or(before, s == t)
    eye = jnp.where(s == t, 1.0, 0.0)
    rr = lax.broadcasted_iota(jnp.int32, (4 * c, SCAN_HG), 0)
    cc = lax.broadcasted_iota(jnp.int32, (4 * c, SCAN_HG), 1)
    same_head = jnp.where((rr // c) == (cc // RWKV_HEAD), 1.0, 0.0)
    return before, upto, eye, same_head, same_head.astype(BF16)


def _bd(xb, head_mask_bf):
    return jnp.concatenate([xb] * 4, axis=0) * head_mask_bf


def _bd_t(x, head_mask_bf):
    xt = jnp.concatenate([x, x], axis=0).T.astype(BF16)
    return jnp.concatenate([xt, xt], axis=1) * head_mask_bf


def _mm(a, b):
    return jnp.dot(a, b, precision=HIGHEST, preferred_element_type=F32)


def _bdot(a, b):
    return jnp.dot(a, b, preferred_element_type=F32)


def _cumsum_rows(tri_bf, x):
    hi = x.astype(BF16)
    lo = (x - hi.astype(F32)).astype(BF16)
    return _bdot(tri_bf, hi) + _bdot(tri_bf, lo)


def _scan_prologue(chains):
    c = SCAN_C
    bf = lambda x: x.astype(BF16)
    idx = range(len(chains))
    r, v, kk, lw, k, a = [[ch[0][j] for ch in chains] for j in range(6)]
    head_mask_bf = chains[0][1][5]
    rev = [ch[2] for ch in chains]
    g = [_cumsum_rows(chains[i][1][0], lw[i]) for i in idx]
    gtot = [g[i][0:1] if rev[i] else g[i][c - 1:c] for i in idx]
    kkd = [kk[i] * jnp.exp(g[i] - lw[i]) for i in idx]
    rd = [r[i] * jnp.exp(g[i]) for i in idx]
    b = [kk[i] * a[i] for i in idx]
    ginv = [jnp.exp(-g[i]) for i in idx]
    lhs = [bf(jnp.concatenate([kkd[i], rd[i]], axis=0)) for i in idx]
    bi_t = [_bd_t(b[i] * ginv[i], head_mask_bf) for i in idx]
    ki_t = [_bd_t(k[i] * ginv[i], head_mask_bf) for i in idx]
    gend = [jnp.exp(gtot[i] - g[i]) for i in idx]
    upd_t = [bf(jnp.concatenate([b[i] * gend[i], k[i] * gend[i]], axis=0).T) for i in idx]
    gcol_src = [jnp.concatenate([g[i], g[i]], axis=0).T for i in idx]
    gcol = [jnp.exp(gcol_src[i][:, 0:1] if rev[i] else gcol_src[i][:, c - 1:c]) for i in idx]
    vb = [bf(v[i]) for i in idx]
    return chains, (vb, lhs, bi_t, ki_t, upd_t, gcol)


def _scan_prepare_stages(prologue, out):
    c = SCAN_C
    bf = lambda x: x.astype(BF16)
    chains, (vb, lhs, bi_t, ki_t, upd_t, gcol) = prologue
    idx = range(len(chains))
    before, upto, eye = [[ch[1][j] for ch in chains] for j in range(1, 4)]
    head_mask_bf = chains[0][1][5]

    a1 = [_bdot(lhs[i], bi_t[i]) for i in idx]
    a2 = [_bdot(lhs[i], ki_t[i]) for i in idx]
    n_ab = [jnp.where(before[i], a1[i][:c], 0.0) for i in idx]
    a_rb = [jnp.where(upto[i], a1[i][c:], 0.0) for i in idx]
    a_k = [jnp.concatenate([jnp.where(before[i], a2[i][:c], 0.0), jnp.where(upto[i], a2[i][c:], 0.0)], axis=0)
           for i in idx]
    yield

    nb = [bf(n_ab[i]) for i in idx]
    npow = [_bdot(nb[i], _bd(nb[i], head_mask_bf)) for i in idx]
    tinv = [eye[i] - n_ab[i] for i in idx]
    yield
    steps = int(math.log2(c)) - 1
    for step in range(steps):
        pb = [bf(npow[i]) for i in idx]
        rhs = [_bd(pb[i], head_mask_bf) for i in idx]
        if step == steps - 1:
            tinv = [tinv[i] + _bdot(bf(tinv[i]), rhs[i]) for i in idx]
        else:
            prod = [_bdot(jnp.concatenate([pb[i], bf(tinv[i])], axis=0), rhs[i]) for i in idx]
            tinv = [tinv[i] + prod[i][c:] for i in idx]
            npow = [prod[i][:c] for i in idx]
        yield

    av = [_bdot(bf(a_k[i]), _bd(vb[i], head_mask_bf)) for i in idx]
    yield
    rhs = [jnp.concatenate([_bd(lhs[i][:c], head_mask_bf), _bd(bf(av[i][:c]), head_mask_bf)], axis=1) for i in idx]
    tw = [_bdot(bf(tinv[i]), rhs[i]) for i in idx]
    lhs_state = [jnp.concatenate([lhs[i][c:], bf(tw[i][:, :SCAN_HG])], axis=0) for i in idx]
    wt = [tw[i][:, SCAN_HG:] for i in idx]
    out.extend((lhs_state[i], wt[i], bf(a_rb[i]), av[i][c:], upd_t[i], gcol[i], vb[i]) for i in idx)


def _scan_apply_stages(preps, states, head_mask, head_mask_bf, emit_y):
    c = SCAN_C
    for pos in range(0, len(preps), 2):
        lhs, wt, a_rb, av_r, upd_t, gcol, vb = zip(*preps[pos:pos + 2])
        o = [_bdot(lhs[i], states[i].astype(BF16)) for i in range(2)]
        yield
        ub = [(-(wt[i] + o[i][c:])).astype(BF16) for i in range(2)]
        upd = [_bdot(upd_t[i], jnp.concatenate([ub[i], vb[i]], axis=0)) for i in range(2)]
        y = [o[i][:c] + av_r[i] + _bdot(a_rb[i], _bd(ub[i], head_mask_bf)) for i in range(2)]
        yield
        for i in range(2):
            emit_y(pos + i, y[i])
            states[i] = states[i] * gcol[i] + head_mask * upd[i]


def _emit_interleaved(*stage_generators):
    pending = list(stage_generators)
    while pending:
        for gen in list(pending):
            try:
                next(gen)
            except StopIteration:
                pending.remove(gen)


def _scan_consts(reverse):
    c = SCAN_C
    before, upto, eye, head_mask, head_mask_bf = _scan_masks(reverse)
    tt = lax.broadcasted_iota(jnp.int32, (c, c), 0)
    ss = lax.broadcasted_iota(jnp.int32, (c, c), 1)
    tri = jnp.where((ss >= tt) if reverse else (ss <= tt), 1.0, 0.0).astype(BF16)
    return tri, before, upto, eye, head_mask, head_mask_bf


def _scan_kernel(ka_ref, rf_ref, vf_ref, kkf_ref, kf_ref, af_ref, lwf_ref, rb_ref, vb_ref, kkb_ref, kb_ref, ab_ref,
                 lwb_ref, yf_ref, yb_ref, sf_ref, sb_ref):
    tb = rf_ref.shape[0]
    nchunk = tb // SCAN_C

    @pl.when(pl.program_id(2) == 0)
    def _():
        sf_ref[...] = jnp.zeros_like(sf_ref)
        sb_ref[...] = jnp.zeros_like(sb_ref)

    consts_f = _scan_consts(False)
    consts_b = _scan_consts(True)
    head_mask = consts_f[4]
    k_a = ka_ref[...]

    def operands(refs, rows):
        r, v, kk, k, a = (ref[rows, :].astype(F32) for ref in refs[:5])
        return r, v, kk, refs[5][rows, :], _k_dir(k, a, k_a), a

    fwd_refs = (rf_ref, vf_ref, kkf_ref, kf_ref, af_ref, lwf_ref)
    bwd_refs = (rb_ref, vb_ref, kkb_ref, kb_ref, ab_ref, lwb_ref)

    def emitter(rows):
        def emit_y(pos, y):
            ref = yb_ref if pos % 2 else yf_ref
            ref[rows[pos], :] = y.astype(ref.dtype)
        return emit_y

    def group_rows(group):
        rows = []
        for u in range(SCAN_UNROLL):
            j = group * SCAN_UNROLL + u
            rows += [pl.ds(j * SCAN_C, SCAN_C), pl.ds((nchunk - 1 - j) * SCAN_C, SCAN_C)]
        return rows

    def group_prologue(rows):
        chains = []
        for pos, rs in enumerate(rows):
            reverse = pos % 2 == 1
            chains.append((operands(bwd_refs if reverse else fwd_refs, rs), consts_b if reverse else consts_f,
                           reverse))
        return _scan_prologue(chains)

    def run_once(fn, box):
        box.append(fn())
        yield

    ngroups = nchunk // SCAN_UNROLL
    states = [sf_ref[...], sb_ref[...]]
    pending = None
    rows = group_rows(0)
    prologue = group_prologue(rows)
    for group in range(ngroups):
        preps, next_box = [], []
        stages = [_scan_prepare_stages(prologue, preps)]
        if pending is not None:
            stages.append(_scan_apply_stages(pending[0], states, head_mask, consts_f[5], emitter(pending[1])))
        if group + 1 < ngroups:
            next_rows = group_rows(group + 1)
            stages.append(run_once(functools.partial(group_prologue, next_rows), next_box))
        _emit_interleaved(*stages)
        pending = (preps, rows)
        if group + 1 < ngroups:
            rows, prologue = next_rows, next_box[0]
    _emit_interleaved(_scan_apply_stages(pending[0], states, head_mask, consts_f[5], emitter(pending[1])))
    sf_ref[...] = states[0]
    sb_ref[...] = states[1]


def _wkv_scan(r, v, kk, k, a0, a1, lw0, lw1, k_a, batch, seq, tb):
    rows, d = r.shape
    nt = seq // tb
    ng = d // SCAN_HG
    fwd = pl.BlockSpec((tb, SCAN_HG), lambda b, g, t: (b * nt + t, g))
    bwd = pl.BlockSpec((tb, SCAN_HG), lambda b, g, t: (b * nt + nt - 1 - t, g))
    return pl.pallas_call(
        _scan_kernel,
        grid=(batch, ng, nt),
        in_specs=[pl.BlockSpec((1, SCAN_HG), lambda b, g, t: (0, g))] + [fwd] * 6 + [bwd] * 6,
        out_specs=[fwd, bwd],
        out_shape=[jax.ShapeDtypeStruct((rows, d), BF16)] * 2,
        scratch_shapes=[pltpu.VMEM((SCAN_HG, SCAN_HG), F32)] * 2,
        compiler_params=_cparams(("parallel", "parallel", "arbitrary")),
        name="wkv_scan",
    )(k_a.reshape(1, d), r, v, kk, k, a0, lw0, r, v, kk, k, a1, lw1)


def _rwkv_post_kernel(yf_ref, yb_ref, bonus_ref, g_ref, x_ref, mod_ref, lnw_ref, lnb_ref, wo_ref, out_ref):
    tm = x_ref.shape[0]
    ones_bd = _ones_blockdiag()
    halves = [slice(p * (tm // 2), (p + 1) * (tm // 2)) for p in range(2)]
    y = [yf_ref[rows, :].astype(F32) + yb_ref[rows, :].astype(F32) for rows in halves]
    mu = [_head_sum(y[p], ones_bd) * (1.0 / RWKV_HEAD) for p in range(2)]
    yc = [y[p] - mu[p] for p in range(2)]
    var = [_head_sum(yc[p] * yc[p], ones_bd) * (1.0 / RWKV_HEAD) for p in range(2)]
    yn = [yc[p] * lax.rsqrt(var[p] + GN_EPS) * lnw_ref[...] + lnb_ref[...] for p in range(2)]
    out = [((yn[p] + bonus_ref[rows, :].astype(F32)) * g_ref[rows, :].astype(F32)).astype(BF16)
           for p, rows in enumerate(halves)]
    mix = [jnp.dot(out[p], wo_ref[...], preferred_element_type=F32) for p in range(2)]
    for p, rows in enumerate(halves):
        out_ref[rows, :] = x_ref[rows, :] + mod_ref[0, 2:3, :] * mix[p]


def _rwkv_post(yf, yb, bonus, g, x2, mods, ln_w, ln_b, wo_bf, seq, tm):
    rows, d = x2.shape
    tpb = seq // tm
    rb = pl.BlockSpec((tm, d), lambda i: (i, 0))
    return pl.pallas_call(
        _rwkv_post_kernel,
        grid=(rows // tm,),
        in_specs=[rb] * 5 + [pl.BlockSpec((1, 6, d), lambda i: (i // tpb, 0, 0)),
                             _const_spec((1, d)), _const_spec((1, d)), _const_spec((d, d))],
        out_specs=rb,
        out_shape=jax.ShapeDtypeStruct((rows, d), F32),
        compiler_params=_cparams(("parallel",)),
        name="rwkv_post",
    )(yf, yb, bonus, g, x2, mods, ln_w.reshape(1, d), ln_b.reshape(1, d), wo_bf)


def _mods(c, w, b):
    batch, d = c.shape
    pad = -batch % SUBLANES
    m = _ada_params(jnp.pad(c, ((0, pad), (0, 0))), w, b)[:batch]
    return m.reshape(batch, 6, d)


def _group_major_w_in(w_in):
    d = w_in.shape[0]
    qkv = w_in[:, :3 * ATTN_WIDTH].reshape(d, 3, N_PATTERNS, GROUP_WIDTH)
    qkv = jnp.transpose(qkv, (0, 2, 1, 3)).reshape(d, 3 * ATTN_WIDTH)
    return jnp.concatenate([qkv, w_in[:, 3 * ATTN_WIDTH:]], axis=1)


def _trunk(xa, xb, c, l0, l1, final_norm):
    seq, d = xa.shape[1:]
    assert xb.shape[1:] == (seq, d)
    batch = xa.shape[0] + xb.shape[0]
    xa2, xb2 = xa.reshape(-1, d), xb.reshape(-1, d)
    bf = lambda a: a.astype(BF16)
    tm = 512

    mods0 = _mods(c, l0['ada_w'], l0['ada_b'])
    q0, q1, q2, hy = _in_proj(xa2, xb2, mods0, l0['norm1'], bf(_group_major_w_in(l0['w_in'])), seq, tm)
    os_, ls = [], []
    for group, qkv in enumerate((q0.reshape(batch, 1, seq, 3 * GROUP_WIDTH), q1, q2)):
        o, lse = _attention_group(qkv, group)
        os_.append(o)
        ls.append(lse)
    ft = _hyena_filter(seq, l0['filt_w1'], l0['filt_b1'], l0['filt_w2'], l0['filt_b2'], l0['filt_w3'],
                       l0['filt_b3'], l0['filt_w4'], l0['filt_freq'])
    zt, x0t = _hyena_pre(hy.reshape(batch, seq, 3 * HYENA_WIDTH), l0['short_w'], l0['short_b'], tl=512)
    hyt = _hyena_conv(zt, x0t, ft, l0['filt_bias'], cb=8)
    x2 = _out_proj(os_, ls, hyt, xa2, xb2, mods0, bf(l0['w_out']), seq, tm)
    x2 = _conv_ffn(x2, mods0, l0['norm2'], bf(l0['ffn_up']), l0['ffn_conv_w'], l0['ffn_conv_b'],
                   bf(l0['ffn_down']), final_norm, seq, tm, final_norm=False)

    mods1 = _mods(c, l1['ada_w'], l1['ada_b'])
    r, v, kk, k, a0, a1, lw0, lw1, g, bonus = _rwkv_pre(x2, mods1, l1, seq, tm=512)
    yf, yb = _wkv_scan(r, v, kk, k, a0, a1, lw0, lw1, l1['k_a'], batch, seq, tb=1024)
    x2 = _rwkv_post(yf, yb, bonus, g, x2, mods1, l1['ln_w'], l1['ln_b'], bf(l1['w_o']), seq, tm)
    ya, yb_ = _conv_ffn(x2, mods1, l1['norm2'], bf(l1['ffn_up']), l1['ffn_conv_w'], l1['ffn_conv_b'],
                        bf(l1['ffn_down']), final_norm, seq, tm, final_norm=True, split_rows=xa2.shape[0])
    return ya.reshape(xa.shape), yb_.reshape(xb.shape)


def kernel(x_prompt, x_sample, c_prompt, c_sample, l0_ada_w, l0_ada_b, l0_norm1, l0_norm2, l0_w_in, l0_short_w, l0_short_b, l0_filt_w1, l0_filt_b1, l0_filt_w2, l0_filt_b2, l0_filt_w3, l0_filt_b3, l0_filt_w4, l0_filt_freq, l0_filt_bias, l0_w_out, l0_ffn_up, l0_ffn_conv_w, l0_ffn_conv_b, l0_ffn_down, l1_ada_w, l1_ada_b, l1_norm1, l1_norm2, l1_mu, l1_w_r, l1_w_k, l1_w_v, l1_w_o, l1_w0, l1_w1, l1_w2, l1_a0, l1_a1, l1_a2, l1_g1, l1_g2, l1_k_k, l1_k_a, l1_r_k, l1_ln_w, l1_ln_b, l1_ffn_up, l1_ffn_conv_w, l1_ffn_conv_b, l1_ffn_down, final_norm):
    layer0 = dict(ada_w=l0_ada_w, ada_b=l0_ada_b, norm1=l0_norm1, norm2=l0_norm2, w_in=l0_w_in,
                  short_w=l0_short_w, short_b=l0_short_b, filt_w1=l0_filt_w1, filt_b1=l0_filt_b1,
                  filt_w2=l0_filt_w2, filt_b2=l0_filt_b2, filt_w3=l0_filt_w3, filt_b3=l0_filt_b3,
                  filt_w4=l0_filt_w4, filt_freq=l0_filt_freq, filt_bias=l0_filt_bias, w_out=l0_w_out,
                  ffn_up=l0_ffn_up, ffn_conv_w=l0_ffn_conv_w, ffn_conv_b=l0_ffn_conv_b, ffn_down=l0_ffn_down)
    layer1 = dict(ada_w=l1_ada_w, ada_b=l1_ada_b, norm1=l1_norm1, norm2=l1_norm2, mu=l1_mu,
                  w_r=l1_w_r, w_k=l1_w_k, w_v=l1_w_v, w_o=l1_w_o, w0=l1_w0, w1=l1_w1, w2=l1_w2,
                  a0=l1_a0, a1=l1_a1, a2=l1_a2, g1=l1_g1, g2=l1_g2, k_k=l1_k_k, k_a=l1_k_a,
                  r_k=l1_r_k.reshape(-1), ln_w=l1_ln_w, ln_b=l1_ln_b, ffn_up=l1_ffn_up,
                  ffn_conv_w=l1_ffn_conv_w, ffn_conv_b=l1_ffn_conv_b, ffn_down=l1_ffn_down)
    c = jnp.concatenate([c_prompt, c_sample], axis=0)
    return _trunk(x_prompt, x_sample, c, layer0, layer1, final_norm)
```

```python
import functools
import math

import jax
import jax.numpy as jnp
import numpy as np
from jax import lax
from jax.experimental import pallas as pl
from jax.experimental.pallas import tpu as pltpu

F32 = jnp.float32
BF16 = jnp.bfloat16
HIGHEST = lax.Precision.HIGHEST

HEAD_DIM = 64
ATTN_PATTERNS = ((128, 1), (512, 4), (2048, 16))
N_PATTERNS = 3
HEADS_PER_GROUP = 4
N_ATTN_HEADS = 12
ATTN_WIDTH = 768
GROUP_WIDTH = HEADS_PER_GROUP * HEAD_DIM
ALIBI_MAX_EXP = 8.0
HYENA_WIDTH = 256
HYENA_BANDS = 16
HYENA_EMB = 33
HYENA_TARGET = 1e-2
HYENA_FAST_DECAY = 0.3
HYENA_SLOW_DECAY = 1.5
RWKV_HEAD = 64
RMS_EPS = 1e-6
GN_EPS = 64e-5
NEG_INF = -1e30

LANES = 128
SUBLANES = 8
MXU_DIM = 256
VMEM_LIMIT = 56 * 1024 * 1024

ATTN_RADIUS = 64
assert all(window // (2 * dilation) == ATTN_RADIUS for window, dilation in ATTN_PATTERNS)
ATTN_TQ = 128
ATTN_TK = ATTN_TQ + 2 * ATTN_RADIUS
CONV_P = 256
SCAN_C = 64
SCAN_HG = 256
SCAN_UNROLL = 4


def _cparams(sem):
    return pltpu.CompilerParams(dimension_semantics=sem, vmem_limit_bytes=VMEM_LIMIT)


def _const_spec(shape):
    nd = len(shape)
    return pl.BlockSpec(shape, lambda *_: (0,) * nd, pipeline_mode=pl.Buffered(1))


def _ada_kernel(c_ref, w_ref, b_ref, o_ref):
    c = c_ref[...]
    s = c * jax.nn.sigmoid(c)
    o_ref[...] = jnp.dot(s, w_ref[...], precision=HIGHEST, preferred_element_type=F32) + b_ref[...]


def _ada_params(c_pad, w, b):
    bp, d = c_pad.shape
    n = w.shape[1]
    tn = 1024
    return pl.pallas_call(
        _ada_kernel,
        grid=(n // tn,),
        in_specs=[_const_spec((bp, d)),
                  pl.BlockSpec((d, tn), lambda j: (0, j)),
                  pl.BlockSpec((1, tn), lambda j: (0, j))],
        out_specs=pl.BlockSpec((bp, tn), lambda j: (0, j)),
        out_shape=jax.ShapeDtypeStruct((bp, n), F32),
        compiler_params=_cparams(("parallel",)),
        name="ada_params",
    )(c_pad, w, b.reshape(1, n))


def _norm_mod(x, gain, shift, scale):
    ms = jnp.mean(x * x, axis=-1, keepdims=True)
    y = x * lax.rsqrt(ms + RMS_EPS) * gain
    return y * (1.0 + scale) + shift


def _two_source_specs(tm, d, n_first):
    return [pl.BlockSpec((tm, d), lambda i: (jnp.minimum(i, n_first - 1), 0)),
            pl.BlockSpec((tm, d), lambda i: (jnp.maximum(i - n_first, 0), 0))]


def _inproj_kernel(xa_ref, xb_ref, mod_ref, gain_ref, w_ref, q0_ref, q1_ref, q2_ref, hy_ref, scr_ref, *, n_first):
    tm = xa_ref.shape[0]
    gw = 3 * GROUP_WIDTH
    from_first = pl.program_id(0) < n_first
    halves = [slice(p * (tm // 2), (p + 1) * (tm // 2)) for p in range(2)]
    hs = [_norm_mod(jnp.where(from_first, xa_ref[rows, :], xb_ref[rows, :]), gain_ref[...],
                    mod_ref[0, 0:1, :], mod_ref[0, 1:2, :]).astype(BF16) for rows in halves]
    ps = [jnp.dot(h, w_ref[...], preferred_element_type=F32) for h in hs]
    for rows, p in zip(halves, ps):
        q0_ref[rows, :] = p[:, :gw].astype(BF16)
        hy_ref[rows, :] = p[:, N_PATTERNS * gw:]
    for g, out_ref in ((1, q1_ref), (2, q2_ref)):
        dil = out_ref.shape[1]
        for cb in range(gw // LANES):
            cols = slice(cb * LANES, (cb + 1) * LANES)
            for rows, p in zip(halves, ps):
                scr_ref[cb, rows, :] = p[:, g * gw + cb * LANES:g * gw + (cb + 1) * LANES]
            for r in range(dil):
                out_ref[0, r, :, cols] = scr_ref[cb, pl.ds(r, tm // dil, stride=dil), :].astype(BF16)


def _in_proj(xa, xb, mods, gain, w_bf, seq, tm):
    d = xa.shape[1]
    rows = xa.shape[0] + xb.shape[0]
    batch = rows // seq
    n = w_bf.shape[1]
    gw = 3 * GROUP_WIDTH
    tpb = seq // tm
    dils = [dil for _, dil in ATTN_PATTERNS]
    assert dils[0] == 1

    def class_spec(dil):
        return pl.BlockSpec((1, dil, tm // dil, gw), lambda i: (i // tpb, 0, i % tpb, 0))

    return pl.pallas_call(
        functools.partial(_inproj_kernel, n_first=xa.shape[0] // tm),
        grid=(rows // tm,),
        in_specs=_two_source_specs(tm, d, xa.shape[0] // tm) + [
            pl.BlockSpec((1, 6, d), lambda i: (i // tpb, 0, 0)),
            _const_spec((1, d)),
            _const_spec((d, n))],
        out_specs=[pl.BlockSpec((tm, gw), lambda i: (i, 0)), class_spec(dils[1]), class_spec(dils[2]),
                   pl.BlockSpec((tm, n - N_PATTERNS * gw), lambda i: (i, 0))],
        out_shape=[jax.ShapeDtypeStruct((rows, gw), BF16)]
        + [jax.ShapeDtypeStruct((batch, dil, seq // dil, gw), BF16) for dil in dils[1:]]
        + [jax.ShapeDtypeStruct((rows, n - N_PATTERNS * gw), F32)],
        scratch_shapes=[pltpu.VMEM((gw // LANES, tm, LANES), F32)],
        compiler_params=_cparams(("parallel",)),
        name="in_proj",
    )(xa, xb, mods, gain.reshape(1, d), w_bf)


def _attn_kernel(q_ref, k_ref, v_ref, bias_ref, o_ref, lse_ref, kpad, vpad, *, n):
    nq = n // ATTN_TQ
    zeros = jnp.zeros((ATTN_RADIUS, GROUP_WIDTH), BF16)
    kpad[0:ATTN_RADIUS, :] = zeros
    vpad[0:ATTN_RADIUS, :] = zeros
    kpad[n + ATTN_RADIUS:n + 2 * ATTN_RADIUS, :] = zeros
    vpad[n + ATTN_RADIUS:n + 2 * ATTN_RADIUS, :] = zeros
    kpad[ATTN_RADIUS:n + ATTN_RADIUS, :] = k_ref[...]
    vpad[ATTN_RADIUS:n + ATTN_RADIUS, :] = v_ref[...]
    lane = lax.broadcasted_iota(jnp.int32, (1, LANES), 1)
    low = lane < HEAD_DIM

    blocks = 2
    assert nq % blocks == 0

    def body(it, carry):
        chains = [(blk, h) for blk in range(blocks) for h in range(HEADS_PER_GROUP)]
        idx = range(len(chains))
        i = [it * blocks + blk for blk in range(blocks)]
        r0 = [pl.multiple_of(i[blk] * ATTN_TQ, ATTN_TQ) for blk in range(blocks)]
        q = [q_ref[pl.ds(r0[blk], ATTN_TQ), :] for blk in range(blocks)]
        kw = [kpad[pl.ds(r0[blk], ATTN_TK), :] for blk in range(blocks)]
        vw = [vpad[pl.ds(r0[blk], ATTN_TK), :] for blk in range(blocks)]
        sel = [jnp.where(i[blk] == 0, 0, jnp.where(i[blk] == nq - 1, 2, 1)) for blk in range(blocks)]
        cols = [slice((h // 2) * LANES, (h // 2 + 1) * LANES) for _, h in chains]
        qm = [jnp.where(low if h % 2 == 0 else jnp.logical_not(low), q[blk][:, cols[c]], jnp.zeros((), BF16))
              for c, (blk, h) in enumerate(chains)]
        s = [lax.dot_general(qm[c], kw[chains[c][0]][:, cols[c]], (((1,), (1,)), ((), ())),
                             preferred_element_type=F32) for c in idx]
        s = [s[c] * (HEAD_DIM ** -0.5) + bias_ref[chains[c][1], sel[chains[c][0]]] for c in idx]
        m = [jnp.max(s[c], axis=-1, keepdims=True) for c in idx]
        p = [jnp.exp(s[c] - m[c]) for c in idx]
        den = [jnp.sum(p[c], axis=-1, keepdims=True) for c in idx]
        o = [jnp.dot(p[c].astype(BF16), vw[chains[c][0]][:, cols[c]], preferred_element_type=F32) for c in idx]
        o = [o[c] / den[c] for c in idx]
        lse = [jnp.broadcast_to(m[c] + jnp.log(den[c]), (ATTN_TQ, LANES)) for c in idx]
        for c in range(0, len(chains), 2):
            rows = pl.ds(r0[chains[c][0]], ATTN_TQ)
            o_ref[rows, cols[c]] = jnp.where(low, o[c], o[c + 1]).astype(o_ref.dtype)
            lse_ref[rows, cols[c]] = jnp.where(low, lse[c], lse[c + 1])
        return carry

    lax.fori_loop(0, nq // blocks, body, 0)


def _attn_bias(group, dilation):
    slopes = np.exp2(-ALIBI_MAX_EXP * (np.arange(N_ATTN_HEADS, dtype=np.float32) + 1.0) / N_ATTN_HEADS)
    slopes = slopes.reshape(N_PATTERNS, HEADS_PER_GROUP)[group].astype(np.float32)
    qi = np.arange(ATTN_TQ)[:, None]
    kj = np.arange(ATTN_TK)[None, :]
    rel = kj - ATTN_RADIUS - qi
    band = np.abs(rel) <= ATTN_RADIUS
    alibi = -slopes[:, None, None] * (np.abs(rel) * dilation).astype(np.float32)[None]
    kinds = []
    for lo, hi in ((ATTN_RADIUS, ATTN_TK), (0, ATTN_TK), (0, ATTN_TQ + ATTN_RADIUS)):
        valid = band & (kj >= lo) & (kj < hi)
        kinds.append(np.where(valid[None], alibi, np.float32(NEG_INF)))
    return jnp.asarray(np.stack(kinds, axis=1), dtype=F32)


def _attention_group(qkv, group):
    batch, dilation, n, _ = qkv.shape
    assert n % ATTN_TQ == 0 and n >= 2 * ATTN_TQ

    def part(p):
        return pl.BlockSpec((None, None, n, GROUP_WIDTH), lambda b, r: (b, r, 0, p))

    out_blk = pl.BlockSpec((None, None, n, GROUP_WIDTH), lambda b, r: (b, r, 0, 0))
    return pl.pallas_call(
        functools.partial(_attn_kernel, n=n),
        grid=(batch, dilation),
        in_specs=[part(0), part(1), part(2), _const_spec((HEADS_PER_GROUP, 3, ATTN_TQ, ATTN_TK))],
        out_specs=[out_blk, out_blk],
        out_shape=[jax.ShapeDtypeStruct((batch, dilation, n, GROUP_WIDTH), BF16),
                   jax.ShapeDtypeStruct((batch, dilation, n, GROUP_WIDTH), F32)],
        scratch_shapes=[pltpu.VMEM((n + 2 * ATTN_RADIUS, GROUP_WIDTH), BF16)] * 2,
        compiler_params=_cparams(("parallel", "parallel")),
        name=f"attn_g{group}",
    )(qkv, qkv, qkv, _attn_bias(group, dilation))


def _filter_kernel(bands_ref, deltas_ref, w1_ref, b1_ref, w2_ref, b2_ref, w3_ref, b3_ref, w4_ref, freq_ref,
                   ft_ref, *, seq, tl):
    i = pl.program_id(0)
    freq = freq_ref[...]
    lane = lax.broadcasted_iota(jnp.int32, (1, LANES), 1)
    row = lax.broadcasted_iota(jnp.int32, (tl, 1), 0) + i * tl

    def half_filter(pos, half):
        posf = pos.astype(F32)
        t = posf / float(seq - 1)
        z = bands_ref[...] * (2.0 * math.pi * posf / float(seq))
        feat = jnp.where(lane == 0, t,
                         jnp.where(lane <= HYENA_BANDS, jnp.cos(z),
                                   jnp.where(lane <= 2 * HYENA_BANDS, -jnp.sin(z), 0.0)))
        h = jnp.sin(freq * (_mm(feat, w1_ref[...]) + b1_ref[...]))
        h = jnp.sin(freq * (_mm(h, w2_ref[...]) + b2_ref[...]))
        h = jnp.sin(freq * (_mm(h, w3_ref[...]) + b3_ref[...]))
        h = _mm(h, w4_ref[:, half * HYENA_WIDTH:(half + 1) * HYENA_WIDTH])
        return h * jnp.exp(-t * jnp.abs(deltas_ref[...]))

    hf = half_filter(row, 0)
    hb = half_filter(jnp.where(row == 0, 0, seq - row), 1)
    hb = jnp.where(row == 0, 0.0, hb)

    @pl.when(i == 0)
    def _():
        ft_ref[:, 0:CONV_P] = jnp.zeros((HYENA_WIDTH, CONV_P), F32)

    c0 = pl.multiple_of(CONV_P + i * tl, LANES)
    ft_ref[:, pl.ds(c0, tl)] = hb.T
    c1 = pl.multiple_of(CONV_P + seq + i * tl, LANES)
    ft_ref[:, pl.ds(c1, tl)] = hf.T

    @pl.when(i == pl.num_programs(0) - 1)
    def _():
        full = ft_ref[...]
        norm = jnp.sum(jnp.abs(full), axis=1, keepdims=True)
        ft_ref[...] = full / norm


def _hyena_filter(seq, w1, b1, w2, b2, w3, b3, w4, freq):
    tl = 512
    f = jnp.linspace(1e-4, HYENA_BANDS - 1, HYENA_BANDS, dtype=F32)
    bands = jnp.concatenate([jnp.zeros((1,), F32), f, f, jnp.zeros((LANES - HYENA_EMB,), F32)]).reshape(1, LANES)
    max_decay = math.log(HYENA_TARGET) / HYENA_FAST_DECAY
    min_decay = math.log(HYENA_TARGET) / HYENA_SLOW_DECAY
    deltas = jnp.linspace(min_decay, max_decay, HYENA_WIDTH, dtype=F32).reshape(1, HYENA_WIDTH)
    w1p = jnp.pad(w1, ((0, LANES - HYENA_EMB), (0, 0)))
    row = lambda a: a.reshape(1, -1)
    args = (bands, deltas, w1p, row(b1), w2, row(b2), w3, row(b3), w4, row(freq))
    return pl.pallas_call(
        functools.partial(_filter_kernel, seq=seq, tl=tl),
        grid=(seq // tl,),
        in_specs=[_const_spec(a.shape) for a in args],
        out_specs=pl.BlockSpec((HYENA_WIDTH, CONV_P + 2 * seq), lambda i: (0, 0)),
        out_shape=jax.ShapeDtypeStruct((HYENA_WIDTH, CONV_P + 2 * seq), F32),
        compiler_params=_cparams(("arbitrary",)),
        name="hyena_filter",
    )(*args)


def _shift_rows(x, prev_row, next_row):
    n = x.shape[0]
    row = lax.broadcasted_iota(jnp.int32, (n, 1), 0)
    xm = jnp.where(row == 0, prev_row, pltpu.roll(x, 1, axis=0))
    xp = jnp.where(row == n - 1, next_row, pltpu.roll(x, n - 1, axis=0))
    return xm, xp


def _hyena_pre_kernel(x_ref, prev_ref, next_ref, w_ref, b_ref, zt_ref, x0t_ref):
    i = pl.program_id(1)
    x = x_ref[0]
    prev_row = jnp.where(i == 0, 0.0, prev_ref[0, SUBLANES - 1:SUBLANES, :])
    next_row = jnp.where(i == pl.num_programs(1) - 1, 0.0, next_ref[0, 0:1, :])
    xm, xp = _shift_rows(x, prev_row, next_row)
    u = xm * w_ref[0:1, :] + x * w_ref[1:2, :] + xp * w_ref[2:3, :] + b_ref[...]
    c = HYENA_WIDTH
    x0, x1, v = u[:, :c], u[:, c:2 * c], u[:, 2 * c:]
    zt_ref[0] = (v * x1).T
    x0t_ref[0] = x0.T


def _hyena_pre(hy3, short_w, short_b, tl):
    batch, seq, width = hy3.shape
    nt = seq // tl
    hb = tl // SUBLANES
    last = seq // SUBLANES - 1
    out_blk = pl.BlockSpec((1, HYENA_WIDTH, tl), lambda b, i: (b, 0, i))
    return pl.pallas_call(
        _hyena_pre_kernel,
        grid=(batch, nt),
        in_specs=[pl.BlockSpec((1, tl, width), lambda b, i: (b, i, 0)),
                  pl.BlockSpec((1, SUBLANES, width), lambda b, i: (b, jnp.maximum(i * hb - 1, 0), 0)),
                  pl.BlockSpec((1, SUBLANES, width), lambda b, i: (b, jnp.minimum((i + 1) * hb, last), 0)),
                  _const_spec((3, width)), _const_spec((1, width))],
        out_specs=[out_blk, out_blk],
        out_shape=[jax.ShapeDtypeStruct((batch, HYENA_WIDTH, seq), F32)] * 2,
        compiler_params=_cparams(("parallel", "parallel")),
        name="hyena_pre",
    )(hy3, hy3, hy3, short_w, short_b.reshape(1, width))


def _hyena_conv_kernel(bias_ref, zt_ref, x0t_ref, ft_ref, o_ref, troll, zpad, *, seq, cb):
    batch = zt_ref.shape[0]
    nb = seq // CONV_P
    p = CONV_P
    g = pl.program_id(0)
    zero_margin = jnp.zeros((batch, nb, p), F32)
    zpad[:, 0:nb, :] = zero_margin
    zpad[:, 2 * nb:3 * nb, :] = zero_margin
    chunk = 1024

    def channel(ci, carry):
        for w in range(2 * seq // chunk):
            a = w * chunk
            src = ft_ref[pl.ds(ci, 1), a:a + chunk + p]
            rolled = pltpu.roll(jnp.broadcast_to(src, (p, chunk + p)), 0, axis=1, stride=1, stride_axis=0)
            troll[:, a:a + chunk] = rolled[:, p:].astype(BF16)
        z = zt_ref[:, ci, :, :]
        zpad[:, nb:2 * nb, :] = z
        acc = jnp.zeros((batch * nb, p), F32)
        for d in range(-(nb - 1), nb):
            zs = zpad[:, nb - d:2 * nb - d, :].reshape(batch * nb, p).astype(BF16)
            t = troll[:, seq + d * p:seq + (d + 1) * p]
            acc = acc + jnp.dot(zs, t, preferred_element_type=F32)
        bias = bias_ref[g * cb + ci]
        y = (acc.reshape(batch, nb, p) + z * bias) * x0t_ref[:, ci, :, :]
        o_ref[:, ci, :, :] = y
        return carry

    lax.fori_loop(0, cb, channel, 0)


def _hyena_conv(zt, x0t, ft, filt_bias, cb):
    batch, c, seq = zt.shape
    nb = seq // CONV_P
    z4 = zt.reshape(batch, c, nb, CONV_P)
    x4 = x0t.reshape(batch, c, nb, CONV_P)
    blk = pl.BlockSpec((batch, cb, nb, CONV_P), lambda g: (0, g, 0, 0))
    out = pl.pallas_call(
        functools.partial(_hyena_conv_kernel, seq=seq, cb=cb),
        grid=(c // cb,),
        in_specs=[pl.BlockSpec(memory_space=pltpu.SMEM), blk, blk,
                  pl.BlockSpec((cb, CONV_P + 2 * seq), lambda g: (g, 0))],
        out_specs=blk,
        out_shape=jax.ShapeDtypeStruct((batch, c, nb, CONV_P), F32),
        scratch_shapes=[pltpu.VMEM((CONV_P, 2 * seq), BF16),
                        pltpu.VMEM((batch, 3 * nb, CONV_P), F32)],
        compiler_params=_cparams(("parallel",)),
        name="hyena_conv",
    )(filt_bias, z4, x4, ft)
    return out.reshape(batch, c, seq)


def _interleave_classes(blk_ref, scr_ref):
    dil, per, width = blk_ref.shape
    slabs = width // LANES
    for r in range(dil):
        rows = blk_ref[r].astype(F32)
        for cb in range(slabs):
            scr_ref[cb, pl.ds(r, per, stride=dil), :] = rows[:, cb * LANES:(cb + 1) * LANES]
    return jnp.concatenate([scr_ref[cb] for cb in range(slabs)], axis=1)


def _outproj_kernel(o0, o1, o2, l0, l1, l2, hyt_ref, xa_ref, xb_ref, mod_ref, wa_ref, wh_ref, out_ref,
                    so1, so2, sl1, sl2, *, n_first):
    ls = [l0[0], _interleave_classes(l1, sl1), _interleave_classes(l2, sl2)]
    os_ = [o0[0].astype(F32), _interleave_classes(o1, so1), _interleave_classes(o2, so2)]
    x = jnp.where(pl.program_id(0) < n_first, xa_ref[...], xb_ref[...])
    m = jnp.maximum(jnp.maximum(ls[0], ls[1]), ls[2])
    es = [jnp.exp(l - m) for l in ls]
    den = es[0] + es[1] + es[2]
    attn = (es[0] * os_[0] + es[1] * os_[1] + es[2] * os_[2]) / den
    hy = hyt_ref[0].T
    mix = jnp.dot(attn.astype(BF16), wa_ref[...], preferred_element_type=F32)
    mix = mix + jnp.dot(hy.astype(BF16), wh_ref[...], preferred_element_type=F32)
    out_ref[...] = x + mod_ref[0, 2:3, :] * mix


def _out_proj(os_, ls, hyt, xa, xb, mods, w_out_bf, seq, tm):
    d = xa.shape[1]
    rows = xa.shape[0] + xb.shape[0]
    tpb = seq // tm
    wa, wh = w_out_bf[:GROUP_WIDTH], w_out_bf[GROUP_WIDTH:]

    def class_spec(a):
        dil = a.shape[1]
        return pl.BlockSpec((None, dil, tm // dil, GROUP_WIDTH), lambda i: (i // tpb, 0, i % tpb, 0))

    return pl.pallas_call(
        functools.partial(_outproj_kernel, n_first=xa.shape[0] // tm),
        grid=(rows // tm,),
        in_specs=[class_spec(a) for a in (*os_, *ls)] + [
            pl.BlockSpec((1, HYENA_WIDTH, tm), lambda i: (i // tpb, 0, i % tpb))]
        + _two_source_specs(tm, d, xa.shape[0] // tm) + [
            pl.BlockSpec((1, 6, d), lambda i: (i // tpb, 0, 0)),
            _const_spec(wa.shape), _const_spec(wh.shape)],
        out_specs=pl.BlockSpec((tm, d), lambda i: (i, 0)),
        out_shape=jax.ShapeDtypeStruct((rows, d), F32),
        scratch_shapes=[pltpu.VMEM((GROUP_WIDTH // LANES, tm, LANES), F32)] * 4,
        compiler_params=_cparams(("parallel",)),
        name="out_proj",
    )(*os_, *ls, hyt, xa, xb, mods, wa, wh)


def _halo_rows(x_ref, prev_ref, next_ref, tiles_per_seq):
    i = pl.program_id(0)
    first = (i % tiles_per_seq) == 0
    last = (i % tiles_per_seq) == tiles_per_seq - 1
    xe = jnp.concatenate([prev_ref[...], x_ref[...], next_ref[...]], axis=0)
    return xe, first, last


def _ffn_kernel(x_ref, prev_ref, next_ref, mod_ref, gain_ref, wa_ref, wg_ref, cw_ref, cb_ref, wd_ref, fin_ref,
                *outs_and_scratch, tiles_per_seq, fc, final_norm, n_first):
    *out_refs, act_ref, a0_ref, g0_ref, a1_ref, g1_ref = outs_and_scratch
    tm = x_ref.shape[0]
    x = x_ref[...]
    xe, first, last = _halo_rows(x_ref, prev_ref, next_ref, tiles_per_seq)
    he32 = _norm_mod(xe, gain_ref[...], mod_ref[0, 3:4, :], mod_ref[0, 4:5, :])
    he = he32.astype(BF16)
    h = he32[SUBLANES:SUBLANES + tm].astype(BF16)
    row = lax.broadcasted_iota(jnp.int32, (tm + 2 * SUBLANES, 1), 0)
    lo = jnp.where(first, SUBLANES, 0)
    hi = jnp.where(last, tm + SUBLANES, tm + 2 * SUBLANES)
    keep = jnp.logical_and(row >= lo, row < hi)
    nchunk = wa_ref.shape[1] // fc
    n_ext = tm + 2 * SUBLANES

    def up(j, bufs):
        c0 = pl.multiple_of(j * fc, fc)
        bufs[0][...] = jnp.dot(he, wa_ref[:, pl.ds(c0, fc)], preferred_element_type=F32)
        bufs[1][...] = jnp.dot(h, wg_ref[:, pl.ds(c0, fc)], preferred_element_type=F32)

    def activate(j, bufs):
        c0 = pl.multiple_of(j * fc, fc)
        a = jnp.where(keep, bufs[0][...], 0.0)
        am = pltpu.roll(a, 1, axis=0)[SUBLANES:SUBLANES + tm]
        ap = pltpu.roll(a, n_ext - 1, axis=0)[SUBLANES:SUBLANES + tm]
        ac = a[SUBLANES:SUBLANES + tm]
        cw = cw_ref[:, pl.ds(c0, fc)]
        conv = am * cw[0:1] + ac * cw[1:2] + ap * cw[2:3] + cb_ref[:, pl.ds(c0, fc)]
        act_ref[:, pl.ds(c0, fc)] = (jax.nn.gelu(conv) * bufs[1][...]).astype(BF16)

    even, odd = (a0_ref, g0_ref), (a1_ref, g1_ref)
    up(0, even)

    def body(j, carry):
        up(2 * j + 1, odd)
        activate(2 * j, even)
        up(2 * j + 2, even)
        activate(2 * j + 1, odd)
        return carry

    assert nchunk % 2 == 1
    lax.fori_loop(0, nchunk // 2, body, 0)
    activate(nchunk - 1, even)
    down = jnp.dot(act_ref[...], wd_ref[...], preferred_element_type=F32)
    y = x + mod_ref[0, 5:6, :] * down
    if final_norm:
        ms = jnp.mean(y * y, axis=-1, keepdims=True)
        y = y * lax.rsqrt(ms + RMS_EPS) * fin_ref[...]
    if n_first is None:
        out_refs[0][...] = y
    else:
        i = pl.program_id(0)

        @pl.when(i < n_first)
        def _():
            out_refs[0][...] = y

        @pl.when(i >= n_first)
        def _():
            out_refs[1][...] = y


def _halo_specs(tm, d, nrows):
    hb = tm // SUBLANES
    last = nrows // SUBLANES - 1
    return [pl.BlockSpec((tm, d), lambda i: (i, 0)),
            pl.BlockSpec((SUBLANES, d), lambda i: (jnp.maximum(i * hb - 1, 0), 0)),
            pl.BlockSpec((SUBLANES, d), lambda i: (jnp.minimum((i + 1) * hb, last), 0))]


def _conv_ffn(x2, mods, gain, up_bf, conv_w, conv_b, down_bf, fin_gain, seq, tm, final_norm, split_rows=None):
    rows, d = x2.shape
    dff = down_bf.shape[0]
    fc = 256
    tpb = seq // tm
    wa, wg = up_bf[:, :dff], up_bf[:, dff:]
    if split_rows is None:
        n_first = None
        out_specs = pl.BlockSpec((tm, d), lambda i: (i, 0))
        out_shape = jax.ShapeDtypeStruct((rows, d), F32)
    else:
        n_first = split_rows // tm
        out_specs = _two_source_specs(tm, d, n_first)
        out_shape = [jax.ShapeDtypeStruct((split_rows, d), F32), jax.ShapeDtypeStruct((rows - split_rows, d), F32)]
    return pl.pallas_call(
        functools.partial(_ffn_kernel, tiles_per_seq=tpb, fc=fc, final_norm=final_norm, n_first=n_first),
        grid=(rows // tm,),
        in_specs=_halo_specs(tm, d, rows) + [
            pl.BlockSpec((1, 6, d), lambda i: (i // tpb, 0, 0)),
            _const_spec((1, d)), _const_spec(wa.shape), _const_spec(wg.shape),
            _const_spec((3, dff)), _const_spec((1, dff)), _const_spec(down_bf.shape), _const_spec((1, d))],
        out_specs=out_specs,
        out_shape=out_shape,
        scratch_shapes=[pltpu.VMEM((tm, dff), BF16)]
        + [pltpu.VMEM((tm + 2 * SUBLANES, fc), F32), pltpu.VMEM((tm, fc), F32)] * 2,
        compiler_params=_cparams(("arbitrary",)),
        name="conv_ffn",
    )(x2, x2, x2, mods, gain.reshape(1, d), wa, wg, conv_w, conv_b.reshape(1, dff), down_bf,
      fin_gain.reshape(1, d))


def _head_sum(x, ones_bd):
    xb = x.astype(BF16)
    parts = [jnp.dot(xb[:, c * MXU_DIM:(c + 1) * MXU_DIM], ones_bd, preferred_element_type=F32)
             for c in range(x.shape[1] // MXU_DIM)]
    return jnp.concatenate(parts, axis=1)


def _sigmoid(x):
    return 0.5 * jnp.tanh(0.5 * x) + 0.5


def _ones_blockdiag():
    r = lax.broadcasted_iota(jnp.int32, (MXU_DIM, MXU_DIM), 0) // RWKV_HEAD
    c = lax.broadcasted_iota(jnp.int32, (MXU_DIM, MXU_DIM), 1) // RWKV_HEAD
    return jnp.where(r == c, 1.0, 0.0).astype(BF16)


def _rwkv_pre_kernel(x_ref, prev_ref, next_ref, mod_ref, gain_ref, mu_ref, wr_ref, wk_ref, wv_ref,
                     w1_ref, w2_ref, w0_ref, a1_ref, a2_ref, a0_ref, g1_ref, g2_ref, kk_w_ref, ka_ref, rk_ref,
                     r_out, v_out, kk_out, k_out, a0_out, a1_out, lw0_out, lw1_out, g_out, bonus_out, he_ref, hb_ref, xx_ref,
                     *, tiles_per_seq):
    tm, d = x_ref.shape
    i = pl.program_id(0)
    first = (i % tiles_per_seq) == 0
    last = (i % tiles_per_seq) == tiles_per_seq - 1
    norm = lambda x: _norm_mod(x, gain_ref[...], mod_ref[0, 0:1, :], mod_ref[0, 1:2, :])
    ones_bd = _ones_blockdiag()
    he_ref[0:SUBLANES, :] = jnp.where(first, 0.0, norm(prev_ref[...]))
    he_ref[SUBLANES + tm:, :] = jnp.where(last, 0.0, norm(next_ref[...]))

    def row_part(lo, n):
        rows = slice(lo, lo + n)
        mid = slice(SUBLANES + lo, SUBLANES + lo + n)
        h = norm(x_ref[rows, :])
        he_ref[mid, :] = h
        hb_ref[rows, :] = h.astype(BF16)
        yield
        hm = he_ref[SUBLANES + lo - 1:SUBLANES + lo - 1 + n, :]
        hp = he_ref[SUBLANES + lo + 1:SUBLANES + lo + 1 + n, :]
        xx_ref[rows, :] = (0.5 * (hm + hp) - he_ref[mid, :]).astype(BF16)
        bf16_rows = 2 * SUBLANES

        def mixed(i):
            mu_tile = jnp.broadcast_to(mu_ref[i:i + 1, :], (bf16_rows, d)).astype(BF16)
            return hb_ref[rows, :] + xx_ref[rows, :] * jnp.tile(mu_tile, (n // bf16_rows, 1))

        wl = jnp.tanh(jnp.dot(mixed(1), w1_ref[...], preferred_element_type=F32))
        wl = jnp.dot(wl.astype(BF16), w2_ref[...], preferred_element_type=F32)
        for direction, lw_o in enumerate((lw0_out, lw1_out)):
            u = w0_ref[direction:direction + 1, :] + wl[:, direction * d:(direction + 1) * d]
            lw_o[rows, :] = -_sigmoid(u) * math.exp(-0.5)
        yield
        al = jnp.dot(mixed(4), a1_ref[...], preferred_element_type=F32)
        al = jnp.dot(al.astype(BF16), a2_ref[...], preferred_element_type=F32)
        a_dir = [_sigmoid(a0_ref[direction:direction + 1, :] + al[:, direction * d:(direction + 1) * d])
                 for direction in range(2)]
        for a, a_o in zip(a_dir, (a0_out, a1_out)):
            a_o[rows, :] = a.astype(BF16)
        yield
        gl = _sigmoid(jnp.dot(mixed(5), g1_ref[...], preferred_element_type=F32))
        g_out[rows, :] = jnp.dot(gl.astype(BF16), g2_ref[...], preferred_element_type=F32).astype(BF16)
        yield
        k = jnp.dot(mixed(2), wk_ref[...], preferred_element_type=F32)
        k_out[rows, :] = k.astype(BF16)
        kk = k * kk_w_ref[...]
        kk_out[rows, :] = (kk * jnp.minimum(lax.rsqrt(_head_sum(kk * kk, ones_bd)), 1e12)).astype(BF16)
        k_mean = 0.5 * (_k_dir(k, a_dir[0], ka_ref[...]) + _k_dir(k, a_dir[1], ka_ref[...]))
        yield
        r = jnp.dot(mixed(0), wr_ref[...], preferred_element_type=F32)
        r_out[rows, :] = r.astype(BF16)
        bonus_scale = _head_sum(r * k_mean * rk_ref[...], ones_bd)
        yield
        v = jnp.dot(mixed(3), wv_ref[...], preferred_element_type=F32)
        v_out[rows, :] = v.astype(BF16)
        bonus_out[rows, :] = (bonus_scale * v).astype(BF16)

    parts = 2
    _emit_interleaved(*[row_part(p * (tm // parts), tm // parts) for p in range(parts)])


def _k_dir(k, a, k_a):
    return k * (1.0 + (a - 1.0) * k_a)


def _blockdiag2(m):
    z = jnp.zeros_like(m[0])
    return jnp.concatenate([jnp.concatenate([m[0], z], axis=1), jnp.concatenate([z, m[1]], axis=1)], axis=0)


def _rwkv_pre(x2, mods, p, seq, tm):
    rows, d = x2.shape
    tpb = seq // tm
    bf = lambda a: a.astype(BF16)
    w1 = bf(jnp.concatenate([p['w1'][0], p['w1'][1]], axis=1))
    w2 = bf(_blockdiag2(p['w2']))
    a1 = bf(jnp.concatenate([p['a1'][0], p['a1'][1]], axis=1))
    a2 = bf(_blockdiag2(p['a2']))
    glora = p['g1'].shape[1]
    gpad = -glora % LANES
    g1 = bf(jnp.pad(p['g1'], ((0, 0), (0, gpad))))
    g2 = bf(jnp.pad(p['g2'], ((0, gpad), (0, 0))))
    row = lambda a: a.reshape(1, d)
    args = (x2, x2, x2, mods, row(p['norm1']), p['mu'], bf(p['w_r']), bf(p['w_k']), bf(p['w_v']),
            w1, w2, p['w0'], a1, a2, p['a0'], g1, g2, row(p['k_k']), row(p['k_a']), row(p['r_k']))
    in_specs = _halo_specs(tm, d, rows) + [pl.BlockSpec((1, 6, d), lambda i: (i // tpb, 0, 0))]
    in_specs += [_const_spec(a.shape) for a in args[4:]]
    ob = pl.BlockSpec((tm, d), lambda i: (i, 0))
    return pl.pallas_call(
        functools.partial(_rwkv_pre_kernel, tiles_per_seq=tpb),
        grid=(rows // tm,),
        in_specs=in_specs,
        out_specs=[ob] * 10,
        out_shape=[jax.ShapeDtypeStruct((rows, d), BF16)] * 6 + [jax.ShapeDtypeStruct((rows, d), F32)] * 2
        + [jax.ShapeDtypeStruct((rows, d), BF16)] * 2,
        scratch_shapes=[pltpu.VMEM((tm + 2 * SUBLANES, d), F32), pltpu.VMEM((tm, d), BF16),
                        pltpu.VMEM((tm, d), BF16)],
        compiler_params=_cparams(("parallel",)),
        name="rwkv_pre",
    )(*args)


def _scan_masks(reverse):
    c = SCAN_C
    assert c == RWKV_HEAD
    t = lax.broadcasted_iota(jnp.int32, (c, 4 * c), 0)
    s = lax.broadcasted_iota(jnp.int32, (c, 4 * c), 1) % c
    before = (s > t) if reverse else (s < t)
    upto = jnp.logical_or(before, s == t)
    eye = jnp.where(s == t, 1.0, 0.0)
    rr = lax.broadcasted_iota(jnp.int32, (4 * c, SCAN_HG), 0)
    cc = lax.broadcasted_iota(jnp.int32, (4 * c, SCAN_HG), 1)
    same_head = jnp.where((rr // c) == (cc // RWKV_HEAD), 1.0, 0.0)
    return before, upto, eye, same_head, same_head.astype(BF16)


def _bd(xb, head_mask_bf):
    return jnp.concatenate([xb] * 4, axis=0) * head_mask_bf


def _bd_t(x, head_mask_bf):
    xt = jnp.concatenate([x, x], axis=0).T.astype(BF16)
    return jnp.concatenate([xt, xt], axis=1) * head_mask_bf


def _mm(a, b):
    return jnp.dot(a, b, precision=HIGHEST, preferred_element_type=F32)


def _bdot(a, b):
    return jnp.dot(a, b, preferred_element_type=F32)


def _cumsum_rows(tri_bf, x):
    hi = x.astype(BF16)
    lo = (x - hi.astype(F32)).astype(BF16)
    return _bdot(tri_bf, hi) + _bdot(tri_bf, lo)


def _scan_prologue(chains):
    c = SCAN_C
    bf = lambda x: x.astype(BF16)
    idx = range(len(chains))
    r, v, kk, lw, k, a = [[ch[0][j] for ch in chains] for j in range(6)]
    head_mask_bf = chains[0][1][5]
    rev = [ch[2] for ch in chains]
    g = [_cumsum_rows(chains[i][1][0], lw[i]) for i in idx]
    gtot = [g[i][0:1] if rev[i] else g[i][c - 1:c] for i in idx]
    kkd = [kk[i] * jnp.exp(g[i] - lw[i]) for i in idx]
    rd = [r[i] * jnp.exp(g[i]) for i in idx]
    b = [kk[i] * a[i] for i in idx]
    ginv = [jnp.exp(-g[i]) for i in idx]
    lhs = [bf(jnp.concatenate([kkd[i], rd[i]], axis=0)) for i in idx]
    bi_t = [_bd_t(b[i] * ginv[i], head_mask_bf) for i in idx]
    ki_t = [_bd_t(k[i] * ginv[i], head_mask_bf) for i in idx]
    gend = [jnp.exp(gtot[i] - g[i]) for i in idx]
    upd_t = [bf(jnp.concatenate([b[i] * gend[i], k[i] * gend[i]], axis=0).T) for i in idx]
    gcol_src = [jnp.concatenate([g[i], g[i]], axis=0).T for i in idx]
    gcol = [jnp.exp(gcol_src[i][:, 0:1] if rev[i] else gcol_src[i][:, c - 1:c]) for i in idx]
    vb = [bf(v[i]) for i in idx]
    return chains, (vb, lhs, bi_t, ki_t, upd_t, gcol)


def _scan_prepare_stages(prologue, out):
    c = SCAN_C
    bf = lambda x: x.astype(BF16)
    chains, (vb, lhs, bi_t, ki_t, upd_t, gcol) = prologue
    idx = range(len(chains))
    before, upto, eye = [[ch[1][j] for ch in chains] for j in range(1, 4)]
    head_mask_bf = chains[0][1][5]

    a1 = [_bdot(lhs[i], bi_t[i]) for i in idx]
    a2 = [_bdot(lhs[i], ki_t[i]) for i in idx]
    n_ab = [jnp.where(before[i], a1[i][:c], 0.0) for i in idx]
    a_rb = [jnp.where(upto[i], a1[i][c:], 0.0) for i in idx]
    a_k = [jnp.concatenate([jnp.where(before[i], a2[i][:c], 0.0), jnp.where(upto[i], a2[i][c:], 0.0)], axis=0)
           for i in idx]
    yield

    nb = [bf(n_ab[i]) for i in idx]
    npow = [_bdot(nb[i], _bd(nb[i], head_mask_bf)) for i in idx]
    tinv = [eye[i] - n_ab[i] for i in idx]
    yield
    steps = int(math.log2(c)) - 1
    for step in range(steps):
        pb = [bf(npow[i]) for i in idx]
        rhs = [_bd(pb[i], head_mask_bf) for i in idx]
        if step == steps - 1:
            tinv = [tinv[i] + _bdot(bf(tinv[i]), rhs[i]) for i in idx]
        else:
            prod = [_bdot(jnp.concatenate([pb[i], bf(tinv[i])], axis=0), rhs[i]) for i in idx]
            tinv = [tinv[i] + prod[i][c:] for i in idx]
            npow = [prod[i][:c] for i in idx]
        yield

    av = [_bdot(bf(a_k[i]), _bd(vb[i], head_mask_bf)) for i in idx]
    yield
    rhs = [jnp.concatenate([_bd(lhs[i][:c], head_mask_bf), _bd(bf(av[i][:c]), head_mask_bf)], axis=1) for i in idx]
    tw = [_bdot(bf(tinv[i]), rhs[i]) for i in idx]
    lhs_state = [jnp.concatenate([lhs[i][c:], bf(tw[i][:, :SCAN_HG])], axis=0) for i in idx]
    wt = [tw[i][:, SCAN_HG:] for i in idx]
    out.extend((lhs_state[i], wt[i], bf(a_rb[i]), av[i][c:], upd_t[i], gcol[i], vb[i]) for i in idx)


def _scan_apply_stages(preps, states, head_mask, head_mask_bf, emit_y):
    c = SCAN_C
    for pos in range(0, len(preps), 2):
        lhs, wt, a_rb, av_r, upd_t, gcol, vb = zip(*preps[pos:pos + 2])
        o = [_bdot(lhs[i], states[i].astype(BF16)) for i in range(2)]
        yield
        ub = [(-(wt[i] + o[i][c:])).astype(BF16) for i in range(2)]
        upd = [_bdot(upd_t[i], jnp.concatenate([ub[i], vb[i]], axis=0)) for i in range(2)]
        y = [o[i][:c] + av_r[i] + _bdot(a_rb[i], _bd(ub[i], head_mask_bf)) for i in range(2)]
        yield
        for i in range(2):
            emit_y(pos + i, y[i])
            states[i] = states[i] * gcol[i] + head_mask * upd[i]


def _emit_interleaved(*stage_generators):
    pending = list(stage_generators)
    while pending:
        for gen in list(pending):
            try:
                next(gen)
            except StopIteration:
                pending.remove(gen)


def _scan_consts(reverse):
    c = SCAN_C
    before, upto, eye, head_mask, head_mask_bf = _scan_masks(reverse)
    tt = lax.broadcasted_iota(jnp.int32, (c, c), 0)
    ss = lax.broadcasted_iota(jnp.int32, (c, c), 1)
    tri = jnp.where((ss >= tt) if reverse else (ss <= tt), 1.0, 0.0).astype(BF16)
    return tri, before, upto, eye, head_mask, head_mask_bf


def _scan_kernel(ka_ref, rf_ref, vf_ref, kkf_ref, kf_ref, af_ref, lwf_ref, rb_ref, vb_ref, kkb_ref, kb_ref, ab_ref,
                 lwb_ref, yf_ref, yb_ref, sf_ref, sb_ref):
    tb = rf_ref.shape[0]
    nchunk = tb // SCAN_C

    @pl.when(pl.program_id(2) == 0)
    def _():
        sf_ref[...] = jnp.zeros_like(sf_ref)
        sb_ref[...] = jnp.zeros_like(sb_ref)

    consts_f = _scan_consts(False)
    consts_b = _scan_consts(True)
    head_mask = consts_f[4]
    k_a = ka_ref[...]

    def operands(refs, rows):
        r, v, kk, k, a = (ref[rows, :].astype(F32) for ref in refs[:5])
        return r, v, kk, refs[5][rows, :], _k_dir(k, a, k_a), a

    fwd_refs = (rf_ref, vf_ref, kkf_ref, kf_ref, af_ref, lwf_ref)
    bwd_refs = (rb_ref, vb_ref, kkb_ref, kb_ref, ab_ref, lwb_ref)

    def emitter(rows):
        def emit_y(pos, y):
            ref = yb_ref if pos % 2 else yf_ref
            ref[rows[pos], :] = y.astype(ref.dtype)
        return emit_y

    def group_rows(group):
        rows = []
        for u in range(SCAN_UNROLL):
            j = group * SCAN_UNROLL + u
            rows += [pl.ds(j * SCAN_C, SCAN_C), pl.ds((nchunk - 1 - j) * SCAN_C, SCAN_C)]
        return rows

    def group_prologue(rows):
        chains = []
        for pos, rs in enumerate(rows):
            reverse = pos % 2 == 1
            chains.append((operands(bwd_refs if reverse else fwd_refs, rs), consts_b if reverse else consts_f,
                           reverse))
        return _scan_prologue(chains)

    def run_once(fn, box, delay=4):
        for _ in range(delay):
            yield
        box.append(fn())
        yield

    ngroups = nchunk // SCAN_UNROLL
    states = [sf_ref[...], sb_ref[...]]
    pending = None
    rows = group_rows(0)
    prologue = group_prologue(rows)
    for group in range(ngroups):
        preps, next_box = [], []
        stages = [_scan_prepare_stages(prologue, preps)]
        if pending is not None:
            stages.append(_scan_apply_stages(pending[0], states, head_mask, consts_f[5], emitter(pending[1])))
        if group + 1 < ngroups:
            next_rows = group_rows(group + 1)
            stages.append(run_once(functools.partial(group_prologue, next_rows), next_box))
        _emit_interleaved(*stages)
        pending = (preps, rows)
        if group + 1 < ngroups:
            rows, prologue = next_rows, next_box[0]
    _emit_interleaved(_scan_apply_stages(pending[0], states, head_mask, consts_f[5], emitter(pending[1])))
    sf_ref[...] = states[0]
    sb_ref[...] = states[1]


def _wkv_scan(r, v, kk, k, a0, a1, lw0, lw1, k_a, batch, seq, tb):
    rows, d = r.shape
    nt = seq // tb
    ng = d // SCAN_HG
    fwd = pl.BlockSpec((tb, SCAN_HG), lambda b, g, t: (b * nt + t, g))
    bwd = pl.BlockSpec((tb, SCAN_HG), lambda b, g, t: (b * nt + nt - 1 - t, g))
    return pl.pallas_call(
        _scan_kernel,
        grid=(batch, ng, nt),
        in_specs=[pl.BlockSpec((1, SCAN_HG), lambda b, g, t: (0, g))] + [fwd] * 6 + [bwd] * 6,
        out_specs=[fwd, bwd],
        out_shape=[jax.ShapeDtypeStruct((rows, d), BF16)] * 2,
        scratch_shapes=[pltpu.VMEM((SCAN_HG, SCAN_HG), F32)] * 2,
        compiler_params=_cparams(("parallel", "parallel", "arbitrary")),
        name="wkv_scan",
    )(k_a.reshape(1, d), r, v, kk, k, a0, lw0, r, v, kk, k, a1, lw1)


def _rwkv_post_kernel(yf_ref, yb_ref, bonus_ref, g_ref, x_ref, mod_ref, lnw_ref, lnb_ref, wo_ref, out_ref):
    tm = x_ref.shape[0]
    ones_bd = _ones_blockdiag()
    halves = [slice(p * (tm // 2), (p + 1) * (tm // 2)) for p in range(2)]
    y = [yf_ref[rows, :].astype(F32) + yb_ref[rows, :].astype(F32) for rows in halves]
    mu = [_head_sum(y[p], ones_bd) * (1.0 / RWKV_HEAD) for p in range(2)]
    yc = [y[p] - mu[p] for p in range(2)]
    var = [_head_sum(yc[p] * yc[p], ones_bd) * (1.0 / RWKV_HEAD) for p in range(2)]
    yn = [yc[p] * lax.rsqrt(var[p] + GN_EPS) * lnw_ref[...] + lnb_ref[...] for p in range(2)]
    out = [((yn[p] + bonus_ref[rows, :].astype(F32)) * g_ref[rows, :].astype(F32)).astype(BF16)
           for p, rows in enumerate(halves)]
    mix = [jnp.dot(out[p], wo_ref[...], preferred_element_type=F32) for p in range(2)]
    for p, rows in enumerate(halves):
        out_ref[rows, :] = x_ref[rows, :] + mod_ref[0, 2:3, :] * mix[p]


def _rwkv_post(yf, yb, bonus, g, x2, mods, ln_w, ln_b, wo_bf, seq, tm):
    rows, d = x2.shape
    tpb = seq // tm
    rb = pl.BlockSpec((tm, d), lambda i: (i, 0))
    return pl.pallas_call(
        _rwkv_post_kernel,
        grid=(rows // tm,),
        in_specs=[rb] * 5 + [pl.BlockSpec((1, 6, d), lambda i: (i // tpb, 0, 0)),
                             _const_spec((1, d)), _const_spec((1, d)), _const_spec((d, d))],
        out_specs=rb,
        out_shape=jax.ShapeDtypeStruct((rows, d), F32),
        compiler_params=_cparams(("parallel",)),
        name="rwkv_post",
    )(yf, yb, bonus, g, x2, mods, ln_w.reshape(1, d), ln_b.reshape(1, d), wo_bf)


def _mods(c, w, b):
    batch, d = c.shape
    pad = -batch % SUBLANES
    m = _ada_params(jnp.pad(c, ((0, pad), (0, 0))), w, b)[:batch]
    return m.reshape(batch, 6, d)


def _group_major_w_in(w_in):
    d = w_in.shape[0]
    qkv = w_in[:, :3 * ATTN_WIDTH].reshape(d, 3, N_PATTERNS, GROUP_WIDTH)
    qkv = jnp.transpose(qkv, (0, 2, 1, 3)).reshape(d, 3 * ATTN_WIDTH)
    return jnp.concatenate([qkv, w_in[:, 3 * ATTN_WIDTH:]], axis=1)


def _trunk(xa, xb, c, l0, l1, final_norm):
    seq, d = xa.shape[1:]
    assert xb.shape[1:] == (seq, d)
    batch = xa.shape[0] + xb.shape[0]
    xa2, xb2 = xa.reshape(-1, d), xb.reshape(-1, d)
    bf = lambda a: a.astype(BF16)
    tm = 512

    mods0 = _mods(c, l0['ada_w'], l0['ada_b'])
    q0, q1, q2, hy = _in_proj(xa2, xb2, mods0, l0['norm1'], bf(_group_major_w_in(l0['w_in'])), seq, tm)
    os_, ls = [], []
    for group, qkv in enumerate((q0.reshape(batch, 1, seq, 3 * GROUP_WIDTH), q1, q2)):
        o, lse = _attention_group(qkv, group)
        os_.append(o)
        ls.append(lse)
    ft = _hyena_filter(seq, l0['filt_w1'], l0['filt_b1'], l0['filt_w2'], l0['filt_b2'], l0['filt_w3'],
                       l0['filt_b3'], l0['filt_w4'], l0['filt_freq'])
    zt, x0t = _hyena_pre(hy.reshape(batch, seq, 3 * HYENA_WIDTH), l0['short_w'], l0['short_b'], tl=512)
    hyt = _hyena_conv(zt, x0t, ft, l0['filt_bias'], cb=8)
    x2 = _out_proj(os_, ls, hyt, xa2, xb2, mods0, bf(l0['w_out']), seq, tm)
    x2 = _conv_ffn(x2, mods0, l0['norm2'], bf(l0['ffn_up']), l0['ffn_conv_w'], l0['ffn_conv_b'],
                   bf(l0['ffn_down']), final_norm, seq, tm, final_norm=False)

    mods1 = _mods(c, l1['ada_w'], l1['ada_b'])
    r, v, kk, k, a0, a1, lw0, lw1, g, bonus = _rwkv_pre(x2, mods1, l1, seq, tm=512)
    yf, yb = _wkv_scan(r, v, kk, k, a0, a1, lw0, lw1, l1['k_a'], batch, seq, tb=1024)
    x2 = _rwkv_post(yf, yb, bonus, g, x2, mods1, l1['ln_w'], l1['ln_b'], bf(l1['w_o']), seq, tm)
    ya, yb_ = _conv_ffn(x2, mods1, l1['norm2'], bf(l1['ffn_up']), l1['ffn_conv_w'], l1['ffn_conv_b'],
                        bf(l1['ffn_down']), final_norm, seq, tm, final_norm=True, split_rows=xa2.shape[0])
    return ya.reshape(xa.shape), yb_.reshape(xb.shape)


def kernel(x_prompt, x_sample, c_prompt, c_sample, l0_ada_w, l0_ada_b, l0_norm1, l0_norm2, l0_w_in, l0_short_w, l0_short_b, l0_filt_w1, l0_filt_b1, l0_filt_w2, l0_filt_b2, l0_filt_w3, l0_filt_b3, l0_filt_w4, l0_filt_freq, l0_filt_bias, l0_w_out, l0_ffn_up, l0_ffn_conv_w, l0_ffn_conv_b, l0_ffn_down, l1_ada_w, l1_ada_b, l1_norm1, l1_norm2, l1_mu, l1_w_r, l1_w_k, l1_w_v, l1_w_o, l1_w0, l1_w1, l1_w2, l1_a0, l1_a1, l1_a2, l1_g1, l1_g2, l1_k_k, l1_k_a, l1_r_k, l1_ln_w, l1_ln_b, l1_ffn_up, l1_ffn_conv_w, l1_ffn_conv_b, l1_ffn_down, final_norm):
    layer0 = dict(ada_w=l0_ada_w, ada_b=l0_ada_b, norm1=l0_norm1, norm2=l0_norm2, w_in=l0_w_in,
                  short_w=l0_short_w, short_b=l0_short_b, filt_w1=l0_filt_w1, filt_b1=l0_filt_b1,
                  filt_w2=l0_filt_w2, filt_b2=l0_filt_b2, filt_w3=l0_filt_w3, filt_b3=l0_filt_b3,
                  filt_w4=l0_filt_w4, filt_freq=l0_filt_freq, filt_bias=l0_filt_bias, w_out=l0_w_out,
                  ffn_up=l0_ffn_up, ffn_conv_w=l0_ffn_conv_w, ffn_conv_b=l0_ffn_conv_b, ffn_down=l0_ffn_down)
    layer1 = dict(ada_w=l1_ada_w, ada_b=l1_ada_b, norm1=l1_norm1, norm2=l1_norm2, mu=l1_mu,
                  w_r=l1_w_r, w_k=l1_w_k, w_v=l1_w_v, w_o=l1_w_o, w0=l1_w0, w1=l1_w1, w2=l1_w2,
                  a0=l1_a0, a1=l1_a1, a2=l1_a2, g1=l1_g1, g2=l1_g2, k_k=l1_k_k, k_a=l1_k_a,
                  r_k=l1_r_k.reshape(-1), ln_w=l1_ln_w, ln_b=l1_ln_b, ffn_up=l1_ffn_up,
                  ffn_conv_w=l1_ffn_conv_w, ffn_conv_b=l1_ffn_conv_b, ffn_down=l1_ffn_down)
    c = jnp.concatenate([c_prompt, c_sample], axis=0)
    return _trunk(x_prompt, x_sample, c, layer0, layer1, final_norm)
```
